```python
import jax, jax.numpy as jnp
from jax import lax
import numpy as np


D_MODEL = 1024
BATCH = 8
SEQ = 4096
DEPTH = 2

EPS = 1e-6
HG_HEADS = 4
HG_HEAD_DIM = 128
HG_DIM = HG_HEADS * HG_HEAD_DIM
GLA_HEADS = 4
GLA_DK = 64
GLA_DV = 128
GLA_KDIM = GLA_HEADS * GLA_DK
GLA_VDIM = GLA_HEADS * GLA_DV
GLA_RANK = 16
GLA_TAU = 16.0
LA_CHUNK = 64
NSA_HEADS = 8
NSA_KV_GROUPS = 2
NSA_REP = NSA_HEADS // NSA_KV_GROUPS
NSA_DH = 64
NSA_DIM = NSA_HEADS * NSA_DH
NSA_KV_DIM = NSA_KV_GROUPS * NSA_DH
CMP_STRIDE = 16
CMP_BLK = 2 * CMP_STRIDE
CMP_HID = 128
SLC_BLK = 64
N_SEL = 16
WIN = 512
Q_BLOCK = 128
ROPE_THETA = 500000.0
ROT_DIM = NSA_DH // 4
N_BRANCH = 3
FF_DIM = ((8 * D_MODEL // 3 + 255) // 256) * 256

IN_SPLITS = (HG_DIM, HG_DIM, HG_DIM, HG_DIM,
             GLA_KDIM, GLA_KDIM, GLA_VDIM, GLA_RANK, GLA_VDIM,
             NSA_DIM, NSA_KV_DIM, NSA_KV_DIM, NSA_KV_DIM, NSA_KV_DIM, NSA_KV_DIM, NSA_KV_DIM,
             3 * NSA_HEADS,
             N_BRANCH * D_MODEL)
IN_COLS = sum(IN_SPLITS)

kernel_name = 'hybrid_hgrn2_gla_nsa_block'


def rmsnorm(x, g):
    x32 = x.astype(jnp.float32)
    y = x32 * lax.rsqrt(jnp.mean(x32 * x32, axis=-1, keepdims=True) + EPS)
    return y.astype(x.dtype) * g


def split_cols(z):
    outs = []
    start = 0
    for n in IN_SPLITS:
        outs.append(z[..., start:start + n])
        start += n
    return outs


def heads(z, n):
    B, T, _ = z.shape
    return z.reshape(B, T, n, -1).transpose(0, 2, 1, 3)


def head_rmsnorm(o, g, dtype):
    B, H, T, dv = o.shape
    o = o.astype(jnp.float32).transpose(0, 2, 1, 3)
    o = o * lax.rsqrt(jnp.mean(o * o, axis=-1, keepdims=True) + EPS)
    return o.reshape(B, T, H * dv).astype(dtype) * g


def chunked_gated_linear_attention(q, k, v, log_g):
    B, H, T, dk = q.shape
    dv = v.shape[-1]
    n = T // LA_CHUNK

    def to_chunks(a):
        return jnp.moveaxis(a.astype(jnp.float32).reshape(B, H, n, LA_CHUNK, a.shape[-1]), 2, 0)

    causal = jnp.tril(jnp.ones((LA_CHUNK, LA_CHUNK), dtype=bool))[:, :, None]

    def step(state, inp):
        qc, kc, vc, gc = inp
        b = jnp.cumsum(gc, axis=2)
        o_inter = jnp.einsum('bhtd,bhdv->bhtv', qc * jnp.exp(b), state)
        decay = jnp.exp(jnp.where(causal, b[:, :, :, None, :] - b[:, :, None, :, :], -jnp.inf))
        scores = jnp.einsum('bhtd,bhtsd,bhsd->bhts', qc, decay, kc)
        o = o_inter + jnp.einsum('bhts,bhsv->bhtv', scores, vc)
        b_last = b[:, :, -1:, :]
        state = (jnp.exp(b_last[:, :, 0, :, None]) * state
                 + jnp.einsum('bhsd,bhsv->bhdv', kc * jnp.exp(b_last - b), vc))
        return state, o

    s0 = jnp.zeros((B, H, dk, dv), jnp.float32)
    _, o = lax.scan(step, s0, (to_chunks(q), to_chunks(k), to_chunks(v), to_chunks(log_g)))
    return jnp.moveaxis(o, 0, 2).reshape(B, H, T, dv)


def partial_rope(x, cos, sin):
    half = ROT_DIM // 2
    c = cos[None, :, None, :].astype(x.dtype)
    s = sin[None, :, None, :].astype(x.dtype)
    x1 = x[..., :half]
    x2 = x[..., half:ROT_DIM]
    return jnp.concatenate([x1 * c - x2 * s, x2 * c + x1 * s, x[..., ROT_DIM:]], axis=-1)


def masked_softmax(s, mask):
    s = jnp.where(mask, s.astype(jnp.float32), -1e30)
    return jnp.where(mask, jax.nn.softmax(s, axis=-1), 0.0)


def compress(k, pos, w1, w2):
    B, G, T, dh = k.shape
    c = k.reshape(B, G, T // CMP_STRIDE, CMP_STRIDE, dh)
    blocks = jnp.concatenate([c[:, :, :-1], c[:, :, 1:]], axis=3) + pos
    flat = blocks.reshape(B, G, blocks.shape[2], CMP_BLK * dh)
    return jax.nn.gelu(flat @ w1) @ w2


def nsa_attention(q, kc, vc, ks, vs, kw, vw, gates):
    B, T, H, dh = q.shape
    G, R = NSA_KV_GROUPS, NSA_REP
    ncmp = kc.shape[2]
    nslc = T // SLC_BLK
    n_sel = min(N_SEL, nslc)
    nq = T // Q_BLOCK
    cmp_start = jnp.arange(ncmp) * CMP_STRIDE
    cmp_end = cmp_start + CMP_BLK - 1
    blk = jnp.arange(nslc)
    overlap = ((cmp_start[:, None] < (blk[None, :] + 1) * SLC_BLK)
               & (cmp_end[:, None] >= blk[None, :] * SLC_BLK)).astype(jnp.float32)
    ks_b = ks.reshape(B, G, nslc, SLC_BLK, dh)
    vs_b = vs.reshape(B, G, nslc, SLC_BLK, dh)
    kw_pad = jnp.pad(kw, ((0, 0), (0, 0), (WIN, 0), (0, 0)))
    vw_pad = jnp.pad(vw, ((0, 0), (0, 0), (WIN, 0), (0, 0)))
    b_idx = jnp.arange(B)[:, None, None, None]
    g_idx = jnp.arange(G)[None, :, None, None]
    q_blocks = jnp.moveaxis(q.reshape(B, nq, Q_BLOCK, G, R, dh), 1, 0).transpose(0, 1, 3, 4, 2, 5)

    def block_fn(args):
        c, qb = args
        t = c * Q_BLOCK + jnp.arange(Q_BLOCK)
        mask_c = cmp_end[None, :] <= t[:, None]
        p_c = masked_softmax(jnp.einsum('bgrqd,bgnd->bgrqn', qb, kc), mask_c)
        o_c = jnp.einsum('bgrqn,bgnd->bgrqd', p_c.astype(vc.dtype), vc)
        imp = jnp.einsum('bgrqn,nj->bgqj', p_c, overlap)
        cur = t // SLC_BLK
        valid = blk[None, :] <= cur[:, None]
        forced = (blk[None, :] == 0) | (blk[None, :] == cur[:, None]) | (blk[None, :] == cur[:, None] - 1)
        score = jnp.where(valid & forced, 1e9, jnp.where(valid, imp, -1e9))
        top_s, top_i = lax.top_k(score, n_sel)
        k_sel = ks_b[b_idx, g_idx, top_i].reshape(B, G, Q_BLOCK, n_sel * SLC_BLK, dh)
        v_sel = vs_b[b_idx, g_idx, top_i].reshape(B, G, Q_BLOCK, n_sel * SLC_BLK, dh)
        tok = (top_i[..., None] * SLC_BLK + jnp.arange(SLC_BLK)).reshape(B, G, Q_BLOCK, n_sel * SLC_BLK)
        mask_s = (tok <= t[None, None, :, None]) & jnp.repeat(top_s > -1e8, SLC_BLK, axis=-1)
        p_s = masked_softmax(jnp.einsum('bgrqd,bgqmd->bgrqm', qb, k_sel), mask_s[:, :, None])
        o_s = jnp.einsum('bgrqm,bgqmd->bgrqd', p_s.astype(v_sel.dtype), v_sel)
        k_win = lax.dynamic_slice_in_dim(kw_pad, c * Q_BLOCK, WIN + Q_BLOCK, axis=2)
        v_win = lax.dynamic_slice_in_dim(vw_pad, c * Q_BLOCK, WIN + Q_BLOCK, axis=2)
        kp = c * Q_BLOCK - WIN + jnp.arange(WIN + Q_BLOCK)
        mask_w = (kp[None, :] <= t[:, None]) & (kp[None, :] > t[:, None] - WIN) & (kp[None, :] >= 0)
        p_w = masked_softmax(jnp.einsum('bgrqd,bgkd->bgrqk', qb, k_win), mask_w)
        o_w = jnp.einsum('bgrqk,bgkd->bgrqd', p_w.astype(v_win.dtype), v_win)
        return jnp.stack([o_c, o_s, o_w], axis=-1)

    out = lax.map(block_fn, (jnp.arange(nq), q_blocks))
    out = out.transpose(1, 0, 4, 2, 3, 5, 6).reshape(B, T, H, dh, 3)
    return jnp.einsum('bthdk,bthk->bthd', out, gates).reshape(B, T, H * dh)


def hybrid_mixer(h, lb, w_in, hg_norm_g, gla_w2, gla_b, gla_norm_g, cmp_pos_k, cmp_pos_v,
                 cmp_w1_k, cmp_w2_k, cmp_w1_v, cmp_w2_v, w_br_hg, w_br_gla, w_br_nsa, w_out, cos, sin):
    B, T, _ = h.shape
    (hq, hf, hi, hog, gq, gk, gv, glr, gog,
     nq_, nkc, nvc, nks, nvs, nkw, nvw, ngate, mgate) = split_cols(h @ w_in)
    lbh = lb.reshape(1, HG_HEADS, 1, HG_HEAD_DIM)
    log_f = jnp.logaddexp(jnp.log(lbh), jnp.log1p(-lbh)
                          + jax.nn.log_sigmoid(heads(hf, HG_HEADS).astype(jnp.float32)))
    o_hg = chunked_gated_linear_attention(jax.nn.silu(heads(hq, HG_HEADS)), -jnp.expm1(log_f),
                                          heads(hi, HG_HEADS), log_f)
    y_hg = head_rmsnorm(o_hg, hg_norm_g, h.dtype) * jax.nn.silu(hog)
    log_a = jax.nn.log_sigmoid((glr @ gla_w2 + gla_b).astype(jnp.float32)) / GLA_TAU
    o_gla = chunked_gated_linear_attention(heads(gq, GLA_HEADS) * GLA_DK ** -0.5, heads(gk, GLA_HEADS),
                                           heads(gv, GLA_HEADS), heads(log_a, GLA_HEADS))
    y_gla = head_rmsnorm(o_gla, gla_norm_g, h.dtype) * jax.nn.silu(gog)
    q = partial_rope(nq_.reshape(B, T, NSA_HEADS, NSA_DH), cos, sin) * NSA_DH ** -0.5

    def kv(z):
        return z.reshape(B, T, NSA_KV_GROUPS, NSA_DH)

    def to_g(z):
        return z.transpose(0, 2, 1, 3)

    kc = compress(to_g(partial_rope(kv(nkc), cos, sin)), cmp_pos_k, cmp_w1_k, cmp_w2_k)
    vc = compress(to_g(kv(nvc)), cmp_pos_v, cmp_w1_v, cmp_w2_v)
    ks = to_g(partial_rope(kv(nks), cos, sin))
    vs = to_g(kv(nvs))
    kw = to_g(partial_rope(kv(nkw), cos, sin))
    vw = to_g(kv(nvw))
    gates = jax.nn.sigmoid(ngate).reshape(B, T, NSA_HEADS, 3)
    y_nsa = nsa_attention(q, kc, vc, ks, vs, kw, vw, gates)
    g = jax.nn.sigmoid(mgate).reshape(B, T, N_BRANCH, D_MODEL)
    m = (g[:, :, 0] * (y_hg @ w_br_hg) + g[:, :, 1] * (y_gla @ w_br_gla)
         + g[:, :, 2] * (y_nsa @ w_br_nsa))
    return m @ w_out


def swiglu(h, w_gate, w_up, w_down):
    return (jax.nn.silu(h @ w_gate) * (h @ w_up)) @ w_down


def setup_inputs(seed: int = 0) -> dict:
    key = jax.random.key(seed)
    k = jax.random.split(key, 24)
    L, D = DEPTH, D_MODEL

    def nrm(kk, shape, scale):
        return jax.random.normal(kk, shape, jnp.float32) * scale

    def gain(kk, shape):
        return 1.0 + 0.02 * jax.random.normal(kk, shape, jnp.float32)

    return {
        'x': nrm(k[0], (BATCH, SEQ, D), 1.0),
        'norm1_g': gain(k[1], (L, D)),
        'w_in': nrm(k[2], (L, D, IN_COLS), D ** -0.5),
        'hg_lb': nrm(k[3], (L, HG_DIM), 0.5),
        'hg_norm_g': gain(k[4], (L, HG_DIM)),
        'gla_w2': nrm(k[5], (L, GLA_RANK, GLA_KDIM), GLA_RANK ** -0.5),
        'gla_b': nrm(k[6], (L, GLA_KDIM), 0.1),
        'gla_norm_g': gain(k[7], (L, GLA_VDIM)),
        'cmp_pos_k': nrm(k[8], (L, CMP_BLK, NSA_DH), 0.1),
        'cmp_pos_v': nrm(k[9], (L, CMP_BLK, NSA_DH), 0.1),
        'cmp_w1_k': nrm(k[10], (L, CMP_BLK * NSA_DH, CMP_HID), (CMP_BLK * NSA_DH) ** -0.5),
        'cmp_w2_k': nrm(k[11], (L, CMP_HID, NSA_DH), CMP_HID ** -0.5),
        'cmp_w1_v': nrm(k[12], (L, CMP_BLK * NSA_DH, CMP_HID), (CMP_BLK * NSA_DH) ** -0.5),
        'cmp_w2_v': nrm(k[13], (L, CMP_HID, NSA_DH), CMP_HID ** -0.5),
        'w_br_hg': nrm(k[14], (L, HG_DIM, D), HG_DIM ** -0.5),
        'w_br_gla': nrm(k[15], (L, GLA_VDIM, D), GLA_VDIM ** -0.5),
        'w_br_nsa': nrm(k[16], (L, NSA_DIM, D), NSA_DIM ** -0.5),
        'w_out': nrm(k[17], (L, D, D), D ** -0.5),
        'norm2_g': gain(k[18], (L, D)),
        'w_ffn_gate': nrm(k[19], (L, D, FF_DIM), D ** -0.5),
        'w_ffn_up': nrm(k[20], (L, D, FF_DIM), D ** -0.5),
        'w_ffn_down': nrm(k[21], (L, FF_DIM, D), FF_DIM ** -0.5),
        'final_norm_g': gain(k[22], (D,)),
    }


def reference(x, norm1_g, w_in, hg_lb, hg_norm_g, gla_w2, gla_b, gla_norm_g, cmp_pos_k, cmp_pos_v,
              cmp_w1_k, cmp_w2_k, cmp_w1_v, cmp_w2_v, w_br_hg, w_br_gla, w_br_nsa, w_out,
              norm2_g, w_ffn_gate, w_ffn_up, w_ffn_down, final_norm_g):
    T = x.shape[1]
    pos = jnp.arange(T, dtype=jnp.float32)
    inv_freq = ROPE_THETA ** (-jnp.arange(0, ROT_DIM, 2, dtype=jnp.float32) / ROT_DIM)
    ang = pos[:, None] * inv_freq[None, :]
    cos, sin = jnp.cos(ang), jnp.sin(ang)
    lbs = jnp.cumsum(jax.nn.softmax(hg_lb.astype(jnp.float32), axis=0), axis=0)
    lbs = lbs - lbs[0]
    for l in range(DEPTH):
        h = rmsnorm(x, norm1_g[l])
        x = x + hybrid_mixer(h, lbs[l], w_in[l], hg_norm_g[l], gla_w2[l], gla_b[l], gla_norm_g[l],
                             cmp_pos_k[l], cmp_pos_v[l], cmp_w1_k[l], cmp_w2_k[l], cmp_w1_v[l], cmp_w2_v[l],
                             w_br_hg[l], w_br_gla[l], w_br_nsa[l], w_out[l], cos, sin)
        h = rmsnorm(x, norm2_g[l])
        x = x + swiglu(h, w_ffn_gate[l], w_ffn_up[l], w_ffn_down[l])
    return rmsnorm(x, final_norm_g)
```

```python
import functools

import jax
import jax.numpy as jnp
from jax import lax
from jax.experimental import pallas as pl
from jax.experimental.pallas import tpu as pltpu

EPS = 1e-6
D_MODEL = 1024
LANES = 128

HG_HEADS = 4
HG_DIM = 512
GLA_HEADS = 4
GLA_DK = 64
GLA_KDIM = 256
GLA_VDIM = 512
GLA_RANK = 16
GLA_TAU = 16.0
LA_CHUNK = 64
LA_SUB = 16

NSA_HEADS = 8
NSA_GROUPS = 2
NSA_REP = 4
NSA_DH = 64
CMP_STRIDE = 16
CMP_BLK = 32
CMP_HID = 128
SLC_BLK = 64
N_SEL = 16
WIN = 512
Q_BLOCK = 128
ROPE_THETA = 500000.0
ROT_DIM = 16
FF_DIM = 2816

C_HQ, C_HF, C_HI, C_HOG = 0, 512, 1024, 1536
C_GQ, C_GK, C_GV, C_GOG = 2048, 2560, 3072, 3584
C_NQ = 4096
C_NKC, C_NVC, C_NKS, C_NVS, C_NKW, C_NVW = 5120, 5248, 5376, 5504, 5632, 5760
C_NGATE, C_GLR = 5888, 6016
C_MG = 6144
Z_COLS = 9216

VMEM_LIMIT = 56 * 1024 * 1024

_NT = (((1,), (1,)), ((), ()))
_TN = (((0,), (0,)), ((), ()))


def _bf(x):
    return x.astype(jnp.bfloat16)


def _dot(a, b):
    return jnp.dot(a, b, preferred_element_type=jnp.float32)


def _dot_nt(a, b):
    return lax.dot_general(a, b, _NT, preferred_element_type=jnp.float32)


def _dot_tn(a, b):
    return lax.dot_general(a, b, _TN, preferred_element_type=jnp.float32)


def _sigmoid(x):
    return 1.0 / (1.0 + jnp.exp(-x))


def _log_sigmoid(x):
    return jnp.minimum(x, 0.0) - jnp.log1p(jnp.exp(-jnp.abs(x)))


def _inproj_kernel(x_ref, g_ref, w_ref, z_ref, h_scr):
    @pl.when(pl.program_id(1) == 0)
    def _():
        x = x_ref[...]
        y = x * lax.rsqrt(jnp.mean(x * x, axis=-1, keepdims=True) + EPS)
        h_scr[...] = _bf(y * g_ref[...])

    z_ref[...] = _dot(h_scr[...], w_ref[...])


def _inproj(x2, g, w, tm=512, tn=1024):
    n, d = x2.shape
    cols = w.shape[1]
    return pl.pallas_call(
        _inproj_kernel,
        grid=(n // tm, cols // tn),
        in_specs=[
            pl.BlockSpec((tm, d), lambda i, j: (i, 0)),
            pl.BlockSpec((1, d), lambda i, j: (0, 0)),
            pl.BlockSpec((d, tn), lambda i, j: (0, j)),
        ],
        out_specs=pl.BlockSpec((tm, tn), lambda i, j: (i, j)),
        out_shape=jax.ShapeDtypeStruct((n, cols), jnp.float32),
        scratch_shapes=[pltpu.VMEM((tm, d), jnp.bfloat16)],
        compiler_params=pltpu.CompilerParams(
            dimension_semantics=("parallel", "arbitrary"), vmem_limit_bytes=VMEM_LIMIT),
        name="inproj",
    )(x2, g, w)


def _chunk_cumsum(g, tb):
    pos = lax.broadcasted_iota(jnp.int32, (tb, 1), 0) % LA_CHUNK
    b = g
    shift = 1
    while shift < LA_CHUNK:
        b = b + jnp.where(pos >= shift, pltpu.roll(b, shift, axis=0), 0.0)
        shift *= 2
    return b


def _la_core(q, k, v, g, st_ref, tb):
    c, r = LA_CHUNK, LA_SUB
    n = tb // c
    nsub = c // r
    dk = q.shape[-1]
    b = _chunk_cumsum(g, tb)
    b3 = b.reshape(n, c, dk)
    q3 = q.reshape(n, c, dk)
    k3 = k.reshape(n, c, dk)
    v3 = _bf(v).reshape(n, c, dk)
    blast = b3[:, c - 1:c, :]
    q_state = _bf(q3 * jnp.exp(b3))
    k_state = _bf(k3 * jnp.exp(blast - b3))

    b4 = b.reshape(n * nsub, r, dk)
    ref = b4[:, 0:1, :]
    q_loc = _bf(q.reshape(n * nsub, r, dk) * jnp.exp(b4 - ref))
    s_idx = lax.broadcasted_iota(jnp.int32, (1, nsub, c, 1), 2)
    i_idx = lax.broadcasted_iota(jnp.int32, (1, nsub, c, 1), 1)
    e = jnp.where(s_idx < r * (i_idx + 1), ref.reshape(n, nsub, 1, dk) - b3[:, None, :, :], 0.0)
    k_loc = _bf(k3[:, None, :, :] * jnp.exp(e)).reshape(n * nsub, c, dk)
    a = jnp.einsum("utd,usd->uts", q_loc, k_loc, preferred_element_type=jnp.float32)
    a = a.reshape(n, c, c)
    t_i = lax.broadcasted_iota(jnp.int32, (1, c, c), 1)
    s_i = lax.broadcasted_iota(jnp.int32, (1, c, c), 2)
    a = jnp.where(s_i <= t_i, a, 0.0)
    o_intra = jnp.einsum("nts,nsv->ntv", _bf(a), v3, preferred_element_type=jnp.float32)

    st = st_ref[...]
    outs = []
    for ci in range(n):
        outs.append(_dot_nt(q_state[ci], _bf(st)) + o_intra[ci])
        st = st * jnp.exp(blast[ci]) + _dot_tn(v3[ci], k_state[ci])
    st_ref[...] = st
    return jnp.concatenate(outs, axis=0)


def _la_finish(o, og, gain, y_ref):
    o = o * lax.rsqrt(jnp.mean(o * o, axis=-1, keepdims=True) + EPS)
    y_ref[0] = _bf(o * gain * (og * _sigmoid(og)))


def _hgrn2_kernel(q_ref, f_ref, i_ref, og_ref, par_ref, y_ref, st_ref, *, tb):
    @pl.when(pl.program_id(2) == 0)
    def _():
        st_ref[...] = jnp.zeros_like(st_ref)

    log_lb = par_ref[0, 0:1, :]
    log1m_lb = par_ref[0, 1:2, :]
    one_m_lb = par_ref[0, 2:3, :]
    gain = par_ref[0, 3:4, :]
    hq = q_ref[0]
    hf = f_ref[0]
    q = hq * _sigmoid(hq)
    cterm = log1m_lb + _log_sigmoid(hf)
    log_f = jnp.maximum(log_lb, cterm) + jnp.log1p(jnp.exp(-jnp.abs(log_lb - cterm)))
    k = one_m_lb * _sigmoid(-hf)
    o = _la_core(q, k, i_ref[0], log_f, st_ref, tb)
    _la_finish(o, og_ref[0], gain, y_ref)


def _gla_kernel(q_ref, k_ref, v_ref, og_ref, lr_ref, w2_ref, par_ref, y_ref, st_ref, *, tb):
    @pl.when(pl.program_id(2) == 0)
    def _():
        st_ref[...] = jnp.zeros_like(st_ref)

    bias = par_ref[0, 0:1, :]
    gain = par_ref[0, 1:2, :]
    zz = _dot(_bf(lr_ref[0]), w2_ref[0]) + bias
    log_a = _log_sigmoid(zz) * (1.0 / GLA_TAU)
    q = q_ref[0] * (GLA_DK ** -0.5)
    o = _la_core(q, k_ref[0], v_ref[0], log_a, st_ref, tb)
    _la_finish(o, og_ref[0], gain, y_ref)


def _zspec(tb, col0):
    base = col0 // LANES
    return pl.BlockSpec((1, tb, LANES), lambda b, h, t: (b, t, base + h))


def _la_params():
    return pltpu.CompilerParams(
        dimension_semantics=("parallel", "parallel", "arbitrary"), vmem_limit_bytes=VMEM_LIMIT)


def _hgrn2(z3, par, tb=256):
    bsz, t, _ = z3.shape
    return pl.pallas_call(
        functools.partial(_hgrn2_kernel, tb=tb),
        grid=(bsz, HG_HEADS, t // tb),
        in_specs=[_zspec(tb, C_HQ), _zspec(tb, C_HF), _zspec(tb, C_HI), _zspec(tb, C_HOG),
                  pl.BlockSpec((1, 8, LANES), lambda b, h, t: (h, 0, 0))],
        out_specs=pl.BlockSpec((1, tb, LANES), lambda b, h, t: (b, t, h)),
        out_shape=jax.ShapeDtypeStruct((bsz, t, HG_DIM), jnp.bfloat16),
        scratch_shapes=[pltpu.VMEM((LANES, LANES), jnp.float32)],
        compiler_params=_la_params(),
        name="hgrn2",
    )(z3, z3, z3, z3, par)


def _gla(z3, w2p, par, tb=256):
    bsz, t, _ = z3.shape
    lr_blk = C_GLR // LANES
    return pl.pallas_call(
        functools.partial(_gla_kernel, tb=tb),
        grid=(bsz, GLA_HEADS, t // tb),
        in_specs=[_zspec(tb, C_GQ), _zspec(tb, C_GK), _zspec(tb, C_GV), _zspec(tb, C_GOG),
                  pl.BlockSpec((1, tb, LANES), lambda b, h, t: (b, t, lr_blk)),
                  pl.BlockSpec((1, LANES, LANES), lambda b, h, t: (h, 0, 0)),
                  pl.BlockSpec((1, 8, LANES), lambda b, h, t: (h, 0, 0))],
        out_specs=pl.BlockSpec((1, tb, LANES), lambda b, h, t: (b, t, h)),
        out_shape=jax.ShapeDtypeStruct((bsz, t, GLA_VDIM), jnp.bfloat16),
        scratch_shapes=[pltpu.VMEM((LANES, LANES), jnp.float32)],
        compiler_params=_la_params(),
        name="gla",
    )(z3, z3, z3, z3, z3, w2p, par)


def _rope(x, cos_t, sin_a, sin_b):
    half = ROT_DIM // 2
    slabs = []
    for i in range(x.shape[-1] // LANES):
        xs = x[:, i * LANES:(i + 1) * LANES]
        slabs.append(xs * cos_t + pltpu.roll(xs, LANES - half, axis=1) * sin_a
                     + pltpu.roll(xs, half, axis=1) * sin_b)
    return slabs[0] if len(slabs) == 1 else jnp.concatenate(slabs, axis=1)


def _nsa_prep_kernel(q_ref, kc_ref, ks_ref, vs_ref, kw_ref, vw_ref, cos_ref, sa_ref, sb_ref,
                     qo_ref, kco_ref, kso_ref, vso_ref, kwo_ref, vwo_ref):
    ct, sa, sb = cos_ref[...], sa_ref[...], sb_ref[...]
    qo_ref[0] = _bf(_rope(q_ref[0], ct, sa, sb) * (NSA_DH ** -0.5))
    kco_ref[0] = _rope(kc_ref[0], ct, sa, sb)
    kso_ref[0] = _bf(_rope(ks_ref[0], ct, sa, sb))
    kwo_ref[0] = _bf(_rope(kw_ref[0], ct, sa, sb))
    vso_ref[0] = _bf(vs_ref[0])
    vwo_ref[0] = _bf(vw_ref[0])


def _nsa_prep(z3, cos_t, sin_a, sin_b, tt=512):
    bsz, t, _ = z3.shape
    qw = NSA_HEADS * LANES

    def zs(col0, width):
        blk = col0 // width
        return pl.BlockSpec((1, tt, width), lambda b, i: (b, i, blk))

    def os_(width):
        return pl.BlockSpec((1, tt, width), lambda b, i: (b, i, 0))

    tab = pl.BlockSpec((tt, LANES), lambda b, i: (i, 0))
    kv_bf = jax.ShapeDtypeStruct((bsz, t, LANES), jnp.bfloat16)
    return pl.pallas_call(
        _nsa_prep_kernel,
        grid=(bsz, t // tt),
        in_specs=[zs(C_NQ, qw), zs(C_NKC, LANES), zs(C_NKS, LANES), zs(C_NVS, LANES),
                  zs(C_NKW, LANES), zs(C_NVW, LANES), tab, tab, tab],
        out_specs=[os_(qw), os_(LANES), os_(LANES), os_(LANES), os_(LANES), os_(LANES)],
        out_shape=[jax.ShapeDtypeStruct((bsz, t, qw), jnp.bfloat16),
                   jax.ShapeDtypeStruct((bsz, t, LANES), jnp.float32),
                   kv_bf, kv_bf, kv_bf, kv_bf],
        compiler_params=pltpu.CompilerParams(
            dimension_semantics=("parallel", "parallel"), vmem_limit_bytes=VMEM_LIMIT),
        name="nsa_prep",
    )(z3, z3, z3, z3, z3, z3, cos_t, sin_a, sin_b)


def _compress_one(src_ref, pos_ref, w1_ref, w2_ref, out_ref, ncmp):
    half = CMP_BLK // 2
    u = jnp.zeros((ncmp, 2 * CMP_HID), jnp.float32)
    v = jnp.zeros((ncmp, 2 * CMP_HID), jnp.float32)
    for j in range(half):
        rows = src_ref[0, pl.ds(j, ncmp, stride=CMP_STRIDE), :]
        u = u + _dot(_bf(rows + pos_ref[j:j + 1, :]), w1_ref[j])
        v = v + _dot(_bf(rows + pos_ref[half + j:half + j + 1, :]), w1_ref[half + j])
    hid = u + pltpu.roll(v, ncmp - 1, axis=0)
    out_ref[0] = _bf(_dot(_bf(jax.nn.gelu(hid)), w2_ref[...]))


def _compress_kernel(k_ref, v_ref, pk_ref, pv_ref, w1k_ref, w2k_ref, w1v_ref, w2v_ref,
                     kc_ref, vc_ref, *, ncmp):
    _compress_one(k_ref, pk_ref, w1k_ref, w2k_ref, kc_ref, ncmp)
    _compress_one(v_ref, pv_ref, w1v_ref, w2v_ref, vc_ref, ncmp)


def _compress(kc_rot, z3, pos_k, pos_v, w1k, w2k, w1v, w2v):
    bsz, t, _ = kc_rot.shape
    ncmp = t // CMP_STRIDE
    vblk = C_NVC // LANES
    full = lambda shape: pl.BlockSpec(shape, lambda b: (0,) * len(shape))
    out = jax.ShapeDtypeStruct((bsz, ncmp, LANES), jnp.bfloat16)
    return pl.pallas_call(
        functools.partial(_compress_kernel, ncmp=ncmp),
        grid=(bsz,),
        in_specs=[pl.BlockSpec((1, t, LANES), lambda b: (b, 0, 0)),
                  pl.BlockSpec((1, t, LANES), lambda b: (b, 0, vblk)),
                  full(pos_k.shape), full(pos_v.shape), full(w1k.shape), full(w2k.shape),
                  full(w1v.shape), full(w2v.shape)],
        out_specs=[pl.BlockSpec((1, ncmp, LANES), lambda b: (b, 0, 0))] * 2,
        out_shape=[out, out],
        compiler_params=pltpu.CompilerParams(
            dimension_semantics=("parallel",), vmem_limit_bytes=VMEM_LIMIT),
        name="nsa_compress",
    )(kc_rot, z3, pos_k, pos_v, w1k, w2k, w1v, w2v)


NEG = -1e30
SEL_KB = 256


def _softmax_rows(s, mask):
    s = jnp.where(mask, s, NEG)
    m = jnp.max(s, axis=-1, keepdims=True)
    p = jnp.where(mask, jnp.exp(s - m), 0.0)
    l = jnp.sum(p, axis=-1, keepdims=True)
    return p / jnp.where(l > 0.0, l, 1.0)


def _select_blocks(imp, t_col, nslc):
    qb = imp.shape[0]
    blk = lax.broadcasted_iota(jnp.int32, (qb, LANES), 1)
    cur = t_col // SLC_BLK
    valid = blk <= cur
    forced = (blk == 0) | (blk == cur) | (blk == cur - 1)
    score = jnp.where(valid & forced, 1e9, jnp.where(valid, imp, -1e9))
    sc_t = score.T
    j_idx = lax.broadcasted_iota(jnp.int32, (LANES, qb), 0)
    rank = jnp.zeros((LANES, qb), jnp.float32)
    for i in range(nslc):
        row = sc_t[i:i + 1, :]
        ahead = (row > sc_t) | ((row == sc_t) & (i < j_idx))
        rank = rank + jnp.where(ahead, 1.0, 0.0)
    sel_t = jnp.where(rank < float(N_SEL), 1.0, 0.0)
    return jnp.where(valid, sel_t.T, 0.0)


def _nsa_attn_kernel(q_ref, kc_ref, vc_ref, ks_ref, vs_ref, kw_ref, vw_ref, gate_ref,
                     ovl_ref, exp_ref, y_ref, *, seq):
    c = pl.program_id(1)
    qb = Q_BLOCK
    rows = NSA_REP * qb
    t0 = c * qb
    t_col = t0 + lax.broadcasted_iota(jnp.int32, (qb, 1), 0)
    t_rows = jnp.concatenate([t_col] * NSA_REP, axis=0)
    gates = _sigmoid(gate_ref[0])
    ncmp = kc_ref.shape[1]
    n_idx = lax.broadcasted_iota(jnp.int32, (1, ncmp), 1)
    mask_c = (n_idx * CMP_STRIDE + (CMP_BLK - 1)) <= t_rows
    w_start = pl.multiple_of(jnp.maximum(t0 - WIN, 0), qb)
    wk = WIN + qb
    kp = w_start + lax.broadcasted_iota(jnp.int32, (1, wk), 1)
    mask_w = (kp <= t_rows) & (kp > t_rows - WIN)
    n_chunks = (t0 + qb + SEL_KB - 1) // SEL_KB

    for g in range(NSA_GROUPS):
        qg = jnp.concatenate(
            [q_ref[0, :, (g * NSA_REP + r) * LANES:(g * NSA_REP + r + 1) * LANES]
             for r in range(NSA_REP)], axis=0)
        p_c = _softmax_rows(_dot_nt(qg, kc_ref[0]), mask_c)
        o_c = _dot(_bf(p_c), vc_ref[0])
        p_sum = p_c[0:qb] + p_c[qb:2 * qb] + p_c[2 * qb:3 * qb] + p_c[3 * qb:4 * qb]
        imp = jnp.dot(p_sum, ovl_ref[...], preferred_element_type=jnp.float32,
                      precision=lax.Precision.HIGHEST)
        sel = _bf(_select_blocks(imp, t_col, seq // SLC_BLK))

        def body(i, carry):
            m, l, acc = carry
            k0 = pl.multiple_of(i * SEL_KB, SEL_KB)
            s = _dot_nt(qg, ks_ref[0, pl.ds(k0, SEL_KB), :])
            picked = _dot(sel, exp_ref[:, pl.ds(k0, SEL_KB)]) > 0.5
            tok = k0 + lax.broadcasted_iota(jnp.int32, (1, SEL_KB), 1)
            mk = picked & (tok <= t_col)
            mk = jnp.concatenate([mk] * NSA_REP, axis=0)
            s = jnp.where(mk, s, NEG)
            m_new = jnp.maximum(m, jnp.max(s, axis=-1, keepdims=True))
            alpha = jnp.exp(m - m_new)
            p = jnp.where(mk, jnp.exp(s - m_new), 0.0)
            l = alpha * l + jnp.sum(p, axis=-1, keepdims=True)
            acc = alpha * acc + _dot(_bf(p), vs_ref[0, pl.ds(k0, SEL_KB), :])
            return m_new, l, acc

        init = (jnp.full((rows, 1), NEG, jnp.float32), jnp.zeros((rows, 1), jnp.float32),
                jnp.zeros((rows, LANES), jnp.float32))
        _, l_s, acc_s = lax.fori_loop(0, n_chunks, body, init)
        o_s = acc_s / l_s

        p_w = _softmax_rows(_dot_nt(qg, kw_ref[0, pl.ds(w_start, wk), :]), mask_w)
        o_w = _dot(_bf(p_w), vw_ref[0, pl.ds(w_start, wk), :])

        for r in range(NSA_REP):
            hd = g * NSA_REP + r
            sl = slice(r * qb, (r + 1) * qb)
            out = (o_c[sl] * gates[:, 3 * hd:3 * hd + 1] + o_s[sl] * gates[:, 3 * hd + 1:3 * hd + 2]
                   + o_w[sl] * gates[:, 3 * hd + 2:3 * hd + 3])
            y_ref[0, :, hd * LANES:(hd + 1) * LANES] = _bf(out)


def _nsa_attn(q_rot, kc, vc, ks, vs, kw, vw, z3, ovl, expand):
    bsz, t, qw = q_rot.shape
    ncmp = kc.shape[1]
    gblk = C_NGATE // LANES
    per_b = lambda rows: pl.BlockSpec((1, rows, LANES), lambda b, c: (b, 0, 0))
    return pl.pallas_call(
        functools.partial(_nsa_attn_kernel, seq=t),
        grid=(bsz, t // Q_BLOCK),
        in_specs=[pl.BlockSpec((1, Q_BLOCK, qw), lambda b, c: (b, c, 0)),
                  per_b(ncmp), per_b(ncmp), per_b(t), per_b(t), per_b(t), per_b(t),
                  pl.BlockSpec((1, Q_BLOCK, LANES), lambda b, c: (b, c, gblk)),
                  pl.BlockSpec(ovl.shape, lambda b, c: (0, 0)),
                  pl.BlockSpec(expand.shape, lambda b, c: (0, 0))],
        out_specs=pl.BlockSpec((1, Q_BLOCK, qw), lambda b, c: (b, c, 0)),
        out_shape=jax.ShapeDtypeStruct((bsz, t, qw), jnp.bfloat16),
        compiler_params=pltpu.CompilerParams(
            dimension_semantics=("parallel", "arbitrary"), vmem_limit_bytes=VMEM_LIMIT),
        name="nsa_attn",
    )(q_rot, kc, vc, ks, vs, kw, vw, z3, ovl, expand)


def _merge_kernel(x_ref, yh_ref, yg_ref, yn_ref, mg_ref, wh_ref, wg_ref, wn_ref, wo_ref, o_ref):
    d = D_MODEL
    m = _sigmoid(mg_ref[:, 0:d]) * _dot(yh_ref[...], wh_ref[...])
    m = m + _sigmoid(mg_ref[:, d:2 * d]) * _dot(yg_ref[...], wg_ref[...])
    m = m + _sigmoid(mg_ref[:, 2 * d:3 * d]) * _dot(yn_ref[...], wn_ref[...])
    o_ref[...] = x_ref[...] + _dot(_bf(m), wo_ref[...])


def _merge(x2, yh, yg, yn, z2, wh, wg, wn, wo, tm=256):
    n, d = x2.shape
    mgblk = C_MG // (3 * d)
    row = lambda w: pl.BlockSpec((tm, w), lambda i: (i, 0))
    full = lambda a: pl.BlockSpec(a.shape, lambda i: (0, 0))
    return pl.pallas_call(
        _merge_kernel,
        grid=(n // tm,),
        in_specs=[row(d), row(yh.shape[1]), row(yg.shape[1]), row(yn.shape[1]),
                  pl.BlockSpec((tm, 3 * d), lambda i: (i, mgblk)),
                  full(wh), full(wg), full(wn), full(wo)],
        out_specs=row(d),
        out_shape=jax.ShapeDtypeStruct((n, d), jnp.float32),
        compiler_params=pltpu.CompilerParams(
            dimension_semantics=("parallel",), vmem_limit_bytes=VMEM_LIMIT),
        name="merge",
    )(x2, yh, yg, yn, z2, wh, wg, wn, wo)


def _ffn_kernel(x_ref, g_ref, wg_ref, wu_ref, wd_ref, fg_ref, o_ref, h_scr, acc_scr, *, final_norm):
    j = pl.program_id(1)

    @pl.when(j == 0)
    def _():
        x = x_ref[...]
        y = x * lax.rsqrt(jnp.mean(x * x, axis=-1, keepdims=True) + EPS)
        h_scr[...] = _bf(y * g_ref[...])
        acc_scr[...] = jnp.zeros_like(acc_scr)

    h = h_scr[...]
    a = _dot(h, wg_ref[...])
    u = _dot(h, wu_ref[...])
    acc_scr[...] += _dot(_bf(a * _sigmoid(a) * u), wd_ref[...])

    @pl.when(j == pl.num_programs(1) - 1)
    def _():
        out = x_ref[...] + acc_scr[...]
        if final_norm:
            out = out * lax.rsqrt(jnp.mean(out * out, axis=-1, keepdims=True) + EPS) * fg_ref[...]
        o_ref[...] = out


def _ffn(x2, g, wg, wu, wd, fg, final_norm, tm=512, tf=256):
    n, d = x2.shape
    ff = wg.shape[1]
    return pl.pallas_call(
        functools.partial(_ffn_kernel, final_norm=final_norm),
        grid=(n // tm, ff // tf),
        in_specs=[pl.BlockSpec((tm, d), lambda i, j: (i, 0)),
                  pl.BlockSpec((1, d), lambda i, j: (0, 0)),
                  pl.BlockSpec((d, tf), lambda i, j: (0, j)),
                  pl.BlockSpec((d, tf), lambda i, j: (0, j)),
                  pl.BlockSpec((tf, d), lambda i, j: (j, 0)),
                  pl.BlockSpec((1, d), lambda i, j: (0, 0))],
        out_specs=pl.BlockSpec((tm, d), lambda i, j: (i, 0)),
        out_shape=jax.ShapeDtypeStruct((n, d), jnp.float32),
        scratch_shapes=[pltpu.VMEM((tm, d), jnp.bfloat16), pltpu.VMEM((tm, d), jnp.float32)],
        compiler_params=pltpu.CompilerParams(
            dimension_semantics=("parallel", "arbitrary"), vmem_limit_bytes=VMEM_LIMIT),
        name="ffn",
    )(x2, g, wg, wu, wd, fg)


def _pad_heads(w, heads, width):
    lead = w.shape[:-1]
    w = w.reshape(lead + (heads, width))
    w = jnp.pad(w, [(0, 0)] * len(lead) + [(0, 0), (0, LANES - width)])
    return w.reshape(lead + (heads * LANES,))


def _layout_w_in(w):
    d = w.shape[0]
    o = 0
    parts = {}
    for name, n in (("hq", 512), ("hf", 512), ("hi", 512), ("hog", 512), ("gq", 256), ("gk", 256),
                    ("gv", 512), ("glr", 16), ("gog", 512), ("nq", 512), ("nkc", 128), ("nvc", 128),
                    ("nks", 128), ("nvs", 128), ("nkw", 128), ("nvw", 128), ("ngate", 24), ("mg", 3072)):
        parts[name] = w[:, o:o + n]
        o += n
    nq = parts["nq"].reshape(d, NSA_HEADS, NSA_DH)
    zero = jnp.zeros_like(nq)
    grp = (jnp.arange(NSA_HEADS) // NSA_REP)[None, :, None]
    nq = jnp.concatenate([jnp.where(grp == 0, nq, zero), jnp.where(grp == 1, nq, zero)], axis=-1)
    nq = nq.reshape(d, NSA_HEADS * LANES)
    padc = lambda a: jnp.pad(a, ((0, 0), (0, LANES - a.shape[1])))
    cols = [parts["hq"], parts["hf"], parts["hi"], parts["hog"],
            _pad_heads(parts["gq"], GLA_HEADS, GLA_DK), _pad_heads(parts["gk"], GLA_HEADS, GLA_DK),
            parts["gv"], parts["gog"], nq,
            parts["nkc"], parts["nvc"], parts["nks"], parts["nvs"], parts["nkw"], parts["nvw"],
            padc(parts["ngate"]), padc(parts["glr"]), parts["mg"]]
    out = jnp.concatenate(cols, axis=1)
    assert out.shape[1] == Z_COLS
    return _bf(out)


def _blockdiag2(w):
    z = jnp.zeros_like(w)
    top = jnp.concatenate([w, z], axis=-1)
    bot = jnp.concatenate([z, w], axis=-1)
    return jnp.concatenate([top, bot], axis=-2)


def _rope_tables(t):
    pos = jnp.arange(t, dtype=jnp.float32)
    inv_freq = ROPE_THETA ** (-jnp.arange(0, ROT_DIM, 2, dtype=jnp.float32) / ROT_DIM)
    ang = pos[:, None] * inv_freq[None, :]
    cos, sin = jnp.cos(ang), jnp.sin(ang)
    half = ROT_DIM // 2
    lane = jnp.arange(LANES) % NSA_DH
    first = lane < half
    second = (lane >= half) & (lane < ROT_DIM)
    idx = jnp.where(first, lane, jnp.where(second, lane - half, 0))
    cos_t = jnp.where(first | second, cos[:, idx], 1.0)
    sin_a = jnp.where(first, -sin[:, idx], 0.0)
    sin_b = jnp.where(second, sin[:, idx], 0.0)
    return cos_t, sin_a, sin_b


def kernel(x, norm1_g, w_in, hg_lb, hg_norm_g, gla_w2, gla_b, gla_norm_g, cmp_pos_k, cmp_pos_v,
           cmp_w1_k, cmp_w2_k, cmp_w1_v, cmp_w2_v, w_br_hg, w_br_gla, w_br_nsa, w_out,
           norm2_g, w_ffn_gate, w_ffn_up, w_ffn_down, final_norm_g):
    bsz, t, d = x.shape
    depth = w_in.shape[0]
    n = bsz * t
    f32 = jnp.float32

    cos_t, sin_a, sin_b = _rope_tables(t)
    lbs = jnp.cumsum(jax.nn.softmax(hg_lb.astype(f32), axis=0), axis=0)
    lbs = lbs - lbs[0]

    nslc = t // SLC_BLK
    ncmp_pad = t // CMP_STRIDE
    cmp_start = jnp.arange(ncmp_pad) * CMP_STRIDE
    blk = jnp.arange(LANES)
    ovl = ((cmp_start[:, None] < (blk[None, :] + 1) * SLC_BLK)
           & (cmp_start[:, None] + CMP_BLK - 1 >= blk[None, :] * SLC_BLK)
           & (blk[None, :] < nslc) & (jnp.arange(ncmp_pad)[:, None] < ncmp_pad - 1)).astype(f32)
    expand = _bf((jnp.arange(t)[None, :] // SLC_BLK) == blk[:, None])

    x2 = x.reshape(n, d)
    for l in range(depth):
        z2 = _inproj(x2, norm1_g[l][None, :], _layout_w_in(w_in[l]))
        z3 = z2.reshape(bsz, t, Z_COLS)

        lb = lbs[l].reshape(HG_HEADS, LANES)
        zeros = jnp.zeros_like(lb)
        hg_par = jnp.stack([jnp.log(lb), jnp.log1p(-lb), 1.0 - lb,
                            hg_norm_g[l].reshape(HG_HEADS, LANES), zeros, zeros, zeros, zeros], axis=1)
        y_hg = _hgrn2(z3, hg_par)

        w2p = jnp.pad(_pad_heads(gla_w2[l], GLA_HEADS, GLA_DK), ((0, LANES - GLA_RANK), (0, 0)))
        w2p = _bf(w2p.reshape(LANES, GLA_HEADS, LANES).transpose(1, 0, 2))
        gb = _pad_heads(gla_b[l], GLA_HEADS, GLA_DK).reshape(GLA_HEADS, LANES)
        gla_par = jnp.stack([gb, gla_norm_g[l].reshape(GLA_HEADS, LANES),
                             zeros, zeros, zeros, zeros, zeros, zeros], axis=1)
        y_gla = _gla(z3, w2p, gla_par)

        q_rot, kc_rot, ks, vs, kw, vw = _nsa_prep(z3, cos_t, sin_a, sin_b)
        tile2 = lambda p: jnp.concatenate([p, p], axis=1)
        w1k = _bf(_blockdiag2(cmp_w1_k[l].reshape(CMP_BLK, NSA_DH, CMP_HID)))
        w1v = _bf(_blockdiag2(cmp_w1_v[l].reshape(CMP_BLK, NSA_DH, CMP_HID)))
        kc, vc = _compress(kc_rot, z3, tile2(cmp_pos_k[l]), tile2(cmp_pos_v[l]),
                           w1k, _bf(_blockdiag2(cmp_w2_k[l])), w1v, _bf(_blockdiag2(cmp_w2_v[l])))
        y_nsa = _nsa_attn(q_rot, kc, vc, ks, vs, kw, vw, z3, ovl, expand)

        wn = w_br_nsa[l].reshape(NSA_HEADS, NSA_DH, d)
        zn = jnp.zeros_like(wn)
        grp = (jnp.arange(NSA_HEADS) // NSA_REP)[:, None, None]
        wn = jnp.concatenate([jnp.where(grp == 0, wn, zn), jnp.where(grp == 1, wn, zn)], axis=1)
        wn = wn.reshape(NSA_HEADS * LANES, d)
        x2 = _merge(x2, y_hg.reshape(n, HG_DIM), y_gla.reshape(n, GLA_VDIM),
                    y_nsa.reshape(n, NSA_HEADS * LANES), z2,
                    _bf(w_br_hg[l]), _bf(w_br_gla[l]), _bf(wn), _bf(w_out[l]))

        x2 = _ffn(x2, norm2_g[l][None, :], _bf(w_ffn_gate[l]), _bf(w_ffn_up[l]), _bf(w_ffn_down[l]),
                  final_norm_g[None, :], final_norm=(l == depth - 1))
    return x2.reshape(bsz, t, d)
```

```python
import functools

import jax
import jax.numpy as jnp
from jax import lax
from jax.experimental import pallas as pl
from jax.experimental.pallas import tpu as pltpu

EPS = 1e-6
D_MODEL = 1024
LANES = 128

HG_HEADS = 4
HG_DIM = 512
GLA_HEADS = 4
GLA_DK = 64
GLA_KDIM = 256
GLA_VDIM = 512
GLA_RANK = 16
GLA_TAU = 16.0
LA_CHUNK = 64
LA_SUB = 16

NSA_HEADS = 8
NSA_GROUPS = 2
NSA_REP = 4
NSA_DH = 64
CMP_STRIDE = 16
CMP_BLK = 32
CMP_HID = 128
SLC_BLK = 64
N_SEL = 16
WIN = 512
Q_BLOCK = 128
ROPE_THETA = 500000.0
ROT_DIM = 16
FF_DIM = 2816

C_HQ, C_HF, C_HI, C_HOG = 0, 512, 1024, 1536
C_GQ, C_GK, C_GV, C_GOG = 2048, 2560, 3072, 3584
C_NQ = 4096
C_NKC, C_NVC, C_NKS, C_NVS, C_NKW, C_NVW = 5120, 5248, 5376, 5504, 5632, 5760
C_NGATE, C_GLR = 5888, 6016
C_MG = 6144
Z_COLS = 9216

VMEM_LIMIT = 56 * 1024 * 1024
LOG2E = 1.4426950408889634

_NT = (((1,), (1,)), ((), ()))
_TN = (((0,), (0,)), ((), ()))


def _bf(x):
    return x.astype(jnp.bfloat16)


def _dot(a, b):
    return jnp.dot(a, b, preferred_element_type=jnp.float32)


def _dot_nt(a, b):
    return lax.dot_general(a, b, _NT, preferred_element_type=jnp.float32)


def _dot_tn(a, b):
    return lax.dot_general(a, b, _TN, preferred_element_type=jnp.float32)


def _sigmoid(x):
    return 1.0 / (1.0 + jnp.exp(-x))


def _log_sigmoid(x):
    return jnp.minimum(x, 0.0) - jnp.log1p(jnp.exp(-jnp.abs(x)))


def _inproj_kernel(x_ref, g_ref, w_ref, z_ref, h_scr):
    @pl.when(pl.program_id(1) == 0)
    def _():
        x = x_ref[...]
        y = x * lax.rsqrt(jnp.mean(x * x, axis=-1, keepdims=True) + EPS)
        h_scr[...] = _bf(y * g_ref[...])

    z_ref[...] = _dot(h_scr[...], w_ref[...])


def _inproj(x2, g, w, tm=512, tn=1024):
    n, d = x2.shape
    cols = w.shape[1]
    return pl.pallas_call(
        _inproj_kernel,
        grid=(n // tm, cols // tn),
        in_specs=[
            pl.BlockSpec((tm, d), lambda i, j: (i, 0)),
            pl.BlockSpec((1, d), lambda i, j: (0, 0)),
            pl.BlockSpec((d, tn), lambda i, j: (0, j)),
        ],
        out_specs=pl.BlockSpec((tm, tn), lambda i, j: (i, j)),
        out_shape=jax.ShapeDtypeStruct((n, cols), jnp.float32),
        scratch_shapes=[pltpu.VMEM((tm, d), jnp.bfloat16)],
        compiler_params=pltpu.CompilerParams(
            dimension_semantics=("parallel", "arbitrary"), vmem_limit_bytes=VMEM_LIMIT),
        name="inproj",
    )(x2, g, w)


def _chunk_cumsum(g, tb):
    pos = lax.broadcasted_iota(jnp.int32, (tb, 1), 0) % LA_CHUNK
    b = g
    shift = 1
    while shift < LA_CHUNK:
        b = b + jnp.where(pos >= shift, pltpu.roll(b, shift, axis=0), 0.0)
        shift *= 2
    return b


def _la_core(q, k, v, g, st_ref, tb):
    c, r = LA_CHUNK, LA_SUB
    n = tb // c
    nsub = c // r
    dk = q.shape[-1]
    b = _chunk_cumsum(g, tb)
    b3 = b.reshape(n, c, dk)
    q3 = q.reshape(n, c, dk)
    k3 = k.reshape(n, c, dk)
    v3 = _bf(v).reshape(n, c, dk)
    blast = b3[:, c - 1:c, :]
    q_state = _bf(q3 * jnp.exp(b3))
    k_state = _bf(k3 * jnp.exp(blast - b3))

    b4 = b.reshape(n * nsub, r, dk)
    ref = b4[:, 0:1, :]
    q_loc = _bf(q.reshape(n * nsub, r, dk) * jnp.exp(b4 - ref))
    s_idx = lax.broadcasted_iota(jnp.int32, (1, nsub, c, 1), 2)
    i_idx = lax.broadcasted_iota(jnp.int32, (1, nsub, c, 1), 1)
    e = jnp.where(s_idx < r * (i_idx + 1), ref.reshape(n, nsub, 1, dk) - b3[:, None, :, :], 0.0)
    k_loc = _bf(k3[:, None, :, :] * jnp.exp(e)).reshape(n * nsub, c, dk)
    a = jnp.einsum("utd,usd->uts", q_loc, k_loc, preferred_element_type=jnp.float32)
    a = a.reshape(n, c, c)
    t_i = lax.broadcasted_iota(jnp.int32, (1, c, c), 1)
    s_i = lax.broadcasted_iota(jnp.int32, (1, c, c), 2)
    a = jnp.where(s_i <= t_i, a, 0.0)
    o_intra = jnp.einsum("nts,nsv->ntv", _bf(a), v3, preferred_element_type=jnp.float32)

    st = st_ref[...]
    outs = []
    for ci in range(n):
        outs.append(_dot_nt(q_state[ci], _bf(st)) + o_intra[ci])
        st = st * jnp.exp(blast[ci]) + _dot_tn(v3[ci], k_state[ci])
    st_ref[...] = st
    return jnp.concatenate(outs, axis=0)


def _la_finish(o, og, gain, y_ref):
    o = o * lax.rsqrt(jnp.mean(o * o, axis=-1, keepdims=True) + EPS)
    y_ref[0] = _bf(o * gain * (og * _sigmoid(og)))


def _hgrn2_kernel(q_ref, f_ref, i_ref, og_ref, par_ref, y_ref, st_ref, *, tb):
    @pl.when(pl.program_id(2) == 0)
    def _():
        st_ref[...] = jnp.zeros_like(st_ref)

    log_lb = par_ref[0, 0:1, :]
    log1m_lb = par_ref[0, 1:2, :]
    one_m_lb = par_ref[0, 2:3, :]
    gain = par_ref[0, 3:4, :]
    hq = q_ref[0]
    hf = f_ref[0]
    q = hq * _sigmoid(hq)
    cterm = log1m_lb + _log_sigmoid(hf)
    log_f = jnp.maximum(log_lb, cterm) + jnp.log1p(jnp.exp(-jnp.abs(log_lb - cterm)))
    k = one_m_lb * _sigmoid(-hf)
    o = _la_core(q, k, i_ref[0], log_f, st_ref, tb)
    _la_finish(o, og_ref[0], gain, y_ref)


def _gla_kernel(q_ref, k_ref, v_ref, og_ref, lr_ref, w2_ref, par_ref, y_ref, st_ref, *, tb):
    @pl.when(pl.program_id(2) == 0)
    def _():
        st_ref[...] = jnp.zeros_like(st_ref)

    bias = par_ref[0, 0:1, :]
    gain = par_ref[0, 1:2, :]
    zz = _dot(_bf(lr_ref[0]), w2_ref[0]) + bias
    log_a = _log_sigmoid(zz) * (1.0 / GLA_TAU)
    q = q_ref[0] * (GLA_DK ** -0.5)
    o = _la_core(q, k_ref[0], v_ref[0], log_a, st_ref, tb)
    _la_finish(o, og_ref[0], gain, y_ref)


def _zspec(tb, col0):
    base = col0 // LANES
    return pl.BlockSpec((1, tb, LANES), lambda b, h, t: (b, t, base + h))


def _la_params():
    return pltpu.CompilerParams(
        dimension_semantics=("parallel", "parallel", "arbitrary"), vmem_limit_bytes=VMEM_LIMIT)


def _hgrn2(z3, par, tb=256):
    bsz, t, _ = z3.shape
    return pl.pallas_call(
        functools.partial(_hgrn2_kernel, tb=tb),
        grid=(bsz, HG_HEADS, t // tb),
        in_specs=[_zspec(tb, C_HQ), _zspec(tb, C_HF), _zspec(tb, C_HI), _zspec(tb, C_HOG),
                  pl.BlockSpec((1, 8, LANES), lambda b, h, t: (h, 0, 0))],
        out_specs=pl.BlockSpec((1, tb, LANES), lambda b, h, t: (b, t, h)),
        out_shape=jax.ShapeDtypeStruct((bsz, t, HG_DIM), jnp.bfloat16),
        scratch_shapes=[pltpu.VMEM((LANES, LANES), jnp.float32)],
        compiler_params=_la_params(),
        name="hgrn2",
    )(z3, z3, z3, z3, par)


def _gla(z3, w2p, par, tb=256):
    bsz, t, _ = z3.shape
    lr_blk = C_GLR // LANES
    return pl.pallas_call(
        functools.partial(_gla_kernel, tb=tb),
        grid=(bsz, GLA_HEADS, t // tb),
        in_specs=[_zspec(tb, C_GQ), _zspec(tb, C_GK), _zspec(tb, C_GV), _zspec(tb, C_GOG),
                  pl.BlockSpec((1, tb, LANES), lambda b, h, t: (b, t, lr_blk)),
                  pl.BlockSpec((1, LANES, LANES), lambda b, h, t: (h, 0, 0)),
                  pl.BlockSpec((1, 8, LANES), lambda b, h, t: (h, 0, 0))],
        out_specs=pl.BlockSpec((1, tb, LANES), lambda b, h, t: (b, t, h)),
        out_shape=jax.ShapeDtypeStruct((bsz, t, GLA_VDIM), jnp.bfloat16),
        scratch_shapes=[pltpu.VMEM((LANES, LANES), jnp.float32)],
        compiler_params=_la_params(),
        name="gla",
    )(z3, z3, z3, z3, z3, w2p, par)


def _rope(x, cos_t, sin_a, sin_b):
    half = ROT_DIM // 2
    slabs = []
    for i in range(x.shape[-1] // LANES):
        xs = x[:, i * LANES:(i + 1) * LANES]
        slabs.append(xs * cos_t + pltpu.roll(xs, LANES - half, axis=1) * sin_a
                     + pltpu.roll(xs, half, axis=1) * sin_b)
    return slabs[0] if len(slabs) == 1 else jnp.concatenate(slabs, axis=1)


def _per_group(x, filler):
    swapped = pltpu.roll(x, NSA_DH, axis=1)
    if filler is None:
        return _bf(x), _bf(swapped)
    low = lax.broadcasted_iota(jnp.int32, x.shape, 1) < NSA_DH
    return _bf(jnp.where(low, x, filler)), _bf(jnp.where(low, swapped, filler))


def _nsa_prep_kernel(q_ref, kc_ref, ks_ref, vs_ref, kw_ref, vw_ref, cos_ref, sa_ref, sb_ref,
                     qo_ref, kco_ref, ks0_ref, ks1_ref, vs0_ref, vs1_ref, kw0_ref, kw1_ref,
                     vw0_ref, vw1_ref, *, tt):
    ct, sa, sb = cos_ref[...], sa_ref[...], sb_ref[...]
    qo_ref[0] = _bf(_rope(q_ref[0], ct, sa, sb) * (NSA_DH ** -0.5 * LOG2E))
    kco_ref[0] = _rope(kc_ref[0], ct, sa, sb)
    tok = pl.program_id(1) * tt + lax.broadcasted_iota(jnp.int32, (tt, LANES), 0)
    lane = lax.broadcasted_iota(jnp.int32, (tt, LANES), 1)
    onehot = jnp.where(lane - NSA_DH == tok // SLC_BLK, 1.0, 0.0)
    ks0_ref[0], ks1_ref[0] = _per_group(_rope(ks_ref[0], ct, sa, sb), onehot)
    kw0_ref[0], kw1_ref[0] = _per_group(_rope(kw_ref[0], ct, sa, sb), None)
    vs0_ref[0], vs1_ref[0] = _per_group(vs_ref[0], 1.0)
    vw0_ref[0], vw1_ref[0] = _per_group(vw_ref[0], 1.0)


def _nsa_prep(z3, cos_t, sin_a, sin_b, tt=512):
    bsz, t, _ = z3.shape
    qw = NSA_HEADS * LANES

    def zs(col0, width):
        blk = col0 // width
        return pl.BlockSpec((1, tt, width), lambda b, i: (b, i, blk))

    def os_(width):
        return pl.BlockSpec((1, tt, width), lambda b, i: (b, i, 0))

    tab = pl.BlockSpec((tt, LANES), lambda b, i: (i, 0))
    kv_bf = jax.ShapeDtypeStruct((bsz, t, LANES), jnp.bfloat16)
    return pl.pallas_call(
        functools.partial(_nsa_prep_kernel, tt=tt),
        grid=(bsz, t // tt),
        in_specs=[zs(C_NQ, qw), zs(C_NKC, LANES), zs(C_NKS, LANES), zs(C_NVS, LANES),
                  zs(C_NKW, LANES), zs(C_NVW, LANES), tab, tab, tab],
        out_specs=[os_(qw), os_(LANES)] + [os_(LANES)] * 8,
        out_shape=[jax.ShapeDtypeStruct((bsz, t, qw), jnp.bfloat16),
                   jax.ShapeDtypeStruct((bsz, t, LANES), jnp.float32)] + [kv_bf] * 8,
        compiler_params=pltpu.CompilerParams(
            dimension_semantics=("parallel", "parallel"), vmem_limit_bytes=VMEM_LIMIT),
        name="nsa_prep",
    )(z3, z3, z3, z3, z3, z3, cos_t, sin_a, sin_b)


def _compress_one(src_ref, pos_ref, w1_ref, w2_ref, out0_ref, out1_ref, ncmp):
    half = CMP_BLK // 2
    u = jnp.zeros((ncmp, 2 * CMP_HID), jnp.float32)
    v = jnp.zeros((ncmp, 2 * CMP_HID), jnp.float32)
    for j in range(half):
        rows = src_ref[0, pl.ds(j, ncmp, stride=CMP_STRIDE), :]
        u = u + _dot(_bf(rows + pos_ref[j:j + 1, :]), w1_ref[j])
        v = v + _dot(_bf(rows + pos_ref[half + j:half + j + 1, :]), w1_ref[half + j])
    hid = u + pltpu.roll(v, ncmp - 1, axis=0)
    out0_ref[0], out1_ref[0] = _per_group(_dot(_bf(jax.nn.gelu(hid)), w2_ref[...]), None)


def _compress_kernel(k_ref, v_ref, pk_ref, pv_ref, w1k_ref, w2k_ref, w1v_ref, w2v_ref,
                     kc0_ref, kc1_ref, vc0_ref, vc1_ref, *, ncmp):
    _compress_one(k_ref, pk_ref, w1k_ref, w2k_ref, kc0_ref, kc1_ref, ncmp)
    _compress_one(v_ref, pv_ref, w1v_ref, w2v_ref, vc0_ref, vc1_ref, ncmp)


def _compress(kc_rot, z3, pos_k, pos_v, w1k, w2k, w1v, w2v):
    bsz, t, _ = kc_rot.shape
    ncmp = t // CMP_STRIDE
    vblk = C_NVC // LANES
    full = lambda shape: pl.BlockSpec(shape, lambda b: (0,) * len(shape))
    out = jax.ShapeDtypeStruct((bsz, ncmp, LANES), jnp.bfloat16)
    return pl.pallas_call(
        functools.partial(_compress_kernel, ncmp=ncmp),
        grid=(bsz,),
        in_specs=[pl.BlockSpec((1, t, LANES), lambda b: (b, 0, 0)),
                  pl.BlockSpec((1, t, LANES), lambda b: (b, 0, vblk)),
                  full(pos_k.shape), full(pos_v.shape), full(w1k.shape), full(w2k.shape),
                  full(w1v.shape), full(w2v.shape)],
        out_specs=[pl.BlockSpec((1, ncmp, LANES), lambda b: (b, 0, 0))] * 4,
        out_shape=[out] * 4,
        compiler_params=pltpu.CompilerParams(
            dimension_semantics=("parallel",), vmem_limit_bytes=VMEM_LIMIT),
        name="nsa_compress",
    )(kc_rot, z3, pos_k, pos_v, w1k, w2k, w1v, w2v)


NEG = -1e30
SEL_KB = 256


def _select_blocks(imp, t_col, nslc):
    qb = imp.shape[0]
    blk = lax.broadcasted_iota(jnp.int32, (qb, LANES), 1) - NSA_DH
    cur = t_col // SLC_BLK
    valid = (blk >= 0) & (blk <= cur)
    forced = (blk == 0) | (blk == cur) | (blk == cur - 1)
    score = jnp.where(valid & forced, 1e9, jnp.where(valid, imp, -1e9))
    sc_t = score.T
    sub = 8
    nslab = (LANES - NSA_DH) // sub
    slabs = [sc_t[NSA_DH + sub * k:NSA_DH + sub * (k + 1), :] for k in range(nslab)]
    jj = lax.broadcasted_iota(jnp.int32, (sub, qb), 0)
    ranks = [jnp.zeros((sub, qb), jnp.float32) for _ in range(nslab)]
    for i in range(nslc):
        row = sc_t[NSA_DH + i:NSA_DH + i + 1, :]
        for k in range(nslab):
            if sub * k > i:
                ahead = row >= slabs[k]
            elif sub * (k + 1) <= i:
                ahead = row > slabs[k]
            else:
                ahead = (row > slabs[k]) | ((row == slabs[k]) & (jj + sub * k > i))
            ranks[k] = ranks[k] + jnp.where(ahead, 1.0, 0.0)
    sel_t = jnp.concatenate(
        [jnp.zeros((NSA_DH, qb), jnp.float32)]
        + [jnp.where(r < float(N_SEL), 1.0, 0.0) for r in ranks], axis=0)
    return valid & (sel_t.T > 0.5)


def _nsa_attn_kernel(q_ref, kc0_ref, kc1_ref, vc0_ref, vc1_ref, ks0_ref, ks1_ref, vs0_ref, vs1_ref,
                     kw0_ref, kw1_ref, vw0_ref, vw1_ref, gate_ref, ovl_ref, y_ref,
                     qa_scr, m_scr, acc_scr, *, seq):
    c = pl.program_id(1)
    qb = Q_BLOCK
    rep = NSA_REP
    rows = rep * qb
    t0 = c * qb
    t_col = t0 + lax.broadcasted_iota(jnp.int32, (qb, 1), 0)
    gates = _sigmoid(gate_ref[0])
    ncmp = kc0_ref.shape[1]
    lane_low = lax.broadcasted_iota(jnp.int32, (qb, LANES), 1) < NSA_DH
    n_idx = lax.broadcasted_iota(jnp.int32, (1, ncmp), 1)
    bias_c = jnp.where((n_idx * CMP_STRIDE + (CMP_BLK - 1)) <= t_col, 0.0, NEG)
    has_cmp = jnp.where(t_col >= CMP_BLK - 1, 1.0, 0.0)
    w_start = pl.multiple_of(jnp.maximum(t0 - WIN, 0), qb)
    wk = WIN + qb
    kp = w_start + lax.broadcasted_iota(jnp.int32, (1, wk), 1)
    bias_w = jnp.where((kp <= t_col) & (kp > t_col - WIN), 0.0, NEG)
    n_full = t0 // SEL_KB
    tail0 = pl.multiple_of(n_full * SEL_KB, SEL_KB)
    tok_tail = tail0 + lax.broadcasted_iota(jnp.int32, (1, SEL_KB), 1)
    bias_tail = jnp.where(tok_tail <= t_col, 0.0, NEG)

    def with_bias(s, bias):
        return (s.reshape(rep, qb, s.shape[-1]) + bias[None]).reshape(s.shape)

    groups = ((kc0_ref, vc0_ref, ks0_ref, vs0_ref, kw0_ref, vw0_ref),
              (kc1_ref, vc1_ref, ks1_ref, vs1_ref, kw1_ref, vw1_ref))
    o_cs, acc_ws = [], []
    for g, (kc_ref, vc_ref, ks_ref, vs_ref, kw_ref, vw_ref) in enumerate(groups):
        q_heads = [q_ref[0, :, (g * rep + r) * LANES:(g * rep + r + 1) * LANES] for r in range(rep)]
        qg = jnp.concatenate(q_heads, axis=0)

        s = with_bias(_dot_nt(qg, kc_ref[0]), bias_c)
        m = jnp.max(s, axis=-1, keepdims=True)
        p = (jnp.exp2(s - m).reshape(rep, qb, ncmp) * has_cmp[None]).reshape(rows, ncmp)
        l = jnp.sum(p, axis=-1, keepdims=True)
        p_c = p / jnp.where(l > 0.0, l, 1.0)
        o_cs.append(_dot(_bf(p_c), vc_ref[0]))
        p_sum = p_c[0:qb] + p_c[qb:2 * qb] + p_c[2 * qb:3 * qb] + p_c[3 * qb:4 * qb]
        imp = jnp.dot(p_sum, ovl_ref[...], preferred_element_type=jnp.float32,
                      precision=lax.Precision.HIGHEST)
        sel = _select_blocks(imp, t_col, seq // SLC_BLK)
        sel_bias = _bf(jnp.where(sel, 0.0, NEG))
        for r, qh in enumerate(q_heads):
            qa_scr[g, r * qb:(r + 1) * qb, :] = jnp.where(lane_low, qh, sel_bias)
        m_scr[g] = jnp.full((rows, LANES), NEG, jnp.float32)
        acc_scr[g] = jnp.zeros((rows, LANES), jnp.float32)

    def sel_step(k0, bias):
        state = [(m_scr[g], acc_scr[g]) for g in range(NSA_GROUPS)]
        new = []
        for g in range(NSA_GROUPS):
            ks_ref, vs_ref = groups[g][2], groups[g][3]
            m, acc = state[g]
            s = _dot_nt(qa_scr[g], ks_ref[0, pl.ds(k0, SEL_KB), :])
            if bias is not None:
                s = with_bias(s, bias)
            m_new = jnp.maximum(m, jnp.max(s, axis=-1, keepdims=True))
            p = jnp.exp2(s - jnp.concatenate([m_new] * (SEL_KB // LANES), axis=1))
            acc = jnp.exp2(m - m_new) * acc + _dot(_bf(p), vs_ref[0, pl.ds(k0, SEL_KB), :])
            new.append((m_new, acc))
        for g in range(NSA_GROUPS):
            m_scr[g], acc_scr[g] = new[g]

    def loop_body(i, carry):
        sel_step(pl.multiple_of(i * SEL_KB, SEL_KB), None)
        return carry

    lax.fori_loop(0, n_full, loop_body, 0)
    sel_step(tail0, bias_tail)
    carry = (None, acc_scr[0], None, acc_scr[1])

    for g in range(NSA_GROUPS):
        kw_ref, vw_ref = groups[g][4], groups[g][5]
        qg = jnp.concatenate(
            [q_ref[0, :, (g * rep + r) * LANES:(g * rep + r + 1) * LANES] for r in range(rep)], axis=0)
        s = with_bias(_dot_nt(qg, kw_ref[0, pl.ds(w_start, wk), :]), bias_w)
        m = jnp.max(s, axis=-1, keepdims=True)
        acc_ws.append(_dot(_bf(jnp.exp2(s - m)), vw_ref[0, pl.ds(w_start, wk), :]))

    def normalized(acc):
        return acc / pltpu.roll(acc, NSA_DH, axis=1)

    for g in range(NSA_GROUPS):
        o_c, o_s, o_w = o_cs[g], normalized(carry[2 * g + 1]), normalized(acc_ws[g])
        for r in range(rep):
            hd = g * rep + r
            sl = slice(r * qb, (r + 1) * qb)
            out = (o_c[sl] * gates[:, 3 * hd:3 * hd + 1] + o_s[sl] * gates[:, 3 * hd + 1:3 * hd + 2]
                   + o_w[sl] * gates[:, 3 * hd + 2:3 * hd + 3])
            y_ref[0, :, hd * LANES:(hd + 1) * LANES] = _bf(jnp.where(lane_low, out, 0.0))


def _nsa_attn(q_rot, kcs, kvs, z3, ovl):
    bsz, t, qw = q_rot.shape
    ncmp = kcs[0].shape[1]
    assert t % SEL_KB == 0 and t >= WIN + Q_BLOCK and t // SLC_BLK <= LANES - NSA_DH
    gblk = C_NGATE // LANES
    per_b = lambda rows: pl.BlockSpec((1, rows, LANES), lambda b, c: (b, 0, 0))
    return pl.pallas_call(
        functools.partial(_nsa_attn_kernel, seq=t),
        grid=(bsz, t // Q_BLOCK),
        in_specs=[pl.BlockSpec((1, Q_BLOCK, qw), lambda b, c: (b, c, 0))]
        + [per_b(ncmp)] * 4 + [per_b(t)] * 8
        + [pl.BlockSpec((1, Q_BLOCK, LANES), lambda b, c: (b, c, gblk)),
           pl.BlockSpec(ovl.shape, lambda b, c: (0, 0))],
        out_specs=pl.BlockSpec((1, Q_BLOCK, qw), lambda b, c: (b, c, 0)),
        out_shape=jax.ShapeDtypeStruct((bsz, t, qw), jnp.bfloat16),
        scratch_shapes=[pltpu.VMEM((NSA_GROUPS, NSA_REP * Q_BLOCK, LANES), jnp.bfloat16),
                        pltpu.VMEM((NSA_GROUPS, NSA_REP * Q_BLOCK, LANES), jnp.float32),
                        pltpu.VMEM((NSA_GROUPS, NSA_REP * Q_BLOCK, LANES), jnp.float32)],
        compiler_params=pltpu.CompilerParams(
            dimension_semantics=("parallel", "arbitrary"), vmem_limit_bytes=VMEM_LIMIT),
        name="nsa_attn",
    )(q_rot, *kcs, *kvs, z3, ovl)


def _merge_kernel(x_ref, yh_ref, yg_ref, yn_ref, mg_ref, wh_ref, wg_ref, wn_ref, wo_ref, o_ref):
    d = D_MODEL
    m = _sigmoid(mg_ref[:, 0:d]) * _dot(yh_ref[...], wh_ref[...])
    m = m + _sigmoid(mg_ref[:, d:2 * d]) * _dot(yg_ref[...], wg_ref[...])
    m = m + _sigmoid(mg_ref[:, 2 * d:3 * d]) * _dot(yn_ref[...], wn_ref[...])
    o_ref[...] = x_ref[...] + _dot(_bf(m), wo_ref[...])


def _merge(x2, yh, yg, yn, z2, wh, wg, wn, wo, tm=256):
    n, d = x2.shape
    mgblk = C_MG // (3 * d)
    row = lambda w: pl.BlockSpec((tm, w), lambda i: (i, 0))
    full = lambda a: pl.BlockSpec(a.shape, lambda i: (0, 0))
    return pl.pallas_call(
        _merge_kernel,
        grid=(n // tm,),
        in_specs=[row(d), row(yh.shape[1]), row(yg.shape[1]), row(yn.shape[1]),
                  pl.BlockSpec((tm, 3 * d), lambda i: (i, mgblk)),
                  full(wh), full(wg), full(wn), full(wo)],
        out_specs=row(d),
        out_shape=jax.ShapeDtypeStruct((n, d), jnp.float32),
        compiler_params=pltpu.CompilerParams(
            dimension_semantics=("parallel",), vmem_limit_bytes=VMEM_LIMIT),
        name="merge",
    )(x2, yh, yg, yn, z2, wh, wg, wn, wo)


def _ffn_kernel(x_ref, g_ref, wg_ref, wu_ref, wd_ref, fg_ref, o_ref, h_scr, acc_scr, *, final_norm):
    j = pl.program_id(1)

    @pl.when(j == 0)
    def _():
        x = x_ref[...]
        y = x * lax.rsqrt(jnp.mean(x * x, axis=-1, keepdims=True) + EPS)
        h_scr[...] = _bf(y * g_ref[...])
        acc_scr[...] = jnp.zeros_like(acc_scr)

    h = h_scr[...]
    a = _dot(h, wg_ref[...])
    u = _dot(h, wu_ref[...])
    acc_scr[...] += _dot(_bf(a * _sigmoid(a) * u), wd_ref[...])

    @pl.when(j == pl.num_programs(1) - 1)
    def _():
        out = x_ref[...] + acc_scr[...]
        if final_norm:
            out = out * lax.rsqrt(jnp.mean(out * out, axis=-1, keepdims=True) + EPS) * fg_ref[...]
        o_ref[...] = out


def _ffn(x2, g, wg, wu, wd, fg, final_norm, tm=512, tf=256):
    n, d = x2.shape
    ff = wg.shape[1]
    return pl.pallas_call(
        functools.partial(_ffn_kernel, final_norm=final_norm),
        grid=(n // tm, ff // tf),
        in_specs=[pl.BlockSpec((tm, d), lambda i, j: (i, 0)),
                  pl.BlockSpec((1, d), lambda i, j: (0, 0)),
                  pl.BlockSpec((d, tf), lambda i, j: (0, j)),
                  pl.BlockSpec((d, tf), lambda i, j: (0, j)),
                  pl.BlockSpec((tf, d), lambda i, j: (j, 0)),
                  pl.BlockSpec((1, d), lambda i, j: (0, 0))],
        out_specs=pl.BlockSpec((tm, d), lambda i, j: (i, 0)),
        out_shape=jax.ShapeDtypeStruct((n, d), jnp.float32),
        scratch_shapes=[pltpu.VMEM((tm, d), jnp.bfloat16), pltpu.VMEM((tm, d), jnp.float32)],
        compiler_params=pltpu.CompilerParams(
            dimension_semantics=("parallel", "arbitrary"), vmem_limit_bytes=VMEM_LIMIT),
        name="ffn",
    )(x2, g, wg, wu, wd, fg)


def _pad_heads(w, heads, width):
    lead = w.shape[:-1]
    w = w.reshape(lead + (heads, width))
    w = jnp.pad(w, [(0, 0)] * len(lead) + [(0, 0), (0, LANES - width)])
    return w.reshape(lead + (heads * LANES,))


def _layout_w_in(w):
    d = w.shape[0]
    o = 0
    parts = {}
    for name, n in (("hq", 512), ("hf", 512), ("hi", 512), ("hog", 512), ("gq", 256), ("gk", 256),
                    ("gv", 512), ("glr", 16), ("gog", 512), ("nq", 512), ("nkc", 128), ("nvc", 128),
                    ("nks", 128), ("nvs", 128), ("nkw", 128), ("nvw", 128), ("ngate", 24), ("mg", 3072)):
        parts[name] = w[:, o:o + n]
        o += n
    nq = _pad_heads(parts["nq"], NSA_HEADS, NSA_DH)
    padc = lambda a: jnp.pad(a, ((0, 0), (0, LANES - a.shape[1])))
    cols = [parts["hq"], parts["hf"], parts["hi"], parts["hog"],
            _pad_heads(parts["gq"], GLA_HEADS, GLA_DK), _pad_heads(parts["gk"], GLA_HEADS, GLA_DK),
            parts["gv"], parts["gog"], nq,
            parts["nkc"], parts["nvc"], parts["nks"], parts["nvs"], parts["nkw"], parts["nvw"],
            padc(parts["ngate"]), padc(parts["glr"]), parts["mg"]]
    out = jnp.concatenate(cols, axis=1)
    assert out.shape[1] == Z_COLS
    return _bf(out)


def _blockdiag2(w):
    z = jnp.zeros_like(w)
    top = jnp.concatenate([w, z], axis=-1)
    bot = jnp.concatenate([z, w], axis=-1)
    return jnp.concatenate([top, bot], axis=-2)


def _rope_tables(t):
    pos = jnp.arange(t, dtype=jnp.float32)
    inv_freq = ROPE_THETA ** (-jnp.arange(0, ROT_DIM, 2, dtype=jnp.float32) / ROT_DIM)
    ang = pos[:, None] * inv_freq[None, :]
    cos, sin = jnp.cos(ang), jnp.sin(ang)
    half = ROT_DIM // 2
    lane = jnp.arange(LANES) % NSA_DH
    first = lane < half
    second = (lane >= half) & (lane < ROT_DIM)
    idx = jnp.where(first, lane, jnp.where(second, lane - half, 0))
    cos_t = jnp.where(first | second, cos[:, idx], 1.0)
    sin_a = jnp.where(first, -sin[:, idx], 0.0)
    sin_b = jnp.where(second, sin[:, idx], 0.0)
    return cos_t, sin_a, sin_b


def kernel(x, norm1_g, w_in, hg_lb, hg_norm_g, gla_w2, gla_b, gla_norm_g, cmp_pos_k, cmp_pos_v,
           cmp_w1_k, cmp_w2_k, cmp_w1_v, cmp_w2_v, w_br_hg, w_br_gla, w_br_nsa, w_out,
           norm2_g, w_ffn_gate, w_ffn_up, w_ffn_down, final_norm_g):
    bsz, t, d = x.shape
    depth = w_in.shape[0]
    n = bsz * t
    f32 = jnp.float32

    cos_t, sin_a, sin_b = _rope_tables(t)
    lbs = jnp.cumsum(jax.nn.softmax(hg_lb.astype(f32), axis=0), axis=0)
    lbs = lbs - lbs[0]

    nslc = t // SLC_BLK
    ncmp_pad = t // CMP_STRIDE
    cmp_start = jnp.arange(ncmp_pad) * CMP_STRIDE
    blk = jnp.arange(LANES) - NSA_DH
    ovl = ((cmp_start[:, None] < (blk[None, :] + 1) * SLC_BLK)
           & (cmp_start[:, None] + CMP_BLK - 1 >= blk[None, :] * SLC_BLK)
           & (blk[None, :] >= 0) & (blk[None, :] < nslc)
           & (jnp.arange(ncmp_pad)[:, None] < ncmp_pad - 1)).astype(f32)

    x2 = x.reshape(n, d)
    for l in range(depth):
        z2 = _inproj(x2, norm1_g[l][None, :], _layout_w_in(w_in[l]))
        z3 = z2.reshape(bsz, t, Z_COLS)

        lb = lbs[l].reshape(HG_HEADS, LANES)
        zeros = jnp.zeros_like(lb)
        hg_par = jnp.stack([jnp.log(lb), jnp.log1p(-lb), 1.0 - lb,
                            hg_norm_g[l].reshape(HG_HEADS, LANES), zeros, zeros, zeros, zeros], axis=1)
        y_hg = _hgrn2(z3, hg_par)

        w2p = jnp.pad(_pad_heads(gla_w2[l], GLA_HEADS, GLA_DK), ((0, LANES - GLA_RANK), (0, 0)))
        w2p = _bf(w2p.reshape(LANES, GLA_HEADS, LANES).transpose(1, 0, 2))
        gb = _pad_heads(gla_b[l], GLA_HEADS, GLA_DK).reshape(GLA_HEADS, LANES)
        gla_par = jnp.stack([gb, gla_norm_g[l].reshape(GLA_HEADS, LANES),
                             zeros, zeros, zeros, zeros, zeros, zeros], axis=1)
        y_gla = _gla(z3, w2p, gla_par)

        q_rot, kc_rot, *kvs = _nsa_prep(z3, cos_t, sin_a, sin_b)
        tile2 = lambda p: jnp.concatenate([p, p], axis=1)
        w1k = _bf(_blockdiag2(cmp_w1_k[l].reshape(CMP_BLK, NSA_DH, CMP_HID)))
        w1v = _bf(_blockdiag2(cmp_w1_v[l].reshape(CMP_BLK, NSA_DH, CMP_HID)))
        kcs = _compress(kc_rot, z3, tile2(cmp_pos_k[l]), tile2(cmp_pos_v[l]),
                        w1k, _bf(_blockdiag2(cmp_w2_k[l])), w1v, _bf(_blockdiag2(cmp_w2_v[l])))
        y_nsa = _nsa_attn(q_rot, kcs, kvs, z3, ovl)

        wn = _pad_heads(w_br_nsa[l].T, NSA_HEADS, NSA_DH).T
        x2 = _merge(x2, y_hg.reshape(n, HG_DIM), y_gla.reshape(n, GLA_VDIM),
                    y_nsa.reshape(n, NSA_HEADS * LANES), z2,
                    _bf(w_br_hg[l]), _bf(w_br_gla[l]), _bf(wn), _bf(w_out[l]))

        x2 = _ffn(x2, norm2_g[l][None, :], _bf(w_ffn_gate[l]), _bf(w_ffn_up[l]), _bf(w_ffn_down[l]),
                  final_norm_g[None, :], final_norm=(l == depth - 1))
    return x2.reshape(bsz, t, d)
```

```python
import functools

import jax
import jax.numpy as jnp
from jax import lax
from jax.experimental import pallas as pl
from jax.experimental.pallas import tpu as pltpu

EPS = 1e-6
D_MODEL = 1024
LANES = 128

HG_HEADS = 4
HG_DIM = 512
GLA_HEADS = 4
GLA_DK = 64
GLA_KDIM = 256
GLA_VDIM = 512
GLA_RANK = 16
GLA_TAU = 16.0
LA_CHUNK = 64
LA_SUB = 16

NSA_HEADS = 8
NSA_GROUPS = 2
NSA_REP = 4
NSA_DH = 64
CMP_STRIDE = 16
CMP_BLK = 32
CMP_HID = 128
SLC_BLK = 64
N_SEL = 16
WIN = 512
Q_BLOCK = 128
ROPE_THETA = 500000.0
ROT_DIM = 16
FF_DIM = 2816

C_HQ, C_HF, C_HI, C_HOG = 0, 512, 1024, 1536
C_GQ, C_GK, C_GV, C_GOG = 2048, 2560, 3072, 3584
C_NQ = 4096
C_NKC, C_NVC, C_NKS, C_NVS, C_NKW, C_NVW = 5120, 5248, 5376, 5504, 5632, 5760
C_NGATE, C_GLR = 5888, 6016
C_MG = 6144
Z_COLS = 9216

VMEM_LIMIT = 56 * 1024 * 1024
LOG2E = 1.4426950408889634

_NT = (((1,), (1,)), ((), ()))
_TN = (((0,), (0,)), ((), ()))


def _bf(x):
    return x.astype(jnp.bfloat16)


def _dot(a, b):
    return jnp.dot(a, b, preferred_element_type=jnp.float32)


def _dot_nt(a, b):
    return lax.dot_general(a, b, _NT, preferred_element_type=jnp.float32)


def _dot_tn(a, b):
    return lax.dot_general(a, b, _TN, preferred_element_type=jnp.float32)


def _sigmoid(x):
    return 1.0 / (1.0 + jnp.exp(-x))


def _log_sigmoid(x):
    return jnp.minimum(x, 0.0) - jnp.log1p(jnp.exp(-jnp.abs(x)))


def _inproj_kernel(x_ref, g_ref, w_ref, z_ref, h_scr):
    @pl.when(pl.program_id(1) == 0)
    def _():
        x = x_ref[...]
        y = x * lax.rsqrt(jnp.mean(x * x, axis=-1, keepdims=True) + EPS)
        h_scr[...] = _bf(y * g_ref[...])

    z_ref[...] = _dot(h_scr[...], w_ref[...])


def _inproj(x2, g, w, tm=1024, tn=1024):
    n, d = x2.shape
    cols = w.shape[1]
    return pl.pallas_call(
        _inproj_kernel,
        grid=(n // tm, cols // tn),
        in_specs=[
            pl.BlockSpec((tm, d), lambda i, j: (i, 0)),
            pl.BlockSpec((1, d), lambda i, j: (0, 0)),
            pl.BlockSpec((d, tn), lambda i, j: (0, j)),
        ],
        out_specs=pl.BlockSpec((tm, tn), lambda i, j: (i, j)),
        out_shape=jax.ShapeDtypeStruct((n, cols), jnp.float32),
        scratch_shapes=[pltpu.VMEM((tm, d), jnp.bfloat16)],
        compiler_params=pltpu.CompilerParams(
            dimension_semantics=("parallel", "arbitrary"), vmem_limit_bytes=VMEM_LIMIT),
        name="inproj",
    )(x2, g, w)


def _chunk_cumsum(g, tb):
    pos = lax.broadcasted_iota(jnp.int32, (tb, 1), 0) % LA_CHUNK
    b = g
    shift = 1
    while shift < LA_CHUNK:
        b = b + jnp.where(pos >= shift, pltpu.roll(b, shift, axis=0), 0.0)
        shift *= 2
    return b


def _la_core(q, k, v, g, st_ref, tb):
    c, r = LA_CHUNK, LA_SUB
    n = tb // c
    nsub = c // r
    dk = q.shape[-1]
    b = _chunk_cumsum(g, tb)
    b3 = b.reshape(n, c, dk)
    q3 = q.reshape(n, c, dk)
    k3 = k.reshape(n, c, dk)
    v3 = _bf(v).reshape(n, c, dk)
    blast = b3[:, c - 1:c, :]
    q_state = _bf(q3 * jnp.exp(b3))
    k_state = _bf(k3 * jnp.exp(blast - b3))

    b4 = b.reshape(n * nsub, r, dk)
    ref = b4[:, 0:1, :]
    q_loc = _bf(q.reshape(n * nsub, r, dk) * jnp.exp(b4 - ref))
    s_idx = lax.broadcasted_iota(jnp.int32, (1, nsub, c, 1), 2)
    i_idx = lax.broadcasted_iota(jnp.int32, (1, nsub, c, 1), 1)
    e = jnp.where(s_idx < r * (i_idx + 1), ref.reshape(n, nsub, 1, dk) - b3[:, None, :, :], 0.0)
    k_loc = _bf(k3[:, None, :, :] * jnp.exp(e)).reshape(n * nsub, c, dk)
    a = jnp.einsum("utd,usd->uts", q_loc, k_loc, preferred_element_type=jnp.float32)
    a = a.reshape(n, c, c)
    t_i = lax.broadcasted_iota(jnp.int32, (1, c, c), 1)
    s_i = lax.broadcasted_iota(jnp.int32, (1, c, c), 2)
    a = jnp.where(s_i <= t_i, a, 0.0)
    o_intra = jnp.einsum("nts,nsv->ntv", _bf(a), v3, preferred_element_type=jnp.float32)

    st = st_ref[...]
    outs = []
    for ci in range(n):
        outs.append(_dot_nt(q_state[ci], _bf(st)) + o_intra[ci])
        st = st * jnp.exp(blast[ci]) + _dot_tn(v3[ci], k_state[ci])
    st_ref[...] = st
    return jnp.concatenate(outs, axis=0)


def _la_finish(o, og, gain, y_ref):
    o = o * lax.rsqrt(jnp.mean(o * o, axis=-1, keepdims=True) + EPS)
    y_ref[0] = _bf(o * gain * (og * _sigmoid(og)))


def _hgrn2_kernel(q_ref, f_ref, i_ref, og_ref, par_ref, y_ref, st_ref, *, tb):
    @pl.when(pl.program_id(2) == 0)
    def _():
        st_ref[...] = jnp.zeros_like(st_ref)

    log_lb = par_ref[0, 0:1, :]
    log1m_lb = par_ref[0, 1:2, :]
    one_m_lb = par_ref[0, 2:3, :]
    gain = par_ref[0, 3:4, :]
    hq = q_ref[0]
    hf = f_ref[0]
    q = hq * _sigmoid(hq)
    cterm = log1m_lb + _log_sigmoid(hf)
    log_f = jnp.maximum(log_lb, cterm) + jnp.log1p(jnp.exp(-jnp.abs(log_lb - cterm)))
    k = one_m_lb * _sigmoid(-hf)
    o = _la_core(q, k, i_ref[0], log_f, st_ref, tb)
    _la_finish(o, og_ref[0], gain, y_ref)


def _gla_kernel(q_ref, k_ref, v_ref, og_ref, lr_ref, w2_ref, par_ref, y_ref, st_ref, *, tb):
    @pl.when(pl.program_id(2) == 0)
    def _():
        st_ref[...] = jnp.zeros_like(st_ref)

    bias = par_ref[0, 0:1, :]
    gain = par_ref[0, 1:2, :]
    zz = _dot(_bf(lr_ref[0]), w2_ref[0]) + bias
    log_a = _log_sigmoid(zz) * (1.0 / GLA_TAU)
    q = q_ref[0] * (GLA_DK ** -0.5)
    o = _la_core(q, k_ref[0], v_ref[0], log_a, st_ref, tb)
    _la_finish(o, og_ref[0], gain, y_ref)


def _zspec(tb, col0):
    base = col0 // LANES
    return pl.BlockSpec((1, tb, LANES), lambda b, h, t: (b, t, base + h))


def _la_params():
    return pltpu.CompilerParams(
        dimension_semantics=("parallel", "parallel", "arbitrary"), vmem_limit_bytes=VMEM_LIMIT)


def _hgrn2(z3, par, tb=512):
    bsz, t, _ = z3.shape
    return pl.pallas_call(
        functools.partial(_hgrn2_kernel, tb=tb),
        grid=(bsz, HG_HEADS, t // tb),
        in_specs=[_zspec(tb, C_HQ), _zspec(tb, C_HF), _zspec(tb, C_HI), _zspec(tb, C_HOG),
                  pl.BlockSpec((1, 8, LANES), lambda b, h, t: (h, 0, 0))],
        out_specs=pl.BlockSpec((1, tb, LANES), lambda b, h, t: (b, t, h)),
        out_shape=jax.ShapeDtypeStruct((bsz, t, HG_DIM), jnp.bfloat16),
        scratch_shapes=[pltpu.VMEM((LANES, LANES), jnp.float32)],
        compiler_params=_la_params(),
        name="hgrn2",
    )(z3, z3, z3, z3, par)


def _gla(z3, w2p, par, tb=512):
    bsz, t, _ = z3.shape
    lr_blk = C_GLR // LANES
    return pl.pallas_call(
        functools.partial(_gla_kernel, tb=tb),
        grid=(bsz, GLA_HEADS, t // tb),
        in_specs=[_zspec(tb, C_GQ), _zspec(tb, C_GK), _zspec(tb, C_GV), _zspec(tb, C_GOG),
                  pl.BlockSpec((1, tb, LANES), lambda b, h, t: (b, t, lr_blk)),
                  pl.BlockSpec((1, LANES, LANES), lambda b, h, t: (h, 0, 0)),
                  pl.BlockSpec((1, 8, LANES), lambda b, h, t: (h, 0, 0))],
        out_specs=pl.BlockSpec((1, tb, LANES), lambda b, h, t: (b, t, h)),
        out_shape=jax.ShapeDtypeStruct((bsz, t, GLA_VDIM), jnp.bfloat16),
        scratch_shapes=[pltpu.VMEM((LANES, LANES), jnp.float32)],
        compiler_params=_la_params(),
        name="gla",
    )(z3, z3, z3, z3, z3, w2p, par)


def _rope(x, cos_t, sin_a, sin_b):
    half = ROT_DIM // 2
    slabs = []
    for i in range(x.shape[-1] // LANES):
        xs = x[:, i * LANES:(i + 1) * LANES]
        slabs.append(xs * cos_t + pltpu.roll(xs, LANES - half, axis=1) * sin_a
                     + pltpu.roll(xs, half, axis=1) * sin_b)
    return slabs[0] if len(slabs) == 1 else jnp.concatenate(slabs, axis=1)


def _per_group(x, filler):
    swapped = pltpu.roll(x, NSA_DH, axis=1)
    if filler is None:
        return _bf(x), _bf(swapped)
    low = lax.broadcasted_iota(jnp.int32, x.shape, 1) < NSA_DH
    return _bf(jnp.where(low, x, filler)), _bf(jnp.where(low, swapped, filler))


def _nsa_prep_kernel(q_ref, kc_ref, ks_ref, vs_ref, kw_ref, vw_ref, cos_ref, sa_ref, sb_ref,
                     qo_ref, kco_ref, ks0_ref, ks1_ref, vs0_ref, vs1_ref, kw0_ref, kw1_ref,
                     vw0_ref, vw1_ref, *, tt):
    ct, sa, sb = cos_ref[...], sa_ref[...], sb_ref[...]
    qo_ref[0] = _bf(_rope(q_ref[0], ct, sa, sb) * (NSA_DH ** -0.5 * LOG2E))
    kco_ref[0] = _rope(kc_ref[0], ct, sa, sb)
    tok = pl.program_id(1) * tt + lax.broadcasted_iota(jnp.int32, (tt, LANES), 0)
    lane = lax.broadcasted_iota(jnp.int32, (tt, LANES), 1)
    onehot = jnp.where(lane - NSA_DH == tok // SLC_BLK, 1.0, 0.0)
    ks0_ref[0], ks1_ref[0] = _per_group(_rope(ks_ref[0], ct, sa, sb), onehot)
    kw0_ref[0], kw1_ref[0] = _per_group(_rope(kw_ref[0], ct, sa, sb), None)
    vs0_ref[0], vs1_ref[0] = _per_group(vs_ref[0], 1.0)
    vw0_ref[0], vw1_ref[0] = _per_group(vw_ref[0], 1.0)


def _nsa_prep(z3, cos_t, sin_a, sin_b, tt=512):
    bsz, t, _ = z3.shape
    qw = NSA_HEADS * LANES

    def zs(col0, width):
        blk = col0 // width
        return pl.BlockSpec((1, tt, width), lambda b, i: (b, i, blk))

    def os_(width):
        return pl.BlockSpec((1, tt, width), lambda b, i: (b, i, 0))

    tab = pl.BlockSpec((tt, LANES), lambda b, i: (i, 0))
    kv_bf = jax.ShapeDtypeStruct((bsz, t, LANES), jnp.bfloat16)
    return pl.pallas_call(
        functools.partial(_nsa_prep_kernel, tt=tt),
        grid=(bsz, t // tt),
        in_specs=[zs(C_NQ, qw), zs(C_NKC, LANES), zs(C_NKS, LANES), zs(C_NVS, LANES),
                  zs(C_NKW, LANES), zs(C_NVW, LANES), tab, tab, tab],
        out_specs=[os_(qw), os_(LANES)] + [os_(LANES)] * 8,
        out_shape=[jax.ShapeDtypeStruct((bsz, t, qw), jnp.bfloat16),
                   jax.ShapeDtypeStruct((bsz, t, LANES), jnp.float32)] + [kv_bf] * 8,
        compiler_params=pltpu.CompilerParams(
            dimension_semantics=("parallel", "parallel"), vmem_limit_bytes=VMEM_LIMIT),
        name="nsa_prep",
    )(z3, z3, z3, z3, z3, z3, cos_t, sin_a, sin_b)


def _compress_one(src_ref, pos_ref, w1_ref, w2_ref, out0_ref, out1_ref, ncmp):
    half = CMP_BLK // 2
    u = jnp.zeros((ncmp, 2 * CMP_HID), jnp.float32)
    v = jnp.zeros((ncmp, 2 * CMP_HID), jnp.float32)
    for j in range(half):
        rows = src_ref[0, pl.ds(j, ncmp, stride=CMP_STRIDE), :]
        u = u + _dot(_bf(rows + pos_ref[j:j + 1, :]), w1_ref[j])
        v = v + _dot(_bf(rows + pos_ref[half + j:half + j + 1, :]), w1_ref[half + j])
    hid = u + pltpu.roll(v, ncmp - 1, axis=0)
    out0_ref[0], out1_ref[0] = _per_group(_dot(_bf(jax.nn.gelu(hid)), w2_ref[...]), None)


def _compress_kernel(k_ref, v_ref, pk_ref, pv_ref, w1k_ref, w2k_ref, w1v_ref, w2v_ref,
                     kc0_ref, kc1_ref, vc0_ref, vc1_ref, *, ncmp):
    _compress_one(k_ref, pk_ref, w1k_ref, w2k_ref, kc0_ref, kc1_ref, ncmp)
    _compress_one(v_ref, pv_ref, w1v_ref, w2v_ref, vc0_ref, vc1_ref, ncmp)


def _compress(kc_rot, z3, pos_k, pos_v, w1k, w2k, w1v, w2v):
    bsz, t, _ = kc_rot.shape
    ncmp = t // CMP_STRIDE
    vblk = C_NVC // LANES
    full = lambda shape: pl.BlockSpec(shape, lambda b: (0,) * len(shape))
    out = jax.ShapeDtypeStruct((bsz, ncmp, LANES), jnp.bfloat16)
    return pl.pallas_call(
        functools.partial(_compress_kernel, ncmp=ncmp),
        grid=(bsz,),
        in_specs=[pl.BlockSpec((1, t, LANES), lambda b: (b, 0, 0)),
                  pl.BlockSpec((1, t, LANES), lambda b: (b, 0, vblk)),
                  full(pos_k.shape), full(pos_v.shape), full(w1k.shape), full(w2k.shape),
                  full(w1v.shape), full(w2v.shape)],
        out_specs=[pl.BlockSpec((1, ncmp, LANES), lambda b: (b, 0, 0))] * 4,
        out_shape=[out] * 4,
        compiler_params=pltpu.CompilerParams(
            dimension_semantics=("parallel",), vmem_limit_bytes=VMEM_LIMIT),
        name="nsa_compress",
    )(kc_rot, z3, pos_k, pos_v, w1k, w2k, w1v, w2v)


NEG = -1e30
SEL_KB = 256


def _select_blocks(imp, t_col, nslc):
    qb = imp.shape[0]
    blk = lax.broadcasted_iota(jnp.int32, (qb, LANES), 1) - NSA_DH
    cur = t_col // SLC_BLK
    valid = (blk >= 0) & (blk <= cur)
    forced = (blk == 0) | (blk == cur) | (blk == cur - 1)
    score = jnp.where(valid & forced, 1e9, jnp.where(valid, imp, -1e9))
    sc_t = score.T
    sub = 8
    nslab = (LANES - NSA_DH) // sub
    slabs = [sc_t[NSA_DH + sub * k:NSA_DH + sub * (k + 1), :] for k in range(nslab)]
    jj = lax.broadcasted_iota(jnp.int32, (sub, qb), 0)
    ranks = [jnp.zeros((sub, qb), jnp.float32) for _ in range(nslab)]
    for i in range(nslc):
        row = sc_t[NSA_DH + i:NSA_DH + i + 1, :]
        for k in range(nslab):
            if sub * k > i:
                ahead = row >= slabs[k]
            elif sub * (k + 1) <= i:
                ahead = row > slabs[k]
            else:
                ahead = (row > slabs[k]) | ((row == slabs[k]) & (jj + sub * k > i))
            ranks[k] = ranks[k] + jnp.where(ahead, 1.0, 0.0)
    sel_t = jnp.concatenate(
        [jnp.zeros((NSA_DH, qb), jnp.float32)]
        + [jnp.where(r < float(N_SEL), 1.0, 0.0) for r in ranks], axis=0)
    return valid & (sel_t.T > 0.5)


def _nsa_attn_kernel(q_ref, kc0_ref, kc1_ref, vc0_ref, vc1_ref, ks0_ref, ks1_ref, vs0_ref, vs1_ref,
                     kw0_ref, kw1_ref, vw0_ref, vw1_ref, gate_ref, ovl_ref, y_ref,
                     qa_scr, m_scr, acc_scr, *, seq):
    c = pl.program_id(1)
    qb = Q_BLOCK
    rep = NSA_REP
    rows = rep * qb
    t0 = c * qb
    t_col = t0 + lax.broadcasted_iota(jnp.int32, (qb, 1), 0)
    gates = _sigmoid(gate_ref[0])
    ncmp = kc0_ref.shape[1]
    lane_low = lax.broadcasted_iota(jnp.int32, (qb, LANES), 1) < NSA_DH
    n_idx = lax.broadcasted_iota(jnp.int32, (1, ncmp), 1)
    bias_c = jnp.where((n_idx * CMP_STRIDE + (CMP_BLK - 1)) <= t_col, 0.0, NEG)
    has_cmp = jnp.where(t_col >= CMP_BLK - 1, 1.0, 0.0)
    w_start = pl.multiple_of(jnp.maximum(t0 - WIN, 0), qb)
    wk = WIN + qb
    kp = w_start + lax.broadcasted_iota(jnp.int32, (1, wk), 1)
    bias_w = jnp.where((kp <= t_col) & (kp > t_col - WIN), 0.0, NEG)
    n_full = t0 // SEL_KB
    tail0 = pl.multiple_of(n_full * SEL_KB, SEL_KB)
    tok_tail = tail0 + lax.broadcasted_iota(jnp.int32, (1, SEL_KB), 1)
    bias_tail = jnp.where(tok_tail <= t_col, 0.0, NEG)

    def with_bias(s, bias):
        return (s.reshape(rep, qb, s.shape[-1]) + bias[None]).reshape(s.shape)

    groups = ((kc0_ref, vc0_ref, ks0_ref, vs0_ref, kw0_ref, vw0_ref),
              (kc1_ref, vc1_ref, ks1_ref, vs1_ref, kw1_ref, vw1_ref))
    o_cs, acc_ws = [], []
    for g, (kc_ref, vc_ref, ks_ref, vs_ref, kw_ref, vw_ref) in enumerate(groups):
        q_heads = [q_ref[0, :, (g * rep + r) * LANES:(g * rep + r + 1) * LANES] for r in range(rep)]
        qg = jnp.concatenate(q_heads, axis=0)

        s = with_bias(_dot_nt(qg, kc_ref[0]), bias_c)
        m = jnp.max(s, axis=-1, keepdims=True)
        p = (jnp.exp2(s - m).reshape(rep, qb, ncmp) * has_cmp[None]).reshape(rows, ncmp)
        l = jnp.sum(p, axis=-1, keepdims=True)
        p_c = p / jnp.where(l > 0.0, l, 1.0)
        o_cs.append(_dot(_bf(p_c), vc_ref[0]))
        p_sum = p_c[0:qb] + p_c[qb:2 * qb] + p_c[2 * qb:3 * qb] + p_c[3 * qb:4 * qb]
        imp = jnp.dot(p_sum, ovl_ref[...], preferred_element_type=jnp.float32,
                      precision=lax.Precision.HIGHEST)
        sel = _select_blocks(imp, t_col, seq // SLC_BLK)
        sel_bias = _bf(jnp.where(sel, 0.0, NEG))
        for r, qh in enumerate(q_heads):
            qa_scr[g, r * qb:(r + 1) * qb, :] = jnp.where(lane_low, qh, sel_bias)
        m_scr[g] = jnp.full((rows, LANES), NEG, jnp.float32)
        acc_scr[g] = jnp.zeros((rows, LANES), jnp.float32)

    def sel_step(k0, bias):
        state = [(m_scr[g], acc_scr[g]) for g in range(NSA_GROUPS)]
        new = []
        for g in range(NSA_GROUPS):
            ks_ref, vs_ref = groups[g][2], groups[g][3]
            m, acc = state[g]
            s = _dot_nt(qa_scr[g], ks_ref[0, pl.ds(k0, SEL_KB), :])
            if bias is not None:
                s = with_bias(s, bias)
            m_new = jnp.maximum(m, jnp.max(s, axis=-1, keepdims=True))
            p = jnp.exp2(s - jnp.concatenate([m_new] * (SEL_KB // LANES), axis=1))
            acc = jnp.exp2(m - m_new) * acc + _dot(_bf(p), vs_ref[0, pl.ds(k0, SEL_KB), :])
            new.append((m_new, acc))
        for g in range(NSA_GROUPS):
            m_scr[g], acc_scr[g] = new[g]

    def loop_body(i, carry):
        sel_step(pl.multiple_of(i * SEL_KB, SEL_KB), None)
        return carry

    lax.fori_loop(0, n_full, loop_body, 0)
    sel_step(tail0, bias_tail)
    carry = (None, acc_scr[0], None, acc_scr[1])

    for g in range(NSA_GROUPS):
        kw_ref, vw_ref = groups[g][4], groups[g][5]
        qg = jnp.concatenate(
            [q_ref[0, :, (g * rep + r) * LANES:(g * rep + r + 1) * LANES] for r in range(rep)], axis=0)
        s = with_bias(_dot_nt(qg, kw_ref[0, pl.ds(w_start, wk), :]), bias_w)
        m = jnp.max(s, axis=-1, keepdims=True)
        acc_ws.append(_dot(_bf(jnp.exp2(s - m)), vw_ref[0, pl.ds(w_start, wk), :]))

    def normalized(acc):
        return acc / pltpu.roll(acc, NSA_DH, axis=1)

    for g in range(NSA_GROUPS):
        o_c, o_s, o_w = o_cs[g], normalized(carry[2 * g + 1]), normalized(acc_ws[g])
        for r in range(rep):
            hd = g * rep + r
            sl = slice(r * qb, (r + 1) * qb)
            out = (o_c[sl] * gates[:, 3 * hd:3 * hd + 1] + o_s[sl] * gates[:, 3 * hd + 1:3 * hd + 2]
                   + o_w[sl] * gates[:, 3 * hd + 2:3 * hd + 3])
            y_ref[0, :, hd * LANES:(hd + 1) * LANES] = _bf(jnp.where(lane_low, out, 0.0))


def _nsa_attn(q_rot, kcs, kvs, z3, ovl):
    bsz, t, qw = q_rot.shape
    ncmp = kcs[0].shape[1]
    assert t % SEL_KB == 0 and t >= WIN + Q_BLOCK and t // SLC_BLK <= LANES - NSA_DH
    gblk = C_NGATE // LANES
    per_b = lambda rows: pl.BlockSpec((1, rows, LANES), lambda b, c: (b, 0, 0))
    return pl.pallas_call(
        functools.partial(_nsa_attn_kernel, seq=t),
        grid=(bsz, t // Q_BLOCK),
        in_specs=[pl.BlockSpec((1, Q_BLOCK, qw), lambda b, c: (b, c, 0))]
        + [per_b(ncmp)] * 4 + [per_b(t)] * 8
        + [pl.BlockSpec((1, Q_BLOCK, LANES), lambda b, c: (b, c, gblk)),
           pl.BlockSpec(ovl.shape, lambda b, c: (0, 0))],
        out_specs=pl.BlockSpec((1, Q_BLOCK, qw), lambda b, c: (b, c, 0)),
        out_shape=jax.ShapeDtypeStruct((bsz, t, qw), jnp.bfloat16),
        scratch_shapes=[pltpu.VMEM((NSA_GROUPS, NSA_REP * Q_BLOCK, LANES), jnp.bfloat16),
                        pltpu.VMEM((NSA_GROUPS, NSA_REP * Q_BLOCK, LANES), jnp.float32),
                        pltpu.VMEM((NSA_GROUPS, NSA_REP * Q_BLOCK, LANES), jnp.float32)],
        compiler_params=pltpu.CompilerParams(
            dimension_semantics=("parallel", "arbitrary"), vmem_limit_bytes=VMEM_LIMIT),
        name="nsa_attn",
    )(q_rot, *kcs, *kvs, z3, ovl)


def _merge_kernel(x_ref, yh_ref, yg_ref, yn_ref, mg_ref, wh_ref, wg_ref, wn_ref, wo_ref, o_ref):
    d = D_MODEL
    m = _sigmoid(mg_ref[:, 0:d]) * _dot(yh_ref[...], wh_ref[...])
    m = m + _sigmoid(mg_ref[:, d:2 * d]) * _dot(yg_ref[...], wg_ref[...])
    m = m + _sigmoid(mg_ref[:, 2 * d:3 * d]) * _dot(yn_ref[...], wn_ref[...])
    o_ref[...] = x_ref[...] + _dot(_bf(m), wo_ref[...])


def _merge(x2, yh, yg, yn, z2, wh, wg, wn, wo, tm=512):
    n, d = x2.shape
    mgblk = C_MG // (3 * d)
    row = lambda w: pl.BlockSpec((tm, w), lambda i: (i, 0))
    full = lambda a: pl.BlockSpec(a.shape, lambda i: (0, 0))
    return pl.pallas_call(
        _merge_kernel,
        grid=(n // tm,),
        in_specs=[row(d), row(yh.shape[1]), row(yg.shape[1]), row(yn.shape[1]),
                  pl.BlockSpec((tm, 3 * d), lambda i: (i, mgblk)),
                  full(wh), full(wg), full(wn), full(wo)],
        out_specs=row(d),
        out_shape=jax.ShapeDtypeStruct((n, d), jnp.float32),
        compiler_params=pltpu.CompilerParams(
            dimension_semantics=("parallel",), vmem_limit_bytes=VMEM_LIMIT),
        name="merge",
    )(x2, yh, yg, yn, z2, wh, wg, wn, wo)


def _ffn_kernel(x_ref, g_ref, wg_ref, wu_ref, wd_ref, fg_ref, o_ref, h_scr, acc_scr, *, final_norm):
    j = pl.program_id(1)

    @pl.when(j == 0)
    def _():
        x = x_ref[...]
        y = x * lax.rsqrt(jnp.mean(x * x, axis=-1, keepdims=True) + EPS)
        h_scr[...] = _bf(y * g_ref[...])
        acc_scr[...] = jnp.zeros_like(acc_scr)

    h = h_scr[...]
    a = _dot(h, wg_ref[...])
    u = _dot(h, wu_ref[...])
    acc_scr[...] += _dot(_bf(a * _sigmoid(a) * u), wd_ref[...])

    @pl.when(j == pl.num_programs(1) - 1)
    def _():
        out = x_ref[...] + acc_scr[...]
        if final_norm:
            out = out * lax.rsqrt(jnp.mean(out * out, axis=-1, keepdims=True) + EPS) * fg_ref[...]
        o_ref[...] = out


def _ffn(x2, g, wg, wu, wd, fg, final_norm, tm=1024, tf=256):
    n, d = x2.shape
    ff = wg.shape[1]
    return pl.pallas_call(
        functools.partial(_ffn_kernel, final_norm=final_norm),
        grid=(n // tm, ff // tf),
        in_specs=[pl.BlockSpec((tm, d), lambda i, j: (i, 0)),
                  pl.BlockSpec((1, d), lambda i, j: (0, 0)),
                  pl.BlockSpec((d, tf), lambda i, j: (0, j)),
                  pl.BlockSpec((d, tf), lambda i, j: (0, j)),
                  pl.BlockSpec((tf, d), lambda i, j: (j, 0)),
                  pl.BlockSpec((1, d), lambda i, j: (0, 0))],
        out_specs=pl.BlockSpec((tm, d), lambda i, j: (i, 0)),
        out_shape=jax.ShapeDtypeStruct((n, d), jnp.float32),
        scratch_shapes=[pltpu.VMEM((tm, d), jnp.bfloat16), pltpu.VMEM((tm, d), jnp.float32)],
        compiler_params=pltpu.CompilerParams(
            dimension_semantics=("parallel", "arbitrary"), vmem_limit_bytes=VMEM_LIMIT),
        name="ffn",
    )(x2, g, wg, wu, wd, fg)


def _pad_heads(w, heads, width):
    lead = w.shape[:-1]
    w = w.reshape(lead + (heads, width))
    w = jnp.pad(w, [(0, 0)] * len(lead) + [(0, 0), (0, LANES - width)])
    return w.reshape(lead + (heads * LANES,))


def _layout_w_in(w):
    d = w.shape[0]
    o = 0
    parts = {}
    for name, n in (("hq", 512), ("hf", 512), ("hi", 512), ("hog", 512), ("gq", 256), ("gk", 256),
                    ("gv", 512), ("glr", 16), ("gog", 512), ("nq", 512), ("nkc", 128), ("nvc", 128),
                    ("nks", 128), ("nvs", 128), ("nkw", 128), ("nvw", 128), ("ngate", 24), ("mg", 3072)):
        parts[name] = w[:, o:o + n]
        o += n
    nq = _pad_heads(parts["nq"], NSA_HEADS, NSA_DH)
    padc = lambda a: jnp.pad(a, ((0, 0), (0, LANES - a.shape[1])))
    cols = [parts["hq"], parts["hf"], parts["hi"], parts["hog"],
            _pad_heads(parts["gq"], GLA_HEADS, GLA_DK), _pad_heads(parts["gk"], GLA_HEADS, GLA_DK),
            parts["gv"], parts["gog"], nq,
            parts["nkc"], parts["nvc"], parts["nks"], parts["nvs"], parts["nkw"], parts["nvw"],
            padc(parts["ngate"]), padc(parts["glr"]), parts["mg"]]
    out = jnp.concatenate(cols, axis=1)
    assert out.shape[1] == Z_COLS
    return _bf(out)


def _blockdiag2(w):
    z = jnp.zeros_like(w)
    top = jnp.concatenate([w, z], axis=-1)
    bot = jnp.concatenate([z, w], axis=-1)
    return jnp.concatenate([top, bot], axis=-2)


def _rope_tables(t):
    pos = jnp.arange(t, dtype=jnp.float32)
    inv_freq = ROPE_THETA ** (-jnp.arange(0, ROT_DIM, 2, dtype=jnp.float32) / ROT_DIM)
    ang = pos[:, None] * inv_freq[None, :]
    cos, sin = jnp.cos(ang), jnp.sin(ang)
    half = ROT_DIM // 2
    lane = jnp.arange(LANES) % NSA_DH
    first = lane < half
    second = (lane >= half) & (lane < ROT_DIM)
    idx = jnp.where(first, lane, jnp.where(second, lane - half, 0))
    cos_t = jnp.where(first | second, cos[:, idx], 1.0)
    sin_a = jnp.where(first, -sin[:, idx], 0.0)
    sin_b = jnp.where(second, sin[:, idx], 0.0)
    return cos_t, sin_a, sin_b


def kernel(x, norm1_g, w_in, hg_lb, hg_norm_g, gla_w2, gla_b, gla_norm_g, cmp_pos_k, cmp_pos_v,
           cmp_w1_k, cmp_w2_k, cmp_w1_v, cmp_w2_v, w_br_hg, w_br_gla, w_br_nsa, w_out,
           norm2_g, w_ffn_gate, w_ffn_up, w_ffn_down, final_norm_g):
    bsz, t, d = x.shape
    depth = w_in.shape[0]
    n = bsz * t
    f32 = jnp.float32

    cos_t, sin_a, sin_b = _rope_tables(t)
    lbs = jnp.cumsum(jax.nn.softmax(hg_lb.astype(f32), axis=0), axis=0)
    lbs = lbs - lbs[0]

    nslc = t // SLC_BLK
    ncmp_pad = t // CMP_STRIDE
    cmp_start = jnp.arange(ncmp_pad) * CMP_STRIDE
    blk = jnp.arange(LANES) - NSA_DH
    ovl = ((cmp_start[:, None] < (blk[None, :] + 1) * SLC_BLK)
           & (cmp_start[:, None] + CMP_BLK - 1 >= blk[None, :] * SLC_BLK)
           & (blk[None, :] >= 0) & (blk[None, :] < nslc)
           & (jnp.arange(ncmp_pad)[:, None] < ncmp_pad - 1)).astype(f32)

    x2 = x.reshape(n, d)
    for l in range(depth):
        z2 = _inproj(x2, norm1_g[l][None, :], _layout_w_in(w_in[l]))
        z3 = z2.reshape(bsz, t, Z_COLS)

        lb = lbs[l].reshape(HG_HEADS, LANES)
        zeros = jnp.zeros_like(lb)
        hg_par = jnp.stack([jnp.log(lb), jnp.log1p(-lb), 1.0 - lb,
                            hg_norm_g[l].reshape(HG_HEADS, LANES), zeros, zeros, zeros, zeros], axis=1)
        y_hg = _hgrn2(z3, hg_par)

        w2p = jnp.pad(_pad_heads(gla_w2[l], GLA_HEADS, GLA_DK), ((0, LANES - GLA_RANK), (0, 0)))
        w2p = _bf(w2p.reshape(LANES, GLA_HEADS, LANES).transpose(1, 0, 2))
        gb = _pad_heads(gla_b[l], GLA_HEADS, GLA_DK).reshape(GLA_HEADS, LANES)
        gla_par = jnp.stack([gb, gla_norm_g[l].reshape(GLA_HEADS, LANES),
                             zeros, zeros, zeros, zeros, zeros, zeros], axis=1)
        y_gla = _gla(z3, w2p, gla_par)

        q_rot, kc_rot, *kvs = _nsa_prep(z3, cos_t, sin_a, sin_b)
        tile2 = lambda p: jnp.concatenate([p, p], axis=1)
        w1k = _bf(_blockdiag2(cmp_w1_k[l].reshape(CMP_BLK, NSA_DH, CMP_HID)))
        w1v = _bf(_blockdiag2(cmp_w1_v[l].reshape(CMP_BLK, NSA_DH, CMP_HID)))
        kcs = _compress(kc_rot, z3, tile2(cmp_pos_k[l]), tile2(cmp_pos_v[l]),
                        w1k, _bf(_blockdiag2(cmp_w2_k[l])), w1v, _bf(_blockdiag2(cmp_w2_v[l])))
        y_nsa = _nsa_attn(q_rot, kcs, kvs, z3, ovl)

        wn = _pad_heads(w_br_nsa[l].T, NSA_HEADS, NSA_DH).T
        x2 = _merge(x2, y_hg.reshape(n, HG_DIM), y_gla.reshape(n, GLA_VDIM),
                    y_nsa.reshape(n, NSA_HEADS * LANES), z2,
                    _bf(w_br_hg[l]), _bf(w_br_gla[l]), _bf(wn), _bf(w_out[l]))

        x2 = _ffn(x2, norm2_g[l][None, :], _bf(w_ffn_gate[l]), _bf(w_ffn_up[l]), _bf(w_ffn_down[l]),
                  final_norm_g[None, :], final_norm=(l == depth - 1))
    return x2.reshape(bsz, t, d)
```

```python
import functools

import jax
import jax.numpy as jnp
from jax import lax
from jax.experimental import pallas as pl
from jax.experimental.pallas import tpu as pltpu

EPS = 1e-6
D_MODEL = 1024
LANES = 128

HG_HEADS = 4
HG_DIM = 512
GLA_HEADS = 4
GLA_DK = 64
GLA_KDIM = 256
GLA_VDIM = 512
GLA_RANK = 16
GLA_TAU = 16.0
LA_CHUNK = 64
LA_SUB = 16

NSA_HEADS = 8
NSA_GROUPS = 2
NSA_REP = 4
NSA_DH = 64
CMP_STRIDE = 16
CMP_BLK = 32
CMP_HID = 128
SLC_BLK = 64
N_SEL = 16
WIN = 512
Q_BLOCK = 128
ROPE_THETA = 500000.0
ROT_DIM = 16
FF_DIM = 2816

C_HQ, C_HF, C_HI, C_HOG = 0, 512, 1024, 1536
C_GQ, C_GK, C_GV, C_GOG = 2048, 2560, 3072, 3584
C_NQ = 4096
C_NKC, C_NVC, C_NKS, C_NVS, C_NKW, C_NVW = 5120, 5248, 5376, 5504, 5632, 5760
C_NGATE, C_GLR = 5888, 6016
C_MG = 6144
Z_COLS = 9216

VMEM_LIMIT = 56 * 1024 * 1024
LOG2E = 1.4426950408889634

_NT = (((1,), (1,)), ((), ()))
_TN = (((0,), (0,)), ((), ()))


def _bf(x):
    return x.astype(jnp.bfloat16)


def _dot(a, b):
    return jnp.dot(a, b, preferred_element_type=jnp.float32)


def _dot_nt(a, b):
    return lax.dot_general(a, b, _NT, preferred_element_type=jnp.float32)


def _dot_tn(a, b):
    return lax.dot_general(a, b, _TN, preferred_element_type=jnp.float32)


def _sigmoid(x):
    return 1.0 / (1.0 + jnp.exp(-x))


def _log_sigmoid(x):
    return jnp.minimum(x, 0.0) - jnp.log1p(jnp.exp(-jnp.abs(x)))


def _inproj_kernel(x_ref, g_ref, w_ref, z_ref, h_scr):
    @pl.when(pl.program_id(1) == 0)
    def _():
        x = x_ref[...]
        y = x * lax.rsqrt(jnp.mean(x * x, axis=-1, keepdims=True) + EPS)
        h_scr[...] = _bf(y * g_ref[...])

    z_ref[...] = _dot(h_scr[...], w_ref[...])


def _inproj(x2, g, w, tm=1024, tn=1024):
    n, d = x2.shape
    cols = w.shape[1]
    return pl.pallas_call(
        _inproj_kernel,
        grid=(n // tm, cols // tn),
        in_specs=[
            pl.BlockSpec((tm, d), lambda i, j: (i, 0)),
            pl.BlockSpec((1, d), lambda i, j: (0, 0)),
            pl.BlockSpec((d, tn), lambda i, j: (0, j)),
        ],
        out_specs=pl.BlockSpec((tm, tn), lambda i, j: (i, j)),
        out_shape=jax.ShapeDtypeStruct((n, cols), jnp.float32),
        scratch_shapes=[pltpu.VMEM((tm, d), jnp.bfloat16)],
        compiler_params=pltpu.CompilerParams(
            dimension_semantics=("parallel", "arbitrary"), vmem_limit_bytes=VMEM_LIMIT),
        name="inproj",
    )(x2, g, w)


def _chunk_cumsum(g, tb):
    pos = lax.broadcasted_iota(jnp.int32, (tb, 1), 0) % LA_CHUNK
    b = g
    shift = 1
    while shift < LA_CHUNK:
        b = b + jnp.where(pos >= shift, pltpu.roll(b, shift, axis=0), 0.0)
        shift *= 2
    return b


def _la_core(q, k, v, g, st_ref, tb):
    c, r = LA_CHUNK, LA_SUB
    n = tb // c
    nsub = c // r
    dk = q.shape[-1]
    b = _chunk_cumsum(g, tb)
    b3 = b.reshape(n, c, dk)
    q3 = q.reshape(n, c, dk)
    k3 = k.reshape(n, c, dk)
    v3 = _bf(v).reshape(n, c, dk)
    blast = b3[:, c - 1:c, :]
    q_state = _bf(q3 * jnp.exp(b3))
    k_state = _bf(k3 * jnp.exp(blast - b3))

    b4 = b.reshape(n * nsub, r, dk)
    ref = b4[:, 0:1, :]
    q_loc = _bf(q.reshape(n * nsub, r, dk) * jnp.exp(b4 - ref))
    s_idx = lax.broadcasted_iota(jnp.int32, (1, nsub, c, 1), 2)
    i_idx = lax.broadcasted_iota(jnp.int32, (1, nsub, c, 1), 1)
    e = jnp.where(s_idx < r * (i_idx + 1), ref.reshape(n, nsub, 1, dk) - b3[:, None, :, :], 0.0)
    k_loc = _bf(k3[:, None, :, :] * jnp.exp(e)).reshape(n * nsub, c, dk)
    a = jnp.einsum("utd,usd->uts", q_loc, k_loc, preferred_element_type=jnp.float32)
    a = a.reshape(n, c, c)
    t_i = lax.broadcasted_iota(jnp.int32, (1, c, c), 1)
    s_i = lax.broadcasted_iota(jnp.int32, (1, c, c), 2)
    a = jnp.where(s_i <= t_i, a, 0.0)
    o_intra = jnp.einsum("nts,nsv->ntv", _bf(a), v3, preferred_element_type=jnp.float32)

    st = st_ref[...]
    outs = []
    for ci in range(n):
        outs.append(_dot_nt(q_state[ci], _bf(st)) + o_intra[ci])
        st = st * jnp.exp(blast[ci]) + _dot_tn(v3[ci], k_state[ci])
    st_ref[...] = st
    return jnp.concatenate(outs, axis=0)


def _la_finish(o, og, gain, y_ref):
    o = o * lax.rsqrt(jnp.mean(o * o, axis=-1, keepdims=True) + EPS)
    y_ref[0] = _bf(o * gain * (og * _sigmoid(og)))


def _hgrn2_kernel(q_ref, f_ref, i_ref, og_ref, par_ref, y_ref, st_ref, *, tb):
    @pl.when(pl.program_id(2) == 0)
    def _():
        st_ref[...] = jnp.zeros_like(st_ref)

    log_lb = par_ref[0, 0:1, :]
    log1m_lb = par_ref[0, 1:2, :]
    one_m_lb = par_ref[0, 2:3, :]
    gain = par_ref[0, 3:4, :]
    hq = q_ref[0]
    hf = f_ref[0]
    q = hq * _sigmoid(hq)
    cterm = log1m_lb + _log_sigmoid(hf)
    log_f = jnp.maximum(log_lb, cterm) + jnp.log1p(jnp.exp(-jnp.abs(log_lb - cterm)))
    k = one_m_lb * _sigmoid(-hf)
    o = _la_core(q, k, i_ref[0], log_f, st_ref, tb)
    _la_finish(o, og_ref[0], gain, y_ref)


def _gla_kernel(q_ref, k_ref, v_ref, og_ref, lr_ref, w2_ref, par_ref, y_ref, st_ref, *, tb):
    @pl.when(pl.program_id(2) == 0)
    def _():
        st_ref[...] = jnp.zeros_like(st_ref)

    bias = par_ref[0, 0:1, :]
    gain = par_ref[0, 1:2, :]
    zz = _dot(_bf(lr_ref[0]), w2_ref[0]) + bias
    log_a = _log_sigmoid(zz) * (1.0 / GLA_TAU)
    q = q_ref[0] * (GLA_DK ** -0.5)
    o = _la_core(q, k_ref[0], v_ref[0], log_a, st_ref, tb)
    _la_finish(o, og_ref[0], gain, y_ref)


def _zspec(tb, col0):
    base = col0 // LANES
    return pl.BlockSpec((1, tb, LANES), lambda b, h, t: (b, t, base + h))


def _la_params():
    return pltpu.CompilerParams(
        dimension_semantics=("parallel", "parallel", "arbitrary"), vmem_limit_bytes=VMEM_LIMIT)


def _hgrn2(z3, par, tb=512):
    bsz, t, _ = z3.shape
    return pl.pallas_call(
        functools.partial(_hgrn2_kernel, tb=tb),
        grid=(bsz, HG_HEADS, t // tb),
        in_specs=[_zspec(tb, C_HQ), _zspec(tb, C_HF), _zspec(tb, C_HI), _zspec(tb, C_HOG),
                  pl.BlockSpec((1, 8, LANES), lambda b, h, t: (h, 0, 0))],
        out_specs=pl.BlockSpec((1, tb, LANES), lambda b, h, t: (b, t, h)),
        out_shape=jax.ShapeDtypeStruct((bsz, t, HG_DIM), jnp.bfloat16),
        scratch_shapes=[pltpu.VMEM((LANES, LANES), jnp.float32)],
        compiler_params=_la_params(),
        name="hgrn2",
    )(z3, z3, z3, z3, par)


def _gla(z3, w2p, par, tb=512):
    bsz, t, _ = z3.shape
    lr_blk = C_GLR // LANES
    return pl.pallas_call(
        functools.partial(_gla_kernel, tb=tb),
        grid=(bsz, GLA_HEADS, t // tb),
        in_specs=[_zspec(tb, C_GQ), _zspec(tb, C_GK), _zspec(tb, C_GV), _zspec(tb, C_GOG),
                  pl.BlockSpec((1, tb, LANES), lambda b, h, t: (b, t, lr_blk)),
                  pl.BlockSpec((1, LANES, LANES), lambda b, h, t: (h, 0, 0)),
                  pl.BlockSpec((1, 8, LANES), lambda b, h, t: (h, 0, 0))],
        out_specs=pl.BlockSpec((1, tb, LANES), lambda b, h, t: (b, t, h)),
        out_shape=jax.ShapeDtypeStruct((bsz, t, GLA_VDIM), jnp.bfloat16),
        scratch_shapes=[pltpu.VMEM((LANES, LANES), jnp.float32)],
        compiler_params=_la_params(),
        name="gla",
    )(z3, z3, z3, z3, z3, w2p, par)


def _rope(x, cos_t, sin_a, sin_b):
    half = ROT_DIM // 2
    slabs = []
    for i in range(x.shape[-1] // LANES):
        xs = x[:, i * LANES:(i + 1) * LANES]
        slabs.append(xs * cos_t + pltpu.roll(xs, LANES - half, axis=1) * sin_a
                     + pltpu.roll(xs, half, axis=1) * sin_b)
    return slabs[0] if len(slabs) == 1 else jnp.concatenate(slabs, axis=1)


def _per_group(x, filler):
    swapped = pltpu.roll(x, NSA_DH, axis=1)
    if filler is None:
        return _bf(x), _bf(swapped)
    low = lax.broadcasted_iota(jnp.int32, x.shape, 1) < NSA_DH
    return _bf(jnp.where(low, x, filler)), _bf(jnp.where(low, swapped, filler))


def _nsa_prep_kernel(q_ref, kc_ref, ks_ref, vs_ref, kw_ref, vw_ref, cos_ref, sa_ref, sb_ref,
                     qo_ref, kco_ref, ks0_ref, ks1_ref, vs0_ref, vs1_ref, kw0_ref, kw1_ref,
                     vw0_ref, vw1_ref, *, tt):
    ct, sa, sb = cos_ref[...], sa_ref[...], sb_ref[...]
    qo_ref[0] = _bf(_rope(q_ref[0], ct, sa, sb) * (NSA_DH ** -0.5 * LOG2E))
    kco_ref[0] = _rope(kc_ref[0], ct, sa, sb)
    tok = pl.program_id(1) * tt + lax.broadcasted_iota(jnp.int32, (tt, LANES), 0)
    lane = lax.broadcasted_iota(jnp.int32, (tt, LANES), 1)
    onehot = jnp.where(lane - NSA_DH == tok // SLC_BLK, 1.0, 0.0)
    ks0_ref[0], ks1_ref[0] = _per_group(_rope(ks_ref[0], ct, sa, sb), onehot)
    kw0_ref[0], kw1_ref[0] = _per_group(_rope(kw_ref[0], ct, sa, sb), None)
    vs0_ref[0], vs1_ref[0] = _per_group(vs_ref[0], 1.0)
    vw0_ref[0], vw1_ref[0] = _per_group(vw_ref[0], 1.0)


def _nsa_prep(z3, cos_t, sin_a, sin_b, tt=512):
    bsz, t, _ = z3.shape
    qw = NSA_HEADS * LANES

    def zs(col0, width):
        blk = col0 // width
        return pl.BlockSpec((1, tt, width), lambda b, i: (b, i, blk))

    def os_(width):
        return pl.BlockSpec((1, tt, width), lambda b, i: (b, i, 0))

    tab = pl.BlockSpec((tt, LANES), lambda b, i: (i, 0))
    kv_bf = jax.ShapeDtypeStruct((bsz, t, LANES), jnp.bfloat16)
    return pl.pallas_call(
        functools.partial(_nsa_prep_kernel, tt=tt),
        grid=(bsz, t // tt),
        in_specs=[zs(C_NQ, qw), zs(C_NKC, LANES), zs(C_NKS, LANES), zs(C_NVS, LANES),
                  zs(C_NKW, LANES), zs(C_NVW, LANES), tab, tab, tab],
        out_specs=[os_(qw), os_(LANES)] + [os_(LANES)] * 8,
        out_shape=[jax.ShapeDtypeStruct((bsz, t, qw), jnp.bfloat16),
                   jax.ShapeDtypeStruct((bsz, t, LANES), jnp.float32)] + [kv_bf] * 8,
        compiler_params=pltpu.CompilerParams(
            dimension_semantics=("parallel", "parallel"), vmem_limit_bytes=VMEM_LIMIT),
        name="nsa_prep",
    )(z3, z3, z3, z3, z3, z3, cos_t, sin_a, sin_b)


def _compress_one(src_ref, pos_ref, w1_ref, w2_ref, out0_ref, out1_ref, ncmp):
    half = CMP_BLK // 2
    u = jnp.zeros((ncmp, 2 * CMP_HID), jnp.float32)
    v = jnp.zeros((ncmp, 2 * CMP_HID), jnp.float32)
    for j in range(half):
        rows = src_ref[0, pl.ds(j, ncmp, stride=CMP_STRIDE), :]
        u = u + _dot(_bf(rows + pos_ref[j:j + 1, :]), w1_ref[j])
        v = v + _dot(_bf(rows + pos_ref[half + j:half + j + 1, :]), w1_ref[half + j])
    hid = u + pltpu.roll(v, ncmp - 1, axis=0)
    out0_ref[0], out1_ref[0] = _per_group(_dot(_bf(jax.nn.gelu(hid)), w2_ref[...]), None)


def _compress_kernel(k_ref, v_ref, pk_ref, pv_ref, w1k_ref, w2k_ref, w1v_ref, w2v_ref,
                     kc0_ref, kc1_ref, vc0_ref, vc1_ref, *, ncmp):
    _compress_one(k_ref, pk_ref, w1k_ref, w2k_ref, kc0_ref, kc1_ref, ncmp)
    _compress_one(v_ref, pv_ref, w1v_ref, w2v_ref, vc0_ref, vc1_ref, ncmp)


def _compress(kc_rot, z3, pos_k, pos_v, w1k, w2k, w1v, w2v):
    bsz, t, _ = kc_rot.shape
    ncmp = t // CMP_STRIDE
    vblk = C_NVC // LANES
    full = lambda shape: pl.BlockSpec(shape, lambda b: (0,) * len(shape))
    out = jax.ShapeDtypeStruct((bsz, ncmp, LANES), jnp.bfloat16)
    return pl.pallas_call(
        functools.partial(_compress_kernel, ncmp=ncmp),
        grid=(bsz,),
        in_specs=[pl.BlockSpec((1, t, LANES), lambda b: (b, 0, 0)),
                  pl.BlockSpec((1, t, LANES), lambda b: (b, 0, vblk)),
                  full(pos_k.shape), full(pos_v.shape), full(w1k.shape), full(w2k.shape),
                  full(w1v.shape), full(w2v.shape)],
        out_specs=[pl.BlockSpec((1, ncmp, LANES), lambda b: (b, 0, 0))] * 4,
        out_shape=[out] * 4,
        compiler_params=pltpu.CompilerParams(
            dimension_semantics=("parallel",), vmem_limit_bytes=VMEM_LIMIT),
        name="nsa_compress",
    )(kc_rot, z3, pos_k, pos_v, w1k, w2k, w1v, w2v)


NEG = -1e30
SEL_KB = 256


def _select_blocks(imp, t_col, nslc):
    qb = imp.shape[0]
    blk = lax.broadcasted_iota(jnp.int32, (qb, LANES), 1) - NSA_DH
    cur = t_col // SLC_BLK
    valid = (blk >= 0) & (blk <= cur)
    forced = (blk == 0) | (blk == cur) | (blk == cur - 1)
    score = jnp.where(valid & forced, 1e9, jnp.where(valid, imp, -1e9))
    sc_t = score.T
    sub = 8
    nslab = (LANES - NSA_DH) // sub
    slabs = [sc_t[NSA_DH + sub * k:NSA_DH + sub * (k + 1), :] for k in range(nslab)]
    jj = lax.broadcasted_iota(jnp.int32, (sub, qb), 0)
    ranks = [jnp.zeros((sub, qb), jnp.float32) for _ in range(nslab)]
    for i in range(nslc):
        row = sc_t[NSA_DH + i:NSA_DH + i + 1, :]
        for k in range(nslab):
            if sub * k > i:
                ahead = row >= slabs[k]
            elif sub * (k + 1) <= i:
                ahead = row > slabs[k]
            else:
                ahead = (row > slabs[k]) | ((row == slabs[k]) & (jj + sub * k > i))
            ranks[k] = ranks[k] + jnp.where(ahead, 1.0, 0.0)
    sel_t = jnp.concatenate(
        [jnp.zeros((NSA_DH, qb), jnp.float32)]
        + [jnp.where(r < float(N_SEL), 1.0, 0.0) for r in ranks], axis=0)
    return valid & (sel_t.T > 0.5)


def _nsa_attn_kernel(q_ref, kc0_ref, kc1_ref, vc0_ref, vc1_ref, ks0_ref, ks1_ref, vs0_ref, vs1_ref,
                     kw0_ref, kw1_ref, vw0_ref, vw1_ref, gate_ref, ovl_ref, y_ref,
                     qa_scr, m_scr, acc_scr, sa_scr, sb_scr, *, seq):
    c = pl.program_id(1)
    qb = Q_BLOCK
    rep = NSA_REP
    rows = rep * qb
    t0 = c * qb
    t_col = t0 + lax.broadcasted_iota(jnp.int32, (qb, 1), 0)
    gates = _sigmoid(gate_ref[0])
    ncmp = kc0_ref.shape[1]
    lane_low = lax.broadcasted_iota(jnp.int32, (qb, LANES), 1) < NSA_DH
    n_idx = lax.broadcasted_iota(jnp.int32, (1, ncmp), 1)
    bias_c = jnp.where((n_idx * CMP_STRIDE + (CMP_BLK - 1)) <= t_col, 0.0, NEG)
    has_cmp = jnp.where(t_col >= CMP_BLK - 1, 1.0, 0.0)
    w_start = pl.multiple_of(jnp.maximum(t0 - WIN, 0), qb)
    wk = WIN + qb
    kp = w_start + lax.broadcasted_iota(jnp.int32, (1, wk), 1)
    bias_w = jnp.where((kp <= t_col) & (kp > t_col - WIN), 0.0, NEG)
    n_full = t0 // SEL_KB

    def with_bias(s, bias):
        return (s.reshape(rep, qb, s.shape[-1]) + bias[None]).reshape(s.shape)

    groups = ((kc0_ref, vc0_ref, ks0_ref, vs0_ref, kw0_ref, vw0_ref),
              (kc1_ref, vc1_ref, ks1_ref, vs1_ref, kw1_ref, vw1_ref))
    o_cs, acc_ws = [], []
    for g, (kc_ref, vc_ref, ks_ref, vs_ref, kw_ref, vw_ref) in enumerate(groups):
        q_heads = [q_ref[0, :, (g * rep + r) * LANES:(g * rep + r + 1) * LANES] for r in range(rep)]
        qg = jnp.concatenate(q_heads, axis=0)

        s = with_bias(_dot_nt(qg, kc_ref[0]), bias_c)
        m = jnp.max(s, axis=-1, keepdims=True)
        p = (jnp.exp2(s - m).reshape(rep, qb, ncmp) * has_cmp[None]).reshape(rows, ncmp)
        l = jnp.sum(p, axis=-1, keepdims=True)
        p_c = p / jnp.where(l > 0.0, l, 1.0)
        o_cs.append(_dot(_bf(p_c), vc_ref[0]))
        p_sum = p_c[0:qb] + p_c[qb:2 * qb] + p_c[2 * qb:3 * qb] + p_c[3 * qb:4 * qb]
        imp = jnp.dot(p_sum, ovl_ref[...], preferred_element_type=jnp.float32,
                      precision=lax.Precision.HIGHEST)
        sel = _select_blocks(imp, t_col, seq // SLC_BLK)
        sel_bias = _bf(jnp.where(sel, 0.0, NEG))
        for r, qh in enumerate(q_heads):
            qa_scr[g, r * qb:(r + 1) * qb, :] = jnp.where(lane_low, qh, sel_bias)
        m_scr[g] = jnp.full((rows, LANES), NEG, jnp.float32)
        acc_scr[g] = jnp.zeros((rows, LANES), jnp.float32)

    def chunk_start(j):
        return pl.multiple_of(jnp.minimum(j * SEL_KB, seq - SEL_KB), SEL_KB)

    def produce(j, dst_scr):
        k0 = chunk_start(j)
        for g in range(NSA_GROUPS):
            dst_scr[g] = _dot_nt(qa_scr[g], groups[g][2][0, pl.ds(k0, SEL_KB), :])

    def consume(j, src_scr, causal):
        k0 = chunk_start(j)
        if causal:
            tok = j * SEL_KB + lax.broadcasted_iota(jnp.int32, (1, SEL_KB), 1)
            bias = jnp.where(tok <= t_col, 0.0, NEG)
        for g in range(NSA_GROUPS):
            vs_ref = groups[g][3]
            m = m_scr[g]
            s = with_bias(src_scr[g], bias) if causal else src_scr[g]
            m_new = jnp.maximum(m, jnp.max(s, axis=-1, keepdims=True))
            p = jnp.exp2(s - jnp.concatenate([m_new] * (SEL_KB // LANES), axis=1))
            pv = _dot(_bf(p), vs_ref[0, pl.ds(k0, SEL_KB), :])
            acc_scr[g] = jnp.exp2(m - m_new) * acc_scr[g] + pv
            m_scr[g] = m_new

    def loop_body(i, carry):
        produce(2 * i + 1, sb_scr)
        consume(2 * i, sa_scr, False)
        produce(2 * i + 2, sa_scr)
        consume(2 * i + 1, sb_scr, False)
        return carry

    n_pairs = n_full // 2
    produce(0, sa_scr)
    lax.fori_loop(0, n_pairs, loop_body, 0)
    produce(2 * n_pairs + 1, sb_scr)
    consume(2 * n_pairs, sa_scr, True)
    consume(2 * n_pairs + 1, sb_scr, True)
    carry = (None, acc_scr[0], None, acc_scr[1])

    for g in range(NSA_GROUPS):
        kw_ref, vw_ref = groups[g][4], groups[g][5]
        qg = jnp.concatenate(
            [q_ref[0, :, (g * rep + r) * LANES:(g * rep + r + 1) * LANES] for r in range(rep)], axis=0)
        s = with_bias(_dot_nt(qg, kw_ref[0, pl.ds(w_start, wk), :]), bias_w)
        m = jnp.max(s, axis=-1, keepdims=True)
        acc_ws.append(_dot(_bf(jnp.exp2(s - m)), vw_ref[0, pl.ds(w_start, wk), :]))

    def normalized(acc):
        return acc / pltpu.roll(acc, NSA_DH, axis=1)

    for g in range(NSA_GROUPS):
        o_c, o_s, o_w = o_cs[g], normalized(carry[2 * g + 1]), normalized(acc_ws[g])
        for r in range(rep):
            hd = g * rep + r
            sl = slice(r * qb, (r + 1) * qb)
            out = (o_c[sl] * gates[:, 3 * hd:3 * hd + 1] + o_s[sl] * gates[:, 3 * hd + 1:3 * hd + 2]
                   + o_w[sl] * gates[:, 3 * hd + 2:3 * hd + 3])
            y_ref[0, :, hd * LANES:(hd + 1) * LANES] = _bf(jnp.where(lane_low, out, 0.0))


def _nsa_attn(q_rot, kcs, kvs, z3, ovl):
    bsz, t, qw = q_rot.shape
    ncmp = kcs[0].shape[1]
    assert t % SEL_KB == 0 and t >= WIN + Q_BLOCK and t // SLC_BLK <= LANES - NSA_DH
    gblk = C_NGATE // LANES
    per_b = lambda rows: pl.BlockSpec((1, rows, LANES), lambda b, c: (b, 0, 0))
    return pl.pallas_call(
        functools.partial(_nsa_attn_kernel, seq=t),
        grid=(bsz, t // Q_BLOCK),
        in_specs=[pl.BlockSpec((1, Q_BLOCK, qw), lambda b, c: (b, c, 0))]
        + [per_b(ncmp)] * 4 + [per_b(t)] * 8
        + [pl.BlockSpec((1, Q_BLOCK, LANES), lambda b, c: (b, c, gblk)),
           pl.BlockSpec(ovl.shape, lambda b, c: (0, 0))],
        out_specs=pl.BlockSpec((1, Q_BLOCK, qw), lambda b, c: (b, c, 0)),
        out_shape=jax.ShapeDtypeStruct((bsz, t, qw), jnp.bfloat16),
        scratch_shapes=[pltpu.VMEM((NSA_GROUPS, NSA_REP * Q_BLOCK, LANES), jnp.bfloat16),
                        pltpu.VMEM((NSA_GROUPS, NSA_REP * Q_BLOCK, LANES), jnp.float32),
                        pltpu.VMEM((NSA_GROUPS, NSA_REP * Q_BLOCK, LANES), jnp.float32),
                        pltpu.VMEM((NSA_GROUPS, NSA_REP * Q_BLOCK, SEL_KB), jnp.float32),
                        pltpu.VMEM((NSA_GROUPS, NSA_REP * Q_BLOCK, SEL_KB), jnp.float32)],
        compiler_params=pltpu.CompilerParams(
            dimension_semantics=("parallel", "arbitrary"), vmem_limit_bytes=VMEM_LIMIT),
        name="nsa_attn",
    )(q_rot, *kcs, *kvs, z3, ovl)


def _merge_kernel(x_ref, yh_ref, yg_ref, yn_ref, mg_ref, wh_ref, wg_ref, wn_ref, wo_ref, o_ref):
    d = D_MODEL
    m = _sigmoid(mg_ref[:, 0:d]) * _dot(yh_ref[...], wh_ref[...])
    m = m + _sigmoid(mg_ref[:, d:2 * d]) * _dot(yg_ref[...], wg_ref[...])
    m = m + _sigmoid(mg_ref[:, 2 * d:3 * d]) * _dot(yn_ref[...], wn_ref[...])
    o_ref[...] = x_ref[...] + _dot(_bf(m), wo_ref[...])


def _merge(x2, yh, yg, yn, z2, wh, wg, wn, wo, tm=512):
    n, d = x2.shape
    mgblk = C_MG // (3 * d)
    row = lambda w: pl.BlockSpec((tm, w), lambda i: (i, 0))
    full = lambda a: pl.BlockSpec(a.shape, lambda i: (0, 0))
    return pl.pallas_call(
        _merge_kernel,
        grid=(n // tm,),
        in_specs=[row(d), row(yh.shape[1]), row(yg.shape[1]), row(yn.shape[1]),
                  pl.BlockSpec((tm, 3 * d), lambda i: (i, mgblk)),
                  full(wh), full(wg), full(wn), full(wo)],
        out_specs=row(d),
        out_shape=jax.ShapeDtypeStruct((n, d), jnp.float32),
        compiler_params=pltpu.CompilerParams(
            dimension_semantics=("parallel",), vmem_limit_bytes=VMEM_LIMIT),
        name="merge",
    )(x2, yh, yg, yn, z2, wh, wg, wn, wo)


def _ffn_kernel(x_ref, g_ref, wg_ref, wu_ref, wd_ref, fg_ref, o_ref, h_scr, acc_scr, *, final_norm):
    j = pl.program_id(1)

    @pl.when(j == 0)
    def _():
        x = x_ref[...]
        y = x * lax.rsqrt(jnp.mean(x * x, axis=-1, keepdims=True) + EPS)
        h_scr[...] = _bf(y * g_ref[...])
        acc_scr[...] = jnp.zeros_like(acc_scr)

    h = h_scr[...]
    a = _dot(h, wg_ref[...])
    u = _dot(h, wu_ref[...])
    acc_scr[...] += _dot(_bf(a * _sigmoid(a) * u), wd_ref[...])

    @pl.when(j == pl.num_programs(1) - 1)
    def _():
        out = x_ref[...] + acc_scr[...]
        if final_norm:
            out = out * lax.rsqrt(jnp.mean(out * out, axis=-1, keepdims=True) + EPS) * fg_ref[...]
        o_ref[...] = out


def _ffn(x2, g, wg, wu, wd, fg, final_norm, tm=1024, tf=256):
    n, d = x2.shape
    ff = wg.shape[1]
    return pl.pallas_call(
        functools.partial(_ffn_kernel, final_norm=final_norm),
        grid=(n // tm, ff // tf),
        in_specs=[pl.BlockSpec((tm, d), lambda i, j: (i, 0)),
                  pl.BlockSpec((1, d), lambda i, j: (0, 0)),
                  pl.BlockSpec((d, tf), lambda i, j: (0, j)),
                  pl.BlockSpec((d, tf), lambda i, j: (0, j)),
                  pl.BlockSpec((tf, d), lambda i, j: (j, 0)),
                  pl.BlockSpec((1, d), lambda i, j: (0, 0))],
        out_specs=pl.BlockSpec((tm, d), lambda i, j: (i, 0)),
        out_shape=jax.ShapeDtypeStruct((n, d), jnp.float32),
        scratch_shapes=[pltpu.VMEM((tm, d), jnp.bfloat16), pltpu.VMEM((tm, d), jnp.float32)],
        compiler_params=pltpu.CompilerParams(
            dimension_semantics=("parallel", "arbitrary"), vmem_limit_bytes=VMEM_LIMIT),
        name="ffn",
    )(x2, g, wg, wu, wd, fg)


def _pad_heads(w, heads, width):
    lead = w.shape[:-1]
    w = w.reshape(lead + (heads, width))
    w = jnp.pad(w, [(0, 0)] * len(lead) + [(0, 0), (0, LANES - width)])
    return w.reshape(lead + (heads * LANES,))


def _layout_w_in(w):
    d = w.shape[0]
    o = 0
    parts = {}
    for name, n in (("hq", 512), ("hf", 512), ("hi", 512), ("hog", 512), ("gq", 256), ("gk", 256),
                    ("gv", 512), ("glr", 16), ("gog", 512), ("nq", 512), ("nkc", 128), ("nvc", 128),
                    ("nks", 128), ("nvs", 128), ("nkw", 128), ("nvw", 128), ("ngate", 24), ("mg", 3072)):
        parts[name] = w[:, o:o + n]
        o += n
    nq = _pad_heads(parts["nq"], NSA_HEADS, NSA_DH)
    padc = lambda a: jnp.pad(a, ((0, 0), (0, LANES - a.shape[1])))
    cols = [parts["hq"], parts["hf"], parts["hi"], parts["hog"],
            _pad_heads(parts["gq"], GLA_HEADS, GLA_DK), _pad_heads(parts["gk"], GLA_HEADS, GLA_DK),
            parts["gv"], parts["gog"], nq,
            parts["nkc"], parts["nvc"], parts["nks"], parts["nvs"], parts["nkw"], parts["nvw"],
            padc(parts["ngate"]), padc(parts["glr"]), parts["mg"]]
    out = jnp.concatenate(cols, axis=1)
    assert out.shape[1] == Z_COLS
    return _bf(out)


def _blockdiag2(w):
    z = jnp.zeros_like(w)
    top = jnp.concatenate([w, z], axis=-1)
    bot = jnp.concatenate([z, w], axis=-1)
    return jnp.concatenate([top, bot], axis=-2)


def _rope_tables(t):
    pos = jnp.arange(t, dtype=jnp.float32)
    inv_freq = ROPE_THETA ** (-jnp.arange(0, ROT_DIM, 2, dtype=jnp.float32) / ROT_DIM)
    ang = pos[:, None] * inv_freq[None, :]
    cos, sin = jnp.cos(ang), jnp.sin(ang)
    half = ROT_DIM // 2
    lane = jnp.arange(LANES) % NSA_DH
    first = lane < half
    second = (lane >= half) & (lane < ROT_DIM)
    idx = jnp.where(first, lane, jnp.where(second, lane - half, 0))
    cos_t = jnp.where(first | second, cos[:, idx], 1.0)
    sin_a = jnp.where(first, -sin[:, idx], 0.0)
    sin_b = jnp.where(second, sin[:, idx], 0.0)
    return cos_t, sin_a, sin_b


def kernel(x, norm1_g, w_in, hg_lb, hg_norm_g, gla_w2, gla_b, gla_norm_g, cmp_pos_k, cmp_pos_v,
           cmp_w1_k, cmp_w2_k, cmp_w1_v, cmp_w2_v, w_br_hg, w_br_gla, w_br_nsa, w_out,
           norm2_g, w_ffn_gate, w_ffn_up, w_ffn_down, final_norm_g):
    bsz, t, d = x.shape
    depth = w_in.shape[0]
    n = bsz * t
    f32 = jnp.float32

    cos_t, sin_a, sin_b = _rope_tables(t)
    lbs = jnp.cumsum(jax.nn.softmax(hg_lb.astype(f32), axis=0), axis=0)
    lbs = lbs - lbs[0]

    nslc = t // SLC_BLK
    ncmp_pad = t // CMP_STRIDE
    cmp_start = jnp.arange(ncmp_pad) * CMP_STRIDE
    blk = jnp.arange(LANES) - NSA_DH
    ovl = ((cmp_start[:, None] < (blk[None, :] + 1) * SLC_BLK)
           & (cmp_start[:, None] + CMP_BLK - 1 >= blk[None, :] * SLC_BLK)
           & (blk[None, :] >= 0) & (blk[None, :] < nslc)
           & (jnp.arange(ncmp_pad)[:, None] < ncmp_pad - 1)).astype(f32)

    x2 = x.reshape(n, d)
    for l in range(depth):
        z2 = _inproj(x2, norm1_g[l][None, :], _layout_w_in(w_in[l]))
        z3 = z2.reshape(bsz, t, Z_COLS)

        lb = lbs[l].reshape(HG_HEADS, LANES)
        zeros = jnp.zeros_like(lb)
        hg_par = jnp.stack([jnp.log(lb), jnp.log1p(-lb), 1.0 - lb,
                            hg_norm_g[l].reshape(HG_HEADS, LANES), zeros, zeros, zeros, zeros], axis=1)
        y_hg = _hgrn2(z3, hg_par)

        w2p = jnp.pad(_pad_heads(gla_w2[l], GLA_HEADS, GLA_DK), ((0, LANES - GLA_RANK), (0, 0)))
        w2p = _bf(w2p.reshape(LANES, GLA_HEADS, LANES).transpose(1, 0, 2))
        gb = _pad_heads(gla_b[l], GLA_HEADS, GLA_DK).reshape(GLA_HEADS, LANES)
        gla_par = jnp.stack([gb, gla_norm_g[l].reshape(GLA_HEADS, LANES),
                             zeros, zeros, zeros, zeros, zeros, zeros], axis=1)
        y_gla = _gla(z3, w2p, gla_par)

        q_rot, kc_rot, *kvs = _nsa_prep(z3, cos_t, sin_a, sin_b)
        tile2 = lambda p: jnp.concatenate([p, p], axis=1)
        w1k = _bf(_blockdiag2(cmp_w1_k[l].reshape(CMP_BLK, NSA_DH, CMP_HID)))
        w1v = _bf(_blockdiag2(cmp_w1_v[l].reshape(CMP_BLK, NSA_DH, CMP_HID)))
        kcs = _compress(kc_rot, z3, tile2(cmp_pos_k[l]), tile2(cmp_pos_v[l]),
                        w1k, _bf(_blockdiag2(cmp_w2_k[l])), w1v, _bf(_blockdiag2(cmp_w2_v[l])))
        y_nsa = _nsa_attn(q_rot, kcs, kvs, z3, ovl)

        wn = _pad_heads(w_br_nsa[l].T, NSA_HEADS, NSA_DH).T
        x2 = _merge(x2, y_hg.reshape(n, HG_DIM), y_gla.reshape(n, GLA_VDIM),
                    y_nsa.reshape(n, NSA_HEADS * LANES), z2,
                    _bf(w_br_hg[l]), _bf(w_br_gla[l]), _bf(wn), _bf(w_out[l]))

        x2 = _ffn(x2, norm2_g[l][None, :], _bf(w_ffn_gate[l]), _bf(w_ffn_up[l]), _bf(w_ffn_down[l]),
                  final_norm_g[None, :], final_norm=(l == depth - 1))
    return x2.reshape(bsz, t, d)
```

```python
import functools

import jax
import jax.numpy as jnp
from jax import lax
from jax.experimental import pallas as pl
from jax.experimental.pallas import tpu as pltpu

EPS = 1e-6
D_MODEL = 1024
LANES = 128

HG_HEADS = 4
HG_DIM = 512
GLA_HEADS = 4
GLA_DK = 64
GLA_KDIM = 256
GLA_VDIM = 512
GLA_RANK = 16
GLA_TAU = 16.0
LA_CHUNK = 64
LA_SUB = 16

NSA_HEADS = 8
NSA_GROUPS = 2
NSA_REP = 4
NSA_DH = 64
CMP_STRIDE = 16
CMP_BLK = 32
CMP_HID = 128
SLC_BLK = 64
N_SEL = 16
WIN = 512
Q_BLOCK = 128
ROPE_THETA = 500000.0
ROT_DIM = 16
FF_DIM = 2816

C_HQ, C_HF, C_HI, C_HOG = 0, 512, 1024, 1536
C_GQ, C_GK, C_GV, C_GOG = 2048, 2560, 3072, 3584
C_NQ = 4096
C_NKC, C_NVC, C_NKS, C_NVS, C_NKW, C_NVW = 5120, 5248, 5376, 5504, 5632, 5760
C_NGATE, C_GLR = 5888, 6016
C_MG = 6144
Z_COLS = 9216

VMEM_LIMIT = 56 * 1024 * 1024
LOG2E = 1.4426950408889634

_NT = (((1,), (1,)), ((), ()))
_TN = (((0,), (0,)), ((), ()))


def _bf(x):
    return x.astype(jnp.bfloat16)


def _dot(a, b):
    return jnp.dot(a, b, preferred_element_type=jnp.float32)


def _dot_nt(a, b):
    return lax.dot_general(a, b, _NT, preferred_element_type=jnp.float32)


def _dot_tn(a, b):
    return lax.dot_general(a, b, _TN, preferred_element_type=jnp.float32)


def _f32(x):
    return x.astype(jnp.float32)


def _sigmoid(x):
    return 1.0 / (1.0 + jnp.exp(-x))


def _log_sigmoid(x):
    return jnp.minimum(x, 0.0) - jnp.log1p(jnp.exp(-jnp.abs(x)))


def _inproj_kernel(x_ref, g_ref, w_ref, z_ref, h_scr):
    @pl.when(pl.program_id(1) == 0)
    def _():
        x = x_ref[...]
        y = x * lax.rsqrt(jnp.mean(x * x, axis=-1, keepdims=True) + EPS)
        h_scr[...] = _bf(y * g_ref[...])

    z_ref[...] = _dot(h_scr[...], w_ref[...]).astype(z_ref.dtype)


def _inproj(x2, g, w, tm=1024, tn=1024):
    n, d = x2.shape
    cols = w.shape[1]
    return pl.pallas_call(
        _inproj_kernel,
        grid=(n // tm, cols // tn),
        in_specs=[
            pl.BlockSpec((tm, d), lambda i, j: (i, 0)),
            pl.BlockSpec((1, d), lambda i, j: (0, 0)),
            pl.BlockSpec((d, tn), lambda i, j: (0, j)),
        ],
        out_specs=pl.BlockSpec((tm, tn), lambda i, j: (i, j)),
        out_shape=jax.ShapeDtypeStruct((n, cols), jnp.bfloat16),
        scratch_shapes=[pltpu.VMEM((tm, d), jnp.bfloat16)],
        compiler_params=pltpu.CompilerParams(
            dimension_semantics=("parallel", "arbitrary"), vmem_limit_bytes=VMEM_LIMIT),
        name="inproj",
    )(x2, g, w)


def _chunk_cumsum(g, tb):
    pos = lax.broadcasted_iota(jnp.int32, (tb, 1), 0) % LA_CHUNK
    b = g
    shift = 1
    while shift < LA_CHUNK:
        b = b + jnp.where(pos >= shift, pltpu.roll(b, shift, axis=0), 0.0)
        shift *= 2
    return b


def _la_core(q, k, v, g, st_ref, tb):
    c, r = LA_CHUNK, LA_SUB
    n = tb // c
    nsub = c // r
    dk = q.shape[-1]
    b = _chunk_cumsum(g, tb)
    b3 = b.reshape(n, c, dk)
    q3 = q.reshape(n, c, dk)
    k3 = k.reshape(n, c, dk)
    v3 = _bf(v).reshape(n, c, dk)
    blast = b3[:, c - 1:c, :]
    q_state = _bf(q3 * jnp.exp(b3))
    k_state = _bf(k3 * jnp.exp(blast - b3))

    b4 = b.reshape(n * nsub, r, dk)
    ref = b4[:, 0:1, :]
    q_loc = _bf(q.reshape(n * nsub, r, dk) * jnp.exp(b4 - ref))
    s_idx = lax.broadcasted_iota(jnp.int32, (1, nsub, c, 1), 2)
    i_idx = lax.broadcasted_iota(jnp.int32, (1, nsub, c, 1), 1)
    e = jnp.where(s_idx < r * (i_idx + 1), ref.reshape(n, nsub, 1, dk) - b3[:, None, :, :], 0.0)
    k_loc = _bf(k3[:, None, :, :] * jnp.exp(e)).reshape(n * nsub, c, dk)
    a = jnp.einsum("utd,usd->uts", q_loc, k_loc, preferred_element_type=jnp.float32)
    a = a.reshape(n, c, c)
    t_i = lax.broadcasted_iota(jnp.int32, (1, c, c), 1)
    s_i = lax.broadcasted_iota(jnp.int32, (1, c, c), 2)
    a = jnp.where(s_i <= t_i, a, 0.0)
    o_intra = jnp.einsum("nts,nsv->ntv", _bf(a), v3, preferred_element_type=jnp.float32)

    st = st_ref[...]
    outs = []
    for ci in range(n):
        outs.append(_dot_nt(q_state[ci], _bf(st)) + o_intra[ci])
        st = st * jnp.exp(blast[ci]) + _dot_tn(v3[ci], k_state[ci])
    st_ref[...] = st
    return jnp.concatenate(outs, axis=0)


def _la_finish(o, og, gain, y_ref):
    o = o * lax.rsqrt(jnp.mean(o * o, axis=-1, keepdims=True) + EPS)
    y_ref[0] = _bf(o * gain * (og * _sigmoid(og)))


def _hgrn2_kernel(q_ref, f_ref, i_ref, og_ref, par_ref, y_ref, st_ref, *, tb):
    @pl.when(pl.program_id(2) == 0)
    def _():
        st_ref[...] = jnp.zeros_like(st_ref)

    log_lb = par_ref[0, 0:1, :]
    log1m_lb = par_ref[0, 1:2, :]
    one_m_lb = par_ref[0, 2:3, :]
    gain = par_ref[0, 3:4, :]
    hq = _f32(q_ref[0])
    hf = _f32(f_ref[0])
    q = hq * _sigmoid(hq)
    cterm = log1m_lb + _log_sigmoid(hf)
    log_f = jnp.maximum(log_lb, cterm) + jnp.log1p(jnp.exp(-jnp.abs(log_lb - cterm)))
    k = one_m_lb * _sigmoid(-hf)
    o = _la_core(q, k, _f32(i_ref[0]), log_f, st_ref, tb)
    _la_finish(o, _f32(og_ref[0]), gain, y_ref)


def _gla_kernel(q_ref, k_ref, v_ref, og_ref, lr_ref, w2_ref, par_ref, y_ref, st_ref, *, tb):
    @pl.when(pl.program_id(2) == 0)
    def _():
        st_ref[...] = jnp.zeros_like(st_ref)

    bias = par_ref[0, 0:1, :]
    gain = par_ref[0, 1:2, :]
    zz = _dot(_bf(lr_ref[0]), w2_ref[0]) + bias
    log_a = _log_sigmoid(zz) * (1.0 / GLA_TAU)
    q = _f32(q_ref[0]) * (GLA_DK ** -0.5)
    o = _la_core(q, _f32(k_ref[0]), _f32(v_ref[0]), log_a, st_ref, tb)
    _la_finish(o, _f32(og_ref[0]), gain, y_ref)


def _zspec(tb, col0):
    base = col0 // LANES
    return pl.BlockSpec((1, tb, LANES), lambda b, h, t: (b, t, base + h))


def _la_params():
    return pltpu.CompilerParams(
        dimension_semantics=("parallel", "parallel", "arbitrary"), vmem_limit_bytes=VMEM_LIMIT)


def _hgrn2(z3, par, tb=512):
    bsz, t, _ = z3.shape
    return pl.pallas_call(
        functools.partial(_hgrn2_kernel, tb=tb),
        grid=(bsz, HG_HEADS, t // tb),
        in_specs=[_zspec(tb, C_HQ), _zspec(tb, C_HF), _zspec(tb, C_HI), _zspec(tb, C_HOG),
                  pl.BlockSpec((1, 8, LANES), lambda b, h, t: (h, 0, 0))],
        out_specs=pl.BlockSpec((1, tb, LANES), lambda b, h, t: (b, t, h)),
        out_shape=jax.ShapeDtypeStruct((bsz, t, HG_DIM), jnp.bfloat16),
        scratch_shapes=[pltpu.VMEM((LANES, LANES), jnp.float32)],
        compiler_params=_la_params(),
        name="hgrn2",
    )(z3, z3, z3, z3, par)


def _gla(z3, w2p, par, tb=512):
    bsz, t, _ = z3.shape
    lr_blk = C_GLR // LANES
    return pl.pallas_call(
        functools.partial(_gla_kernel, tb=tb),
        grid=(bsz, GLA_HEADS, t // tb),
        in_specs=[_zspec(tb, C_GQ), _zspec(tb, C_GK), _zspec(tb, C_GV), _zspec(tb, C_GOG),
                  pl.BlockSpec((1, tb, LANES), lambda b, h, t: (b, t, lr_blk)),
                  pl.BlockSpec((1, LANES, LANES), lambda b, h, t: (h, 0, 0)),
                  pl.BlockSpec((1, 8, LANES), lambda b, h, t: (h, 0, 0))],
        out_specs=pl.BlockSpec((1, tb, LANES), lambda b, h, t: (b, t, h)),
        out_shape=jax.ShapeDtypeStruct((bsz, t, GLA_VDIM), jnp.bfloat16),
        scratch_shapes=[pltpu.VMEM((LANES, LANES), jnp.float32)],
        compiler_params=_la_params(),
        name="gla",
    )(z3, z3, z3, z3, z3, w2p, par)


def _rope(x, cos_t, sin_a, sin_b):
    half = ROT_DIM // 2
    slabs = []
    for i in range(x.shape[-1] // LANES):
        xs = x[:, i * LANES:(i + 1) * LANES]
        slabs.append(xs * cos_t + pltpu.roll(xs, LANES - half, axis=1) * sin_a
                     + pltpu.roll(xs, half, axis=1) * sin_b)
    return slabs[0] if len(slabs) == 1 else jnp.concatenate(slabs, axis=1)


def _per_group(x, filler):
    swapped = pltpu.roll(x, NSA_DH, axis=1)
    if filler is None:
        return _bf(x), _bf(swapped)
    low = lax.broadcasted_iota(jnp.int32, x.shape, 1) < NSA_DH
    return _bf(jnp.where(low, x, filler)), _bf(jnp.where(low, swapped, filler))


def _nsa_prep_kernel(q_ref, kc_ref, vc_ref, ks_ref, vs_ref, kw_ref, vw_ref, cos_ref, sa_ref, sb_ref,
                     qo_ref, kco_ref, vco_ref, ks0_ref, ks1_ref, vs0_ref, vs1_ref, kw0_ref, kw1_ref,
                     vw0_ref, vw1_ref, *, tt):
    ct, sa, sb = cos_ref[...], sa_ref[...], sb_ref[...]
    qo_ref[0] = _bf(_rope(_f32(q_ref[0]), ct, sa, sb) * (NSA_DH ** -0.5 * LOG2E))
    kco_ref[0] = _rope(_f32(kc_ref[0]), ct, sa, sb)
    vco_ref[0] = _f32(vc_ref[0])
    tok = pl.program_id(1) * tt + lax.broadcasted_iota(jnp.int32, (tt, LANES), 0)
    lane = lax.broadcasted_iota(jnp.int32, (tt, LANES), 1)
    onehot = jnp.where(lane - NSA_DH == tok // SLC_BLK, 1.0, 0.0)
    ks0_ref[0], ks1_ref[0] = _per_group(_rope(_f32(ks_ref[0]), ct, sa, sb), onehot)
    kw0_ref[0], kw1_ref[0] = _per_group(_rope(_f32(kw_ref[0]), ct, sa, sb), None)
    vs0_ref[0], vs1_ref[0] = _per_group(_f32(vs_ref[0]), 1.0)
    vw0_ref[0], vw1_ref[0] = _per_group(_f32(vw_ref[0]), 1.0)


def _nsa_prep(z3, cos_t, sin_a, sin_b, tt=512):
    bsz, t, _ = z3.shape
    qw = NSA_HEADS * LANES

    def zs(col0, width):
        blk = col0 // width
        return pl.BlockSpec((1, tt, width), lambda b, i: (b, i, blk))

    def os_(width):
        return pl.BlockSpec((1, tt, width), lambda b, i: (b, i, 0))

    tab = pl.BlockSpec((tt, LANES), lambda b, i: (i, 0))
    kv_bf = jax.ShapeDtypeStruct((bsz, t, LANES), jnp.bfloat16)
    return pl.pallas_call(
        functools.partial(_nsa_prep_kernel, tt=tt),
        grid=(bsz, t // tt),
        in_specs=[zs(C_NQ, qw), zs(C_NKC, LANES), zs(C_NVC, LANES), zs(C_NKS, LANES),
                  zs(C_NVS, LANES), zs(C_NKW, LANES), zs(C_NVW, LANES), tab, tab, tab],
        out_specs=[os_(qw), os_(LANES), os_(LANES)] + [os_(LANES)] * 8,
        out_shape=[jax.ShapeDtypeStruct((bsz, t, qw), jnp.bfloat16),
                   jax.ShapeDtypeStruct((bsz, t, LANES), jnp.float32),
                   jax.ShapeDtypeStruct((bsz, t, LANES), jnp.float32)] + [kv_bf] * 8,
        compiler_params=pltpu.CompilerParams(
            dimension_semantics=("parallel", "parallel"), vmem_limit_bytes=VMEM_LIMIT),
        name="nsa_prep",
    )(z3, z3, z3, z3, z3, z3, z3, cos_t, sin_a, sin_b)


def _compress_one(src_ref, pos_ref, w1_ref, w2_ref, out0_ref, out1_ref, ncmp):
    half = CMP_BLK // 2
    u = jnp.zeros((ncmp, 2 * CMP_HID), jnp.float32)
    v = jnp.zeros((ncmp, 2 * CMP_HID), jnp.float32)
    for j in range(half):
        rows = src_ref[0, pl.ds(j, ncmp, stride=CMP_STRIDE), :]
        u = u + _dot(_bf(rows + pos_ref[j:j + 1, :]), w1_ref[j])
        v = v + _dot(_bf(rows + pos_ref[half + j:half + j + 1, :]), w1_ref[half + j])
    hid = u + pltpu.roll(v, ncmp - 1, axis=0)
    out0_ref[0], out1_ref[0] = _per_group(_dot(_bf(jax.nn.gelu(hid)), w2_ref[...]), None)


def _compress_kernel(k_ref, v_ref, pk_ref, pv_ref, w1k_ref, w2k_ref, w1v_ref, w2v_ref,
                     kc0_ref, kc1_ref, vc0_ref, vc1_ref, *, ncmp):
    _compress_one(k_ref, pk_ref, w1k_ref, w2k_ref, kc0_ref, kc1_ref, ncmp)
    _compress_one(v_ref, pv_ref, w1v_ref, w2v_ref, vc0_ref, vc1_ref, ncmp)


def _compress(kc_rot, vc_raw, pos_k, pos_v, w1k, w2k, w1v, w2v):
    bsz, t, _ = kc_rot.shape
    ncmp = t // CMP_STRIDE
    full = lambda shape: pl.BlockSpec(shape, lambda b: (0,) * len(shape))
    out = jax.ShapeDtypeStruct((bsz, ncmp, LANES), jnp.bfloat16)
    return pl.pallas_call(
        functools.partial(_compress_kernel, ncmp=ncmp),
        grid=(bsz,),
        in_specs=[pl.BlockSpec((1, t, LANES), lambda b: (b, 0, 0)),
                  pl.BlockSpec((1, t, LANES), lambda b: (b, 0, 0)),
                  full(pos_k.shape), full(pos_v.shape), full(w1k.shape), full(w2k.shape),
                  full(w1v.shape), full(w2v.shape)],
        out_specs=[pl.BlockSpec((1, ncmp, LANES), lambda b: (b, 0, 0))] * 4,
        out_shape=[out] * 4,
        compiler_params=pltpu.CompilerParams(
            dimension_semantics=("parallel",), vmem_limit_bytes=VMEM_LIMIT),
        name="nsa_compress",
    )(kc_rot, vc_raw, pos_k, pos_v, w1k, w2k, w1v, w2v)


NEG = -1e30
SEL_KB = 256


def _select_blocks(imp, t_col, nslc):
    qb = imp.shape[0]
    blk = lax.broadcasted_iota(jnp.int32, (qb, LANES), 1) - NSA_DH
    cur = t_col // SLC_BLK
    valid = (blk >= 0) & (blk <= cur)
    forced = (blk == 0) | (blk == cur) | (blk == cur - 1)
    score = jnp.where(valid & forced, 1e9, jnp.where(valid, imp, -1e9))
    sc_t = score.T
    sub = 8
    nslab = (LANES - NSA_DH) // sub
    slabs = [sc_t[NSA_DH + sub * k:NSA_DH + sub * (k + 1), :] for k in range(nslab)]
    jj = lax.broadcasted_iota(jnp.int32, (sub, qb), 0)
    ranks = [jnp.zeros((sub, qb), jnp.float32) for _ in range(nslab)]
    for i in range(nslc):
        row = sc_t[NSA_DH + i:NSA_DH + i + 1, :]
        for k in range(nslab):
            if sub * k > i:
                ahead = row >= slabs[k]
            elif sub * (k + 1) <= i:
                ahead = row > slabs[k]
            else:
                ahead = (row > slabs[k]) | ((row == slabs[k]) & (jj + sub * k > i))
            ranks[k] = ranks[k] + jnp.where(ahead, 1.0, 0.0)
    sel_t = jnp.concatenate(
        [jnp.zeros((NSA_DH, qb), jnp.float32)]
        + [jnp.where(r < float(N_SEL), 1.0, 0.0) for r in ranks], axis=0)
    return valid & (sel_t.T > 0.5)


def _nsa_attn_kernel(q_ref, kc0_ref, kc1_ref, vc0_ref, vc1_ref, ks0_ref, ks1_ref, vs0_ref, vs1_ref,
                     kw0_ref, kw1_ref, vw0_ref, vw1_ref, gate_ref, ovl_ref, y_ref,
                     qa_scr, m_scr, acc_scr, sa_scr, sb_scr, *, seq):
    c = pl.program_id(1)
    qb = Q_BLOCK
    rep = NSA_REP
    rows = rep * qb
    t0 = c * qb
    t_col = t0 + lax.broadcasted_iota(jnp.int32, (qb, 1), 0)
    gates = _sigmoid(_f32(gate_ref[0]))
    ncmp = kc0_ref.shape[1]
    lane_low = lax.broadcasted_iota(jnp.int32, (qb, LANES), 1) < NSA_DH
    n_idx = lax.broadcasted_iota(jnp.int32, (1, ncmp), 1)
    bias_c = jnp.where((n_idx * CMP_STRIDE + (CMP_BLK - 1)) <= t_col, 0.0, NEG)
    has_cmp = jnp.where(t_col >= CMP_BLK - 1, 1.0, 0.0)
    w_start = pl.multiple_of(jnp.maximum(t0 - WIN, 0), qb)
    wk = WIN + qb
    kp = w_start + lax.broadcasted_iota(jnp.int32, (1, wk), 1)
    bias_w = jnp.where((kp <= t_col) & (kp > t_col - WIN), 0.0, NEG)
    n_full = t0 // SEL_KB

    def with_bias(s, bias):
        return (s.reshape(rep, qb, s.shape[-1]) + bias[None]).reshape(s.shape)

    groups = ((kc0_ref, vc0_ref, ks0_ref, vs0_ref, kw0_ref, vw0_ref),
              (kc1_ref, vc1_ref, ks1_ref, vs1_ref, kw1_ref, vw1_ref))
    o_cs, acc_ws = [], []
    for g, (kc_ref, vc_ref, ks_ref, vs_ref, kw_ref, vw_ref) in enumerate(groups):
        q_heads = [q_ref[0, :, (g * rep + r) * LANES:(g * rep + r + 1) * LANES] for r in range(rep)]
        qg = jnp.concatenate(q_heads, axis=0)

        s = with_bias(_dot_nt(qg, kc_ref[0]), bias_c)
        m = jnp.max(s, axis=-1, keepdims=True)
        p = (jnp.exp2(s - m).reshape(rep, qb, ncmp) * has_cmp[None]).reshape(rows, ncmp)
        l = jnp.sum(p, axis=-1, keepdims=True)
        p_c = p / jnp.where(l > 0.0, l, 1.0)
        o_cs.append(_dot(_bf(p_c), vc_ref[0]))
        p_sum = p_c[0:qb] + p_c[qb:2 * qb] + p_c[2 * qb:3 * qb] + p_c[3 * qb:4 * qb]
        imp = jnp.dot(p_sum, ovl_ref[...], preferred_element_type=jnp.float32,
                      precision=lax.Precision.HIGHEST)
        sel = _select_blocks(imp, t_col, seq // SLC_BLK)
        sel_bias = _bf(jnp.where(sel, 0.0, NEG))
        for r, qh in enumerate(q_heads):
            qa_scr[g, r * qb:(r + 1) * qb, :] = jnp.where(lane_low, qh, sel_bias)
        m_scr[g] = jnp.full((rows, LANES), NEG, jnp.float32)
        acc_scr[g] = jnp.zeros((rows, LANES), jnp.float32)

    def chunk_start(j):
        return pl.multiple_of(jnp.minimum(j * SEL_KB, seq - SEL_KB), SEL_KB)

    def produce(j, dst_scr):
        k0 = chunk_start(j)
        for g in range(NSA_GROUPS):
            dst_scr[g] = _dot_nt(qa_scr[g], groups[g][2][0, pl.ds(k0, SEL_KB), :])

    def consume(j, src_scr, causal):
        k0 = chunk_start(j)
        if causal:
            tok = j * SEL_KB + lax.broadcasted_iota(jnp.int32, (1, SEL_KB), 1)
            bias = jnp.where(tok <= t_col, 0.0, NEG)
        for g in range(NSA_GROUPS):
            vs_ref = groups[g][3]
            m = m_scr[g]
            s = with_bias(src_scr[g], bias) if causal else src_scr[g]
            m_new = jnp.maximum(m, jnp.max(s, axis=-1, keepdims=True))
            p = jnp.exp2(s - jnp.concatenate([m_new] * (SEL_KB // LANES), axis=1))
            pv = _dot(_bf(p), vs_ref[0, pl.ds(k0, SEL_KB), :])
            acc_scr[g] = jnp.exp2(m - m_new) * acc_scr[g] + pv
            m_scr[g] = m_new

    def loop_body(i, carry):
        produce(2 * i + 1, sb_scr)
        consume(2 * i, sa_scr, False)
        produce(2 * i + 2, sa_scr)
        consume(2 * i + 1, sb_scr, False)
        return carry

    n_pairs = n_full // 2
    produce(0, sa_scr)
    lax.fori_loop(0, n_pairs, loop_body, 0)
    produce(2 * n_pairs + 1, sb_scr)
    consume(2 * n_pairs, sa_scr, True)
    consume(2 * n_pairs + 1, sb_scr, True)
    carry = (None, acc_scr[0], None, acc_scr[1])

    for g in range(NSA_GROUPS):
        kw_ref, vw_ref = groups[g][4], groups[g][5]
        qg = jnp.concatenate(
            [q_ref[0, :, (g * rep + r) * LANES:(g * rep + r + 1) * LANES] for r in range(rep)], axis=0)
        s = with_bias(_dot_nt(qg, kw_ref[0, pl.ds(w_start, wk), :]), bias_w)
        m = jnp.max(s, axis=-1, keepdims=True)
        acc_ws.append(_dot(_bf(jnp.exp2(s - m)), vw_ref[0, pl.ds(w_start, wk), :]))

    def normalized(acc):
        return acc / pltpu.roll(acc, NSA_DH, axis=1)

    for g in range(NSA_GROUPS):
        o_c, o_s, o_w = o_cs[g], normalized(carry[2 * g + 1]), normalized(acc_ws[g])
        for r in range(rep):
            hd = g * rep + r
            sl = slice(r * qb, (r + 1) * qb)
            out = (o_c[sl] * gates[:, 3 * hd:3 * hd + 1] + o_s[sl] * gates[:, 3 * hd + 1:3 * hd + 2]
                   + o_w[sl] * gates[:, 3 * hd + 2:3 * hd + 3])
            y_ref[0, :, hd * LANES:(hd + 1) * LANES] = _bf(jnp.where(lane_low, out, 0.0))


def _nsa_attn(q_rot, kcs, kvs, z3, ovl):
    bsz, t, qw = q_rot.shape
    ncmp = kcs[0].shape[1]
    assert t % SEL_KB == 0 and t >= WIN + Q_BLOCK and t // SLC_BLK <= LANES - NSA_DH
    gblk = C_NGATE // LANES
    per_b = lambda rows: pl.BlockSpec((1, rows, LANES), lambda b, c: (b, 0, 0))
    return pl.pallas_call(
        functools.partial(_nsa_attn_kernel, seq=t),
        grid=(bsz, t // Q_BLOCK),
        in_specs=[pl.BlockSpec((1, Q_BLOCK, qw), lambda b, c: (b, c, 0))]
        + [per_b(ncmp)] * 4 + [per_b(t)] * 8
        + [pl.BlockSpec((1, Q_BLOCK, LANES), lambda b, c: (b, c, gblk)),
           pl.BlockSpec(ovl.shape, lambda b, c: (0, 0))],
        out_specs=pl.BlockSpec((1, Q_BLOCK, qw), lambda b, c: (b, c, 0)),
        out_shape=jax.ShapeDtypeStruct((bsz, t, qw), jnp.bfloat16),
        scratch_shapes=[pltpu.VMEM((NSA_GROUPS, NSA_REP * Q_BLOCK, LANES), jnp.bfloat16),
                        pltpu.VMEM((NSA_GROUPS, NSA_REP * Q_BLOCK, LANES), jnp.float32),
                        pltpu.VMEM((NSA_GROUPS, NSA_REP * Q_BLOCK, LANES), jnp.float32),
                        pltpu.VMEM((NSA_GROUPS, NSA_REP * Q_BLOCK, SEL_KB), jnp.float32),
                        pltpu.VMEM((NSA_GROUPS, NSA_REP * Q_BLOCK, SEL_KB), jnp.float32)],
        compiler_params=pltpu.CompilerParams(
            dimension_semantics=("parallel", "arbitrary"), vmem_limit_bytes=VMEM_LIMIT),
        name="nsa_attn",
    )(q_rot, *kcs, *kvs, z3, ovl)


def _merge_kernel(x_ref, yh_ref, yg_ref, yn_ref, mg_ref, wh_ref, wg_ref, wn_ref, wo_ref, o_ref):
    d = D_MODEL
    m = _sigmoid(_f32(mg_ref[:, 0:d])) * _dot(yh_ref[...], wh_ref[...])
    m = m + _sigmoid(_f32(mg_ref[:, d:2 * d])) * _dot(yg_ref[...], wg_ref[...])
    m = m + _sigmoid(_f32(mg_ref[:, 2 * d:3 * d])) * _dot(yn_ref[...], wn_ref[...])
    o_ref[...] = x_ref[...] + _dot(_bf(m), wo_ref[...])


def _merge(x2, yh, yg, yn, z2, wh, wg, wn, wo, tm=512):
    n, d = x2.shape
    mgblk = C_MG // (3 * d)
    row = lambda w: pl.BlockSpec((tm, w), lambda i: (i, 0))
    full = lambda a: pl.BlockSpec(a.shape, lambda i: (0, 0))
    return pl.pallas_call(
        _merge_kernel,
        grid=(n // tm,),
        in_specs=[row(d), row(yh.shape[1]), row(yg.shape[1]), row(yn.shape[1]),
                  pl.BlockSpec((tm, 3 * d), lambda i: (i, mgblk)),
                  full(wh), full(wg), full(wn), full(wo)],
        out_specs=row(d),
        out_shape=jax.ShapeDtypeStruct((n, d), jnp.float32),
        compiler_params=pltpu.CompilerParams(
            dimension_semantics=("parallel",), vmem_limit_bytes=VMEM_LIMIT),
        name="merge",
    )(x2, yh, yg, yn, z2, wh, wg, wn, wo)


def _ffn_kernel(x_ref, g_ref, wg_ref, wu_ref, wd_ref, fg_ref, o_ref, h_scr, acc_scr, *, final_norm):
    j = pl.program_id(1)

    @pl.when(j == 0)
    def _():
        x = x_ref[...]
        y = x * lax.rsqrt(jnp.mean(x * x, axis=-1, keepdims=True) + EPS)
        h_scr[...] = _bf(y * g_ref[...])
        acc_scr[...] = jnp.zeros_like(acc_scr)

    h = h_scr[...]
    a = _dot(h, wg_ref[...])
    u = _dot(h, wu_ref[...])
    acc_scr[...] += _dot(_bf(a * _sigmoid(a) * u), wd_ref[...])

    @pl.when(j == pl.num_programs(1) - 1)
    def _():
        out = x_ref[...] + acc_scr[...]
        if final_norm:
            out = out * lax.rsqrt(jnp.mean(out * out, axis=-1, keepdims=True) + EPS) * fg_ref[...]
        o_ref[...] = out


def _ffn(x2, g, wg, wu, wd, fg, final_norm, tm=1024, tf=256):
    n, d = x2.shape
    ff = wg.shape[1]
    return pl.pallas_call(
        functools.partial(_ffn_kernel, final_norm=final_norm),
        grid=(n // tm, ff // tf),
        in_specs=[pl.BlockSpec((tm, d), lambda i, j: (i, 0)),
                  pl.BlockSpec((1, d), lambda i, j: (0, 0)),
                  pl.BlockSpec((d, tf), lambda i, j: (0, j)),
                  pl.BlockSpec((d, tf), lambda i, j: (0, j)),
                  pl.BlockSpec((tf, d), lambda i, j: (j, 0)),
                  pl.BlockSpec((1, d), lambda i, j: (0, 0))],
        out_specs=pl.BlockSpec((tm, d), lambda i, j: (i, 0)),
        out_shape=jax.ShapeDtypeStruct((n, d), jnp.float32),
        scratch_shapes=[pltpu.VMEM((tm, d), jnp.bfloat16), pltpu.VMEM((tm, d), jnp.float32)],
        compiler_params=pltpu.CompilerParams(
            dimension_semantics=("parallel", "arbitrary"), vmem_limit_bytes=VMEM_LIMIT),
        name="ffn",
    )(x2, g, wg, wu, wd, fg)


def _pad_heads(w, heads, width):
    lead = w.shape[:-1]
    w = w.reshape(lead + (heads, width))
    w = jnp.pad(w, [(0, 0)] * len(lead) + [(0, 0), (0, LANES - width)])
    return w.reshape(lead + (heads * LANES,))


def _layout_w_in(w):
    d = w.shape[0]
    o = 0
    parts = {}
    for name, n in (("hq", 512), ("hf", 512), ("hi", 512), ("hog", 512), ("gq", 256), ("gk", 256),
                    ("gv", 512), ("glr", 16), ("gog", 512), ("nq", 512), ("nkc", 128), ("nvc", 128),
                    ("nks", 128), ("nvs", 128), ("nkw", 128), ("nvw", 128), ("ngate", 24), ("mg", 3072)):
        parts[name] = w[:, o:o + n]
        o += n
    nq = _pad_heads(parts["nq"], NSA_HEADS, NSA_DH)
    padc = lambda a: jnp.pad(a, ((0, 0), (0, LANES - a.shape[1])))
    cols = [parts["hq"], parts["hf"], parts["hi"], parts["hog"],
            _pad_heads(parts["gq"], GLA_HEADS, GLA_DK), _pad_heads(parts["gk"], GLA_HEADS, GLA_DK),
            parts["gv"], parts["gog"], nq,
            parts["nkc"], parts["nvc"], parts["nks"], parts["nvs"], parts["nkw"], parts["nvw"],
            padc(parts["ngate"]), padc(parts["glr"]), parts["mg"]]
    out = jnp.concatenate(cols, axis=1)
    assert out.shape[1] == Z_COLS
    return _bf(out)


def _blockdiag2(w):
    z = jnp.zeros_like(w)
    top = jnp.concatenate([w, z], axis=-1)
    bot = jnp.concatenate([z, w], axis=-1)
    return jnp.concatenate([top, bot], axis=-2)


def _rope_tables(t):
    pos = jnp.arange(t, dtype=jnp.float32)
    inv_freq = ROPE_THETA ** (-jnp.arange(0, ROT_DIM, 2, dtype=jnp.float32) / ROT_DIM)
    ang = pos[:, None] * inv_freq[None, :]
    cos, sin = jnp.cos(ang), jnp.sin(ang)
    half = ROT_DIM // 2
    lane = jnp.arange(LANES) % NSA_DH
    first = lane < half
    second = (lane >= half) & (lane < ROT_DIM)
    idx = jnp.where(first, lane, jnp.where(second, lane - half, 0))
    cos_t = jnp.where(first | second, cos[:, idx], 1.0)
    sin_a = jnp.where(first, -sin[:, idx], 0.0)
    sin_b = jnp.where(second, sin[:, idx], 0.0)
    return cos_t, sin_a, sin_b


def kernel(x, norm1_g, w_in, hg_lb, hg_norm_g, gla_w2, gla_b, gla_norm_g, cmp_pos_k, cmp_pos_v,
           cmp_w1_k, cmp_w2_k, cmp_w1_v, cmp_w2_v, w_br_hg, w_br_gla, w_br_nsa, w_out,
           norm2_g, w_ffn_gate, w_ffn_up, w_ffn_down, final_norm_g):
    bsz, t, d = x.shape
    depth = w_in.shape[0]
    n = bsz * t
    f32 = jnp.float32

    cos_t, sin_a, sin_b = _rope_tables(t)
    lbs = jnp.cumsum(jax.nn.softmax(hg_lb.astype(f32), axis=0), axis=0)
    lbs = lbs - lbs[0]

    nslc = t // SLC_BLK
    ncmp_pad = t // CMP_STRIDE
    cmp_start = jnp.arange(ncmp_pad) * CMP_STRIDE
    blk = jnp.arange(LANES) - NSA_DH
    ovl = ((cmp_start[:, None] < (blk[None, :] + 1) * SLC_BLK)
           & (cmp_start[:, None] + CMP_BLK - 1 >= blk[None, :] * SLC_BLK)
           & (blk[None, :] >= 0) & (blk[None, :] < nslc)
           & (jnp.arange(ncmp_pad)[:, None] < ncmp_pad - 1)).astype(f32)

    x2 = x.reshape(n, d)
    for l in range(depth):
        z2 = _inproj(x2, norm1_g[l][None, :], _layout_w_in(w_in[l]))
        z3 = z2.reshape(bsz, t, Z_COLS)

        lb = lbs[l].reshape(HG_HEADS, LANES)
        zeros = jnp.zeros_like(lb)
        hg_par = jnp.stack([jnp.log(lb), jnp.log1p(-lb), 1.0 - lb,
                            hg_norm_g[l].reshape(HG_HEADS, LANES), zeros, zeros, zeros, zeros], axis=1)
        y_hg = _hgrn2(z3, hg_par)

        w2p = jnp.pad(_pad_heads(gla_w2[l], GLA_HEADS, GLA_DK), ((0, LANES - GLA_RANK), (0, 0)))
        w2p = _bf(w2p.reshape(LANES, GLA_HEADS, LANES).transpose(1, 0, 2))
        gb = _pad_heads(gla_b[l], GLA_HEADS, GLA_DK).reshape(GLA_HEADS, LANES)
        gla_par = jnp.stack([gb, gla_norm_g[l].reshape(GLA_HEADS, LANES),
                             zeros, zeros, zeros, zeros, zeros, zeros], axis=1)
        y_gla = _gla(z3, w2p, gla_par)

        q_rot, kc_rot, vc_raw, *kvs = _nsa_prep(z3, cos_t, sin_a, sin_b)
        tile2 = lambda p: jnp.concatenate([p, p], axis=1)
        w1k = _bf(_blockdiag2(cmp_w1_k[l].reshape(CMP_BLK, NSA_DH, CMP_HID)))
        w1v = _bf(_blockdiag2(cmp_w1_v[l].reshape(CMP_BLK, NSA_DH, CMP_HID)))
        kcs = _compress(kc_rot, vc_raw, tile2(cmp_pos_k[l]), tile2(cmp_pos_v[l]),
                        w1k, _bf(_blockdiag2(cmp_w2_k[l])), w1v, _bf(_blockdiag2(cmp_w2_v[l])))
        y_nsa = _nsa_attn(q_rot, kcs, kvs, z3, ovl)

        wn = _pad_heads(w_br_nsa[l].T, NSA_HEADS, NSA_DH).T
        x2 = _merge(x2, y_hg.reshape(n, HG_DIM), y_gla.reshape(n, GLA_VDIM),
                    y_nsa.reshape(n, NSA_HEADS * LANES), z2,
                    _bf(w_br_hg[l]), _bf(w_br_gla[l]), _bf(wn), _bf(w_out[l]))

        x2 = _ffn(x2, norm2_g[l][None, :], _bf(w_ffn_gate[l]), _bf(w_ffn_up[l]), _bf(w_ffn_down[l]),
                  final_norm_g[None, :], final_norm=(l == depth - 1))
    return x2.reshape(bsz, t, d)
```

```python
import functools

import jax
import jax.numpy as jnp
from jax import lax
from jax.experimental import pallas as pl
from jax.experimental.pallas import tpu as pltpu

EPS = 1e-6
D_MODEL = 1024
LANES = 128

HG_HEADS = 4
HG_DIM = 512
GLA_HEADS = 4
GLA_DK = 64
GLA_KDIM = 256
GLA_VDIM = 512
GLA_RANK = 16
GLA_TAU = 16.0
LA_CHUNK = 64
LA_SUB = 16

NSA_HEADS = 8
NSA_GROUPS = 2
NSA_REP = 4
NSA_DH = 64
CMP_STRIDE = 16
CMP_BLK = 32
CMP_HID = 128
SLC_BLK = 64
N_SEL = 16
WIN = 512
Q_BLOCK = 128
ROPE_THETA = 500000.0
ROT_DIM = 16
FF_DIM = 2816

C_HQ, C_HF, C_HI, C_HOG = 0, 512, 1024, 1536
C_GQ, C_GK, C_GV, C_GOG = 2048, 2560, 3072, 3584
C_NQ = 4096
C_NKC, C_NVC, C_NKS, C_NVS, C_NKW, C_NVW = 5120, 5248, 5376, 5504, 5632, 5760
C_NGATE, C_GLR = 5888, 6016
C_MG = 6144
Z_COLS = 9216

VMEM_LIMIT = 56 * 1024 * 1024
LOG2E = 1.4426950408889634

_NT = (((1,), (1,)), ((), ()))
_TN = (((0,), (0,)), ((), ()))


def _bf(x):
    return x.astype(jnp.bfloat16)


def _dot(a, b):
    return jnp.dot(a, b, preferred_element_type=jnp.float32)


def _dot_nt(a, b):
    return lax.dot_general(a, b, _NT, preferred_element_type=jnp.float32)


def _dot_tn(a, b):
    return lax.dot_general(a, b, _TN, preferred_element_type=jnp.float32)


def _f32(x):
    return x.astype(jnp.float32)


def _sigmoid(x):
    return 1.0 / (1.0 + jnp.exp(-x))


def _softplus_neg_abs(x):
    return jnp.log(1.0 + jnp.exp(-jnp.abs(x)))


def _log_sigmoid(x):
    return jnp.minimum(x, 0.0) - _softplus_neg_abs(x)


def _inproj_kernel(x_ref, g_ref, w_ref, z_ref, h_scr):
    @pl.when(pl.program_id(1) == 0)
    def _():
        x = x_ref[...]
        y = x * lax.rsqrt(jnp.mean(x * x, axis=-1, keepdims=True) + EPS)
        h_scr[...] = _bf(y * g_ref[...])

    z_ref[...] = _dot(h_scr[...], w_ref[...]).astype(z_ref.dtype)


def _inproj(x2, g, w, tm=1024, tn=1024):
    n, d = x2.shape
    cols = w.shape[1]
    return pl.pallas_call(
        _inproj_kernel,
        grid=(n // tm, cols // tn),
        in_specs=[
            pl.BlockSpec((tm, d), lambda i, j: (i, 0)),
            pl.BlockSpec((1, d), lambda i, j: (0, 0)),
            pl.BlockSpec((d, tn), lambda i, j: (0, j)),
        ],
        out_specs=pl.BlockSpec((tm, tn), lambda i, j: (i, j)),
        out_shape=jax.ShapeDtypeStruct((n, cols), jnp.bfloat16),
        scratch_shapes=[pltpu.VMEM((tm, d), jnp.bfloat16)],
        compiler_params=pltpu.CompilerParams(
            dimension_semantics=("parallel", "arbitrary"), vmem_limit_bytes=VMEM_LIMIT),
        name="inproj",
    )(x2, g, w)


def _chunk_cumsum(g, tb):
    pos = lax.broadcasted_iota(jnp.int32, (tb, 1), 0) % LA_CHUNK
    b = g
    shift = 1
    while shift < LA_CHUNK:
        b = b + jnp.where(pos >= shift, pltpu.roll(b, shift, axis=0), 0.0)
        shift *= 2
    return b


def _la_core(q, k, v, g, st_ref, tb):
    c, r = LA_CHUNK, LA_SUB
    n = tb // c
    nsub = c // r
    dk = q.shape[-1]
    b = _chunk_cumsum(g, tb)
    b3 = b.reshape(n, c, dk)
    q3 = q.reshape(n, c, dk)
    k3 = k.reshape(n, c, dk)
    v3 = _bf(v).reshape(n, c, dk)
    blast = b3[:, c - 1:c, :]
    q_state = _bf(q3 * jnp.exp(b3))
    k_state = _bf(k3 * jnp.exp(blast - b3))

    b4 = b.reshape(n * nsub, r, dk)
    ref = b4[:, 0:1, :]
    q_loc = _bf(q.reshape(n * nsub, r, dk) * jnp.exp(b4 - ref))
    s_idx = lax.broadcasted_iota(jnp.int32, (1, nsub, c, 1), 2)
    i_idx = lax.broadcasted_iota(jnp.int32, (1, nsub, c, 1), 1)
    e = jnp.where(s_idx < r * (i_idx + 1), ref.reshape(n, nsub, 1, dk) - b3[:, None, :, :], 0.0)
    k_loc = _bf(k3[:, None, :, :] * jnp.exp(e)).reshape(n * nsub, c, dk)
    a = jnp.einsum("utd,usd->uts", q_loc, k_loc, preferred_element_type=jnp.float32)
    a = a.reshape(n, c, c)
    t_i = lax.broadcasted_iota(jnp.int32, (1, c, c), 1)
    s_i = lax.broadcasted_iota(jnp.int32, (1, c, c), 2)
    a = jnp.where(s_i <= t_i, a, 0.0)
    o_intra = jnp.einsum("nts,nsv->ntv", _bf(a), v3, preferred_element_type=jnp.float32)

    st = st_ref[...]
    outs = []
    for ci in range(n):
        outs.append(_dot_nt(q_state[ci], _bf(st)) + o_intra[ci])
        st = st * jnp.exp(blast[ci]) + _dot_tn(v3[ci], k_state[ci])
    st_ref[...] = st
    return jnp.concatenate(outs, axis=0)


def _la_finish(o, og, gain, y_ref):
    o = o * lax.rsqrt(jnp.mean(o * o, axis=-1, keepdims=True) + EPS)
    y_ref[0] = _bf(o * gain * (og * _sigmoid(og)))


def _hgrn2_kernel(q_ref, f_ref, i_ref, og_ref, par_ref, y_ref, st_ref, *, tb):
    @pl.when(pl.program_id(2) == 0)
    def _():
        st_ref[...] = jnp.zeros_like(st_ref)

    log_lb = par_ref[0, 0:1, :]
    log1m_lb = par_ref[0, 1:2, :]
    one_m_lb = par_ref[0, 2:3, :]
    gain = par_ref[0, 3:4, :]
    hq = _f32(q_ref[0])
    hf = _f32(f_ref[0])
    q = hq * _sigmoid(hq)
    e = jnp.exp(-jnp.abs(hf))
    cterm = log1m_lb + (jnp.minimum(hf, 0.0) - jnp.log(1.0 + e))
    log_f = jnp.maximum(log_lb, cterm) + _softplus_neg_abs(log_lb - cterm)
    k = one_m_lb * (jnp.where(hf >= 0.0, e, 1.0) / (1.0 + e))
    o = _la_core(q, k, _f32(i_ref[0]), log_f, st_ref, tb)
    _la_finish(o, _f32(og_ref[0]), gain, y_ref)


def _gla_kernel(q_ref, k_ref, v_ref, og_ref, lr_ref, w2_ref, par_ref, y_ref, st_ref, *, tb):
    @pl.when(pl.program_id(2) == 0)
    def _():
        st_ref[...] = jnp.zeros_like(st_ref)

    bias = par_ref[0, 0:1, :]
    gain = par_ref[0, 1:2, :]
    zz = _dot(_bf(lr_ref[0]), w2_ref[0]) + bias
    log_a = _log_sigmoid(zz) * (1.0 / GLA_TAU)
    q = _f32(q_ref[0]) * (GLA_DK ** -0.5)
    o = _la_core(q, _f32(k_ref[0]), _f32(v_ref[0]), log_a, st_ref, tb)
    _la_finish(o, _f32(og_ref[0]), gain, y_ref)


def _zspec(tb, col0):
    base = col0 // LANES
    return pl.BlockSpec((1, tb, LANES), lambda b, h, t: (b, t, base + h))


def _la_params():
    return pltpu.CompilerParams(
        dimension_semantics=("parallel", "parallel", "arbitrary"), vmem_limit_bytes=VMEM_LIMIT)


def _hgrn2(z3, par, tb=512):
    bsz, t, _ = z3.shape
    return pl.pallas_call(
        functools.partial(_hgrn2_kernel, tb=tb),
        grid=(bsz, HG_HEADS, t // tb),
        in_specs=[_zspec(tb, C_HQ), _zspec(tb, C_HF), _zspec(tb, C_HI), _zspec(tb, C_HOG),
                  pl.BlockSpec((1, 8, LANES), lambda b, h, t: (h, 0, 0))],
        out_specs=pl.BlockSpec((1, tb, LANES), lambda b, h, t: (b, t, h)),
        out_shape=jax.ShapeDtypeStruct((bsz, t, HG_DIM), jnp.bfloat16),
        scratch_shapes=[pltpu.VMEM((LANES, LANES), jnp.float32)],
        compiler_params=_la_params(),
        name="hgrn2",
    )(z3, z3, z3, z3, par)


def _gla(z3, w2p, par, tb=512):
    bsz, t, _ = z3.shape
    lr_blk = C_GLR // LANES
    return pl.pallas_call(
        functools.partial(_gla_kernel, tb=tb),
        grid=(bsz, GLA_HEADS, t // tb),
        in_specs=[_zspec(tb, C_GQ), _zspec(tb, C_GK), _zspec(tb, C_GV), _zspec(tb, C_GOG),
                  pl.BlockSpec((1, tb, LANES), lambda b, h, t: (b, t, lr_blk)),
                  pl.BlockSpec((1, LANES, LANES), lambda b, h, t: (h, 0, 0)),
                  pl.BlockSpec((1, 8, LANES), lambda b, h, t: (h, 0, 0))],
        out_specs=pl.BlockSpec((1, tb, LANES), lambda b, h, t: (b, t, h)),
        out_shape=jax.ShapeDtypeStruct((bsz, t, GLA_VDIM), jnp.bfloat16),
        scratch_shapes=[pltpu.VMEM((LANES, LANES), jnp.float32)],
        compiler_params=_la_params(),
        name="gla",
    )(z3, z3, z3, z3, z3, w2p, par)


def _rope(x, cos_t, sin_a, sin_b):
    half = ROT_DIM // 2
    slabs = []
    for i in range(x.shape[-1] // LANES):
        xs = x[:, i * LANES:(i + 1) * LANES]
        slabs.append(xs * cos_t + pltpu.roll(xs, LANES - half, axis=1) * sin_a
                     + pltpu.roll(xs, half, axis=1) * sin_b)
    return slabs[0] if len(slabs) == 1 else jnp.concatenate(slabs, axis=1)


def _per_group(x, filler):
    swapped = pltpu.roll(x, NSA_DH, axis=1)
    if filler is None:
        return _bf(x), _bf(swapped)
    low = lax.broadcasted_iota(jnp.int32, x.shape, 1) < NSA_DH
    return _bf(jnp.where(low, x, filler)), _bf(jnp.where(low, swapped, filler))


def _nsa_prep_kernel(q_ref, kc_ref, vc_ref, ks_ref, vs_ref, kw_ref, vw_ref, cos_ref, sa_ref, sb_ref,
                     qo_ref, kco_ref, vco_ref, ks0_ref, ks1_ref, vs0_ref, vs1_ref, kw0_ref, kw1_ref,
                     vw0_ref, vw1_ref, *, tt):
    ct, sa, sb = cos_ref[...], sa_ref[...], sb_ref[...]
    qo_ref[0] = _bf(_rope(_f32(q_ref[0]), ct, sa, sb) * (NSA_DH ** -0.5 * LOG2E))
    kco_ref[0] = _rope(_f32(kc_ref[0]), ct, sa, sb)
    vco_ref[0] = _f32(vc_ref[0])
    tok = pl.program_id(1) * tt + lax.broadcasted_iota(jnp.int32, (tt, LANES), 0)
    lane = lax.broadcasted_iota(jnp.int32, (tt, LANES), 1)
    onehot = jnp.where(lane - NSA_DH == tok // SLC_BLK, 1.0, 0.0)
    ks0_ref[0], ks1_ref[0] = _per_group(_rope(_f32(ks_ref[0]), ct, sa, sb), onehot)
    kw0_ref[0], kw1_ref[0] = _per_group(_rope(_f32(kw_ref[0]), ct, sa, sb), None)
    vs0_ref[0], vs1_ref[0] = _per_group(_f32(vs_ref[0]), 1.0)
    vw0_ref[0], vw1_ref[0] = _per_group(_f32(vw_ref[0]), 1.0)


def _nsa_prep(z3, cos_t, sin_a, sin_b, tt=512):
    bsz, t, _ = z3.shape
    qw = NSA_HEADS * LANES

    def zs(col0, width):
        blk = col0 // width
        return pl.BlockSpec((1, tt, width), lambda b, i: (b, i, blk))

    def os_(width):
        return pl.BlockSpec((1, tt, width), lambda b, i: (b, i, 0))

    tab = pl.BlockSpec((tt, LANES), lambda b, i: (i, 0))
    kv_bf = jax.ShapeDtypeStruct((bsz, t, LANES), jnp.bfloat16)
    return pl.pallas_call(
        functools.partial(_nsa_prep_kernel, tt=tt),
        grid=(bsz, t // tt),
        in_specs=[zs(C_NQ, qw), zs(C_NKC, LANES), zs(C_NVC, LANES), zs(C_NKS, LANES),
                  zs(C_NVS, LANES), zs(C_NKW, LANES), zs(C_NVW, LANES), tab, tab, tab],
        out_specs=[os_(qw), os_(LANES), os_(LANES)] + [os_(LANES)] * 8,
        out_shape=[jax.ShapeDtypeStruct((bsz, t, qw), jnp.bfloat16),
                   jax.ShapeDtypeStruct((bsz, t, LANES), jnp.float32),
                   jax.ShapeDtypeStruct((bsz, t, LANES), jnp.float32)] + [kv_bf] * 8,
        compiler_params=pltpu.CompilerParams(
            dimension_semantics=("parallel", "parallel"), vmem_limit_bytes=VMEM_LIMIT),
        name="nsa_prep",
    )(z3, z3, z3, z3, z3, z3, z3, cos_t, sin_a, sin_b)


def _compress_one(src_ref, pos_ref, w1_ref, w2_ref, out0_ref, out1_ref, ncmp):
    half = CMP_BLK // 2
    u = jnp.zeros((ncmp, 2 * CMP_HID), jnp.float32)
    v = jnp.zeros((ncmp, 2 * CMP_HID), jnp.float32)
    for j in range(half):
        rows = src_ref[0, pl.ds(j, ncmp, stride=CMP_STRIDE), :]
        u = u + _dot(_bf(rows + pos_ref[j:j + 1, :]), w1_ref[j])
        v = v + _dot(_bf(rows + pos_ref[half + j:half + j + 1, :]), w1_ref[half + j])
    hid = u + pltpu.roll(v, ncmp - 1, axis=0)
    out0_ref[0], out1_ref[0] = _per_group(_dot(_bf(jax.nn.gelu(hid)), w2_ref[...]), None)


def _compress_kernel(k_ref, v_ref, pk_ref, pv_ref, w1k_ref, w2k_ref, w1v_ref, w2v_ref,
                     kc0_ref, kc1_ref, vc0_ref, vc1_ref, *, ncmp):
    _compress_one(k_ref, pk_ref, w1k_ref, w2k_ref, kc0_ref, kc1_ref, ncmp)
    _compress_one(v_ref, pv_ref, w1v_ref, w2v_ref, vc0_ref, vc1_ref, ncmp)


def _compress(kc_rot, vc_raw, pos_k, pos_v, w1k, w2k, w1v, w2v):
    bsz, t, _ = kc_rot.shape
    ncmp = t // CMP_STRIDE
    full = lambda shape: pl.BlockSpec(shape, lambda b: (0,) * len(shape))
    out = jax.ShapeDtypeStruct((bsz, ncmp, LANES), jnp.bfloat16)
    return pl.pallas_call(
        functools.partial(_compress_kernel, ncmp=ncmp),
        grid=(bsz,),
        in_specs=[pl.BlockSpec((1, t, LANES), lambda b: (b, 0, 0)),
                  pl.BlockSpec((1, t, LANES), lambda b: (b, 0, 0)),
                  full(pos_k.shape), full(pos_v.shape), full(w1k.shape), full(w2k.shape),
                  full(w1v.shape), full(w2v.shape)],
        out_specs=[pl.BlockSpec((1, ncmp, LANES), lambda b: (b, 0, 0))] * 4,
        out_shape=[out] * 4,
        compiler_params=pltpu.CompilerParams(
            dimension_semantics=("parallel",), vmem_limit_bytes=VMEM_LIMIT),
        name="nsa_compress",
    )(kc_rot, vc_raw, pos_k, pos_v, w1k, w2k, w1v, w2v)


NEG = -1e30
SEL_KB = 256


def _select_blocks(imp, t_col, nslc):
    qb = imp.shape[0]
    blk = lax.broadcasted_iota(jnp.int32, (qb, LANES), 1) - NSA_DH
    cur = t_col // SLC_BLK
    valid = (blk >= 0) & (blk <= cur)
    forced = (blk == 0) | (blk == cur) | (blk == cur - 1)
    score = jnp.where(valid & forced, 1e9, jnp.where(valid, imp, -1e9))
    sc_t = score.T
    sub = 8
    nslab = (LANES - NSA_DH) // sub
    slabs = [sc_t[NSA_DH + sub * k:NSA_DH + sub * (k + 1), :] for k in range(nslab)]
    jj = lax.broadcasted_iota(jnp.int32, (sub, qb), 0)
    ranks = [jnp.zeros((sub, qb), jnp.float32) for _ in range(nslab)]
    for i in range(nslc):
        row = sc_t[NSA_DH + i:NSA_DH + i + 1, :]
        for k in range(nslab):
            if sub * k > i:
                ahead = row >= slabs[k]
            elif sub * (k + 1) <= i:
                ahead = row > slabs[k]
            else:
                ahead = (row > slabs[k]) | ((row == slabs[k]) & (jj + sub * k > i))
            ranks[k] = ranks[k] + jnp.where(ahead, 1.0, 0.0)
    sel_t = jnp.concatenate(
        [jnp.zeros((NSA_DH, qb), jnp.float32)]
        + [jnp.where(r < float(N_SEL), 1.0, 0.0) for r in ranks], axis=0)
    return valid & (sel_t.T > 0.5)


def _nsa_attn_kernel(q_ref, kc0_ref, kc1_ref, vc0_ref, vc1_ref, ks0_ref, ks1_ref, vs0_ref, vs1_ref,
                     kw0_ref, kw1_ref, vw0_ref, vw1_ref, gate_ref, ovl_ref, y_ref,
                     qa_scr, m_scr, acc_scr, sa_scr, sb_scr, mxa_scr, mxb_scr, *, seq):
    c = pl.program_id(1)
    qb = Q_BLOCK
    rep = NSA_REP
    rows = rep * qb
    t0 = c * qb
    t_col = t0 + lax.broadcasted_iota(jnp.int32, (qb, 1), 0)
    gates = _sigmoid(_f32(gate_ref[0]))
    ncmp = kc0_ref.shape[1]
    lane_low = lax.broadcasted_iota(jnp.int32, (qb, LANES), 1) < NSA_DH
    n_idx = lax.broadcasted_iota(jnp.int32, (1, ncmp), 1)
    bias_c = jnp.where((n_idx * CMP_STRIDE + (CMP_BLK - 1)) <= t_col, 0.0, NEG)
    has_cmp = jnp.where(t_col >= CMP_BLK - 1, 1.0, 0.0)
    w_start = pl.multiple_of(jnp.maximum(t0 - WIN, 0), qb)
    wk = WIN + qb
    kp = w_start + lax.broadcasted_iota(jnp.int32, (1, wk), 1)
    bias_w = jnp.where((kp <= t_col) & (kp > t_col - WIN), 0.0, NEG)
    n_full = t0 // SEL_KB

    def with_bias(s, bias):
        return (s.reshape(rep, qb, s.shape[-1]) + bias[None]).reshape(s.shape)

    groups = ((kc0_ref, vc0_ref, ks0_ref, vs0_ref, kw0_ref, vw0_ref),
              (kc1_ref, vc1_ref, ks1_ref, vs1_ref, kw1_ref, vw1_ref))
    o_cs, acc_ws = [], []
    for g, (kc_ref, vc_ref, ks_ref, vs_ref, kw_ref, vw_ref) in enumerate(groups):
        q_heads = [q_ref[0, :, (g * rep + r) * LANES:(g * rep + r + 1) * LANES] for r in range(rep)]
        qg = jnp.concatenate(q_heads, axis=0)

        s = with_bias(_dot_nt(qg, kc_ref[0]), bias_c)
        m = jnp.max(s, axis=-1, keepdims=True)
        p = (jnp.exp2(s - m).reshape(rep, qb, ncmp) * has_cmp[None]).reshape(rows, ncmp)
        l = jnp.sum(p, axis=-1, keepdims=True)
        p_c = p / jnp.where(l > 0.0, l, 1.0)
        o_cs.append(_dot(_bf(p_c), vc_ref[0]))
        p_sum = p_c[0:qb] + p_c[qb:2 * qb] + p_c[2 * qb:3 * qb] + p_c[3 * qb:4 * qb]
        imp = jnp.dot(p_sum, ovl_ref[...], preferred_element_type=jnp.float32,
                      precision=lax.Precision.HIGHEST)
        sel = _select_blocks(imp, t_col, seq // SLC_BLK)
        sel_bias = _bf(jnp.where(sel, 0.0, NEG))
        for r, qh in enumerate(q_heads):
            qa_scr[g, r * qb:(r + 1) * qb, :] = jnp.where(lane_low, qh, sel_bias)
        m_scr[g] = jnp.full((rows, LANES), NEG, jnp.float32)
        acc_scr[g] = jnp.zeros((rows, LANES), jnp.float32)

    def chunk_start(j):
        return pl.multiple_of(jnp.minimum(j * SEL_KB, seq - SEL_KB), SEL_KB)

    def produce(j, dst_scr, dst_max_scr):
        k0 = chunk_start(j)
        for g in range(NSA_GROUPS):
            s = _dot_nt(qa_scr[g], groups[g][2][0, pl.ds(k0, SEL_KB), :])
            dst_scr[g] = s
            dst_max_scr[g] = jnp.broadcast_to(jnp.max(s, axis=-1, keepdims=True), (rows, LANES))

    def consume(j, src_scr, src_max_scr, causal):
        k0 = chunk_start(j)
        if causal:
            tok = j * SEL_KB + lax.broadcasted_iota(jnp.int32, (1, SEL_KB), 1)
            bias = jnp.where(tok <= t_col, 0.0, NEG)
        for g in range(NSA_GROUPS):
            vs_ref = groups[g][3]
            m = m_scr[g]
            if causal:
                s = with_bias(src_scr[g], bias)
                m_new = jnp.maximum(m, jnp.max(s, axis=-1, keepdims=True))
            else:
                s = src_scr[g]
                m_new = jnp.maximum(m, src_max_scr[g])
            p = jnp.exp2(s - jnp.concatenate([m_new] * (SEL_KB // LANES), axis=1))
            pv = _dot(_bf(p), vs_ref[0, pl.ds(k0, SEL_KB), :])
            acc_scr[g] = jnp.exp2(m - m_new) * acc_scr[g] + pv
            m_scr[g] = m_new

    def loop_body(i, carry):
        produce(2 * i + 1, sb_scr, mxb_scr)
        consume(2 * i, sa_scr, mxa_scr, False)
        produce(2 * i + 2, sa_scr, mxa_scr)
        consume(2 * i + 1, sb_scr, mxb_scr, False)
        return carry

    n_pairs = n_full // 2
    produce(0, sa_scr, mxa_scr)
    lax.fori_loop(0, n_pairs, loop_body, 0)
    produce(2 * n_pairs + 1, sb_scr, mxb_scr)
    consume(2 * n_pairs, sa_scr, mxa_scr, True)

    consume(2 * n_pairs + 1, sb_scr, mxb_scr, True)
    carry = (None, acc_scr[0], None, acc_scr[1])

    for g in range(NSA_GROUPS):
        kw_ref, vw_ref = groups[g][4], groups[g][5]
        qg = jnp.concatenate(
            [q_ref[0, :, (g * rep + r) * LANES:(g * rep + r + 1) * LANES] for r in range(rep)], axis=0)
        s = with_bias(_dot_nt(qg, kw_ref[0, pl.ds(w_start, wk), :]), bias_w)
        m = jnp.max(s, axis=-1, keepdims=True)
        acc_ws.append(_dot(_bf(jnp.exp2(s - m)), vw_ref[0, pl.ds(w_start, wk), :]))

    def normalized(acc):
        return acc / pltpu.roll(acc, NSA_DH, axis=1)

    for g in range(NSA_GROUPS):
        o_c, o_s, o_w = o_cs[g], normalized(carry[2 * g + 1]), normalized(acc_ws[g])
        for r in range(rep):
            hd = g * rep + r
            sl = slice(r * qb, (r + 1) * qb)
            out = (o_c[sl] * gates[:, 3 * hd:3 * hd + 1] + o_s[sl] * gates[:, 3 * hd + 1:3 * hd + 2]
                   + o_w[sl] * gates[:, 3 * hd + 2:3 * hd + 3])
            y_ref[0, :, hd * LANES:(hd + 1) * LANES] = _bf(jnp.where(lane_low, out, 0.0))


def _nsa_attn(q_rot, kcs, kvs, z3, ovl):
    bsz, t, qw = q_rot.shape
    ncmp = kcs[0].shape[1]
    assert t % SEL_KB == 0 and t >= WIN + Q_BLOCK and t // SLC_BLK <= LANES - NSA_DH
    gblk = C_NGATE // LANES
    per_b = lambda rows: pl.BlockSpec((1, rows, LANES), lambda b, c: (b, 0, 0))
    return pl.pallas_call(
        functools.partial(_nsa_attn_kernel, seq=t),
        grid=(bsz, t // Q_BLOCK),
        in_specs=[pl.BlockSpec((1, Q_BLOCK, qw), lambda b, c: (b, c, 0))]
        + [per_b(ncmp)] * 4 + [per_b(t)] * 8
        + [pl.BlockSpec((1, Q_BLOCK, LANES), lambda b, c: (b, c, gblk)),
           pl.BlockSpec(ovl.shape, lambda b, c: (0, 0))],
        out_specs=pl.BlockSpec((1, Q_BLOCK, qw), lambda b, c: (b, c, 0)),
        out_shape=jax.ShapeDtypeStruct((bsz, t, qw), jnp.bfloat16),
        scratch_shapes=[pltpu.VMEM((NSA_GROUPS, NSA_REP * Q_BLOCK, LANES), jnp.bfloat16),
                        pltpu.VMEM((NSA_GROUPS, NSA_REP * Q_BLOCK, LANES), jnp.float32),
                        pltpu.VMEM((NSA_GROUPS, NSA_REP * Q_BLOCK, LANES), jnp.float32),
                        pltpu.VMEM((NSA_GROUPS, NSA_REP * Q_BLOCK, SEL_KB), jnp.float32),
                        pltpu.VMEM((NSA_GROUPS, NSA_REP * Q_BLOCK, SEL_KB), jnp.float32),
                        pltpu.VMEM((NSA_GROUPS, NSA_REP * Q_BLOCK, LANES), jnp.float32),
                        pltpu.VMEM((NSA_GROUPS, NSA_REP * Q_BLOCK, LANES), jnp.float32)],
        compiler_params=pltpu.CompilerParams(
            dimension_semantics=("parallel", "arbitrary"), vmem_limit_bytes=VMEM_LIMIT),
        name="nsa_attn",
    )(q_rot, *kcs, *kvs, z3, ovl)


def _merge_kernel(x_ref, yh_ref, yg_ref, yn_ref, mg_ref, wh_ref, wg_ref, wn_ref, wo_ref, o_ref):
    d = D_MODEL
    m = _sigmoid(_f32(mg_ref[:, 0:d])) * _dot(yh_ref[...], wh_ref[...])
    m = m + _sigmoid(_f32(mg_ref[:, d:2 * d])) * _dot(yg_ref[...], wg_ref[...])
    m = m + _sigmoid(_f32(mg_ref[:, 2 * d:3 * d])) * _dot(yn_ref[...], wn_ref[...])
    o_ref[...] = x_ref[...] + _dot(_bf(m), wo_ref[...])


def _merge(x2, yh, yg, yn, z2, wh, wg, wn, wo, tm=512):
    n, d = x2.shape
    mgblk = C_MG // (3 * d)
    row = lambda w: pl.BlockSpec((tm, w), lambda i: (i, 0))
    full = lambda a: pl.BlockSpec(a.shape, lambda i: (0, 0))
    return pl.pallas_call(
        _merge_kernel,
        grid=(n // tm,),
        in_specs=[row(d), row(yh.shape[1]), row(yg.shape[1]), row(yn.shape[1]),
                  pl.BlockSpec((tm, 3 * d), lambda i: (i, mgblk)),
                  full(wh), full(wg), full(wn), full(wo)],
        out_specs=row(d),
        out_shape=jax.ShapeDtypeStruct((n, d), jnp.float32),
        compiler_params=pltpu.CompilerParams(
            dimension_semantics=("parallel",), vmem_limit_bytes=VMEM_LIMIT),
        name="merge",
    )(x2, yh, yg, yn, z2, wh, wg, wn, wo)


def _ffn_kernel(x_ref, g_ref, wg_ref, wu_ref, wd_ref, fg_ref, o_ref, h_scr, acc_scr, *, final_norm):
    j = pl.program_id(1)

    @pl.when(j == 0)
    def _():
        x = x_ref[...]
        y = x * lax.rsqrt(jnp.mean(x * x, axis=-1, keepdims=True) + EPS)
        h_scr[...] = _bf(y * g_ref[...])
        acc_scr[...] = jnp.zeros_like(acc_scr)

    h = h_scr[...]
    a = _dot(h, wg_ref[...])
    u = _dot(h, wu_ref[...])
    acc_scr[...] += _dot(_bf(a * _sigmoid(a) * u), wd_ref[...])

    @pl.when(j == pl.num_programs(1) - 1)
    def _():
        out = x_ref[...] + acc_scr[...]
        if final_norm:
            out = out * lax.rsqrt(jnp.mean(out * out, axis=-1, keepdims=True) + EPS) * fg_ref[...]
        o_ref[...] = out


def _ffn(x2, g, wg, wu, wd, fg, final_norm, tm=1024, tf=256):
    n, d = x2.shape
    ff = wg.shape[1]
    return pl.pallas_call(
        functools.partial(_ffn_kernel, final_norm=final_norm),
        grid=(n // tm, ff // tf),
        in_specs=[pl.BlockSpec((tm, d), lambda i, j: (i, 0)),
                  pl.BlockSpec((1, d), lambda i, j: (0, 0)),
                  pl.BlockSpec((d, tf), lambda i, j: (0, j)),
                  pl.BlockSpec((d, tf), lambda i, j: (0, j)),
                  pl.BlockSpec((tf, d), lambda i, j: (j, 0)),
                  pl.BlockSpec((1, d), lambda i, j: (0, 0))],
        out_specs=pl.BlockSpec((tm, d), lambda i, j: (i, 0)),
        out_shape=jax.ShapeDtypeStruct((n, d), jnp.float32),
        scratch_shapes=[pltpu.VMEM((tm, d), jnp.bfloat16), pltpu.VMEM((tm, d), jnp.float32)],
        compiler_params=pltpu.CompilerParams(
            dimension_semantics=("parallel", "arbitrary"), vmem_limit_bytes=VMEM_LIMIT),
        name="ffn",
    )(x2, g, wg, wu, wd, fg)


def _pad_heads(w, heads, width):
    lead = w.shape[:-1]
    w = w.reshape(lead + (heads, width))
    w = jnp.pad(w, [(0, 0)] * len(lead) + [(0, 0), (0, LANES - width)])
    return w.reshape(lead + (heads * LANES,))


def _layout_w_in(w):
    d = w.shape[0]
    o = 0
    parts = {}
    for name, n in (("hq", 512), ("hf", 512), ("hi", 512), ("hog", 512), ("gq", 256), ("gk", 256),
                    ("gv", 512), ("glr", 16), ("gog", 512), ("nq", 512), ("nkc", 128), ("nvc", 128),
                    ("nks", 128), ("nvs", 128), ("nkw", 128), ("nvw", 128), ("ngate", 24), ("mg", 3072)):
        parts[name] = w[:, o:o + n]
        o += n
    nq = _pad_heads(parts["nq"], NSA_HEADS, NSA_DH)
    padc = lambda a: jnp.pad(a, ((0, 0), (0, LANES - a.shape[1])))
    cols = [parts["hq"], parts["hf"], parts["hi"], parts["hog"],
            _pad_heads(parts["gq"], GLA_HEADS, GLA_DK), _pad_heads(parts["gk"], GLA_HEADS, GLA_DK),
            parts["gv"], parts["gog"], nq,
            parts["nkc"], parts["nvc"], parts["nks"], parts["nvs"], parts["nkw"], parts["nvw"],
            padc(parts["ngate"]), padc(parts["glr"]), parts["mg"]]
    out = jnp.concatenate(cols, axis=1)
    assert out.shape[1] == Z_COLS
    return _bf(out)


def _blockdiag2(w):
    z = jnp.zeros_like(w)
    top = jnp.concatenate([w, z], axis=-1)
    bot = jnp.concatenate([z, w], axis=-1)
    return jnp.concatenate([top, bot], axis=-2)


def _rope_tables(t):
    pos = jnp.arange(t, dtype=jnp.float32)
    inv_freq = ROPE_THETA ** (-jnp.arange(0, ROT_DIM, 2, dtype=jnp.float32) / ROT_DIM)
    ang = pos[:, None] * inv_freq[None, :]
    cos, sin = jnp.cos(ang), jnp.sin(ang)
    half = ROT_DIM // 2
    lane = jnp.arange(LANES) % NSA_DH
    first = lane < half
    second = (lane >= half) & (lane < ROT_DIM)
    idx = jnp.where(first, lane, jnp.where(second, lane - half, 0))
    cos_t = jnp.where(first | second, cos[:, idx], 1.0)
    sin_a = jnp.where(first, -sin[:, idx], 0.0)
    sin_b = jnp.where(second, sin[:, idx], 0.0)
    return cos_t, sin_a, sin_b


def kernel(x, norm1_g, w_in, hg_lb, hg_norm_g, gla_w2, gla_b, gla_norm_g, cmp_pos_k, cmp_pos_v,
           cmp_w1_k, cmp_w2_k, cmp_w1_v, cmp_w2_v, w_br_hg, w_br_gla, w_br_nsa, w_out,
           norm2_g, w_ffn_gate, w_ffn_up, w_ffn_down, final_norm_g):
    bsz, t, d = x.shape
    depth = w_in.shape[0]
    n = bsz * t
    f32 = jnp.float32

    cos_t, sin_a, sin_b = _rope_tables(t)
    lbs = jnp.cumsum(jax.nn.softmax(hg_lb.astype(f32), axis=0), axis=0)
    lbs = lbs - lbs[0]

    nslc = t // SLC_BLK
    ncmp_pad = t // CMP_STRIDE
    cmp_start = jnp.arange(ncmp_pad) * CMP_STRIDE
    blk = jnp.arange(LANES) - NSA_DH
    ovl = ((cmp_start[:, None] < (blk[None, :] + 1) * SLC_BLK)
           & (cmp_start[:, None] + CMP_BLK - 1 >= blk[None, :] * SLC_BLK)
           & (blk[None, :] >= 0) & (blk[None, :] < nslc)
           & (jnp.arange(ncmp_pad)[:, None] < ncmp_pad - 1)).astype(f32)

    x2 = x.reshape(n, d)
    for l in range(depth):
        z2 = _inproj(x2, norm1_g[l][None, :], _layout_w_in(w_in[l]))
        z3 = z2.reshape(bsz, t, Z_COLS)

        lb = lbs[l].reshape(HG_HEADS, LANES)
        zeros = jnp.zeros_like(lb)
        hg_par = jnp.stack([jnp.log(lb), jnp.log1p(-lb), 1.0 - lb,
                            hg_norm_g[l].reshape(HG_HEADS, LANES), zeros, zeros, zeros, zeros], axis=1)
        y_hg = _hgrn2(z3, hg_par)

        w2p = jnp.pad(_pad_heads(gla_w2[l], GLA_HEADS, GLA_DK), ((0, LANES - GLA_RANK), (0, 0)))
        w2p = _bf(w2p.reshape(LANES, GLA_HEADS, LANES).transpose(1, 0, 2))
        gb = _pad_heads(gla_b[l], GLA_HEADS, GLA_DK).reshape(GLA_HEADS, LANES)
        gla_par = jnp.stack([gb, gla_norm_g[l].reshape(GLA_HEADS, LANES),
                             zeros, zeros, zeros, zeros, zeros, zeros], axis=1)
        y_gla = _gla(z3, w2p, gla_par)

        q_rot, kc_rot, vc_raw, *kvs = _nsa_prep(z3, cos_t, sin_a, sin_b)
        tile2 = lambda p: jnp.concatenate([p, p], axis=1)
        w1k = _bf(_blockdiag2(cmp_w1_k[l].reshape(CMP_BLK, NSA_DH, CMP_HID)))
        w1v = _bf(_blockdiag2(cmp_w1_v[l].reshape(CMP_BLK, NSA_DH, CMP_HID)))
        kcs = _compress(kc_rot, vc_raw, tile2(cmp_pos_k[l]), tile2(cmp_pos_v[l]),
                        w1k, _bf(_blockdiag2(cmp_w2_k[l])), w1v, _bf(_blockdiag2(cmp_w2_v[l])))
        y_nsa = _nsa_attn(q_rot, kcs, kvs, z3, ovl)

        wn = _pad_heads(w_br_nsa[l].T, NSA_HEADS, NSA_DH).T
        x2 = _merge(x2, y_hg.reshape(n, HG_DIM), y_gla.reshape(n, GLA_VDIM),
                    y_nsa.reshape(n, NSA_HEADS * LANES), z2,
                    _bf(w_br_hg[l]), _bf(w_br_gla[l]), _bf(wn), _bf(w_out[l]))

        x2 = _ffn(x2, norm2_g[l][None, :], _bf(w_ffn_gate[l]), _bf(w_ffn_up[l]), _bf(w_ffn_down[l]),
                  final_norm_g[None, :], final_norm=(l == depth - 1))
    return x2.reshape(bsz, t, d)
```

```python
import functools

import jax
import jax.numpy as jnp
from jax import lax
from jax.experimental import pallas as pl
from jax.experimental.pallas import tpu as pltpu

EPS = 1e-6
D_MODEL = 1024
LANES = 128

HG_HEADS = 4
HG_DIM = 512
GLA_HEADS = 4
GLA_DK = 64
GLA_KDIM = 256
GLA_VDIM = 512
GLA_RANK = 16
GLA_TAU = 16.0
LA_CHUNK = 64
LA_SUB = 16

NSA_HEADS = 8
NSA_GROUPS = 2
NSA_REP = 4
NSA_DH = 64
CMP_STRIDE = 16
CMP_BLK = 32
CMP_HID = 128
SLC_BLK = 64
N_SEL = 16
WIN = 512
Q_BLOCK = 128
ROPE_THETA = 500000.0
ROT_DIM = 16
FF_DIM = 2816

C_HQ, C_HF, C_HI, C_HOG = 0, 512, 1024, 1536
C_GQ, C_GK, C_GV, C_GOG = 2048, 2560, 3072, 3584
C_NQ = 4096
C_NKC, C_NVC, C_NKS, C_NVS, C_NKW, C_NVW = 5120, 5248, 5376, 5504, 5632, 5760
C_NGATE, C_GLR = 5888, 6016
C_MG = 6144
Z_COLS = 9216

VMEM_LIMIT = 56 * 1024 * 1024
LOG2E = 1.4426950408889634

_NT = (((1,), (1,)), ((), ()))
_TN = (((0,), (0,)), ((), ()))


def _bf(x):
    return x.astype(jnp.bfloat16)


def _dot(a, b):
    return jnp.dot(a, b, preferred_element_type=jnp.float32)


def _dot_nt(a, b):
    return lax.dot_general(a, b, _NT, preferred_element_type=jnp.float32)


def _dot_tn(a, b):
    return lax.dot_general(a, b, _TN, preferred_element_type=jnp.float32)


def _f32(x):
    return x.astype(jnp.float32)


def _sigmoid(x):
    return 1.0 / (1.0 + jnp.exp(-x))


def _softplus_neg_abs(x):
    return jnp.log(1.0 + jnp.exp(-jnp.abs(x)))


def _log_sigmoid(x):
    return jnp.minimum(x, 0.0) - _softplus_neg_abs(x)


def _inproj_kernel(x_ref, g_ref, w_ref, z_ref, h_scr):
    @pl.when(pl.program_id(1) == 0)
    def _():
        x = x_ref[...]
        y = x * lax.rsqrt(jnp.mean(x * x, axis=-1, keepdims=True) + EPS)
        h_scr[...] = _bf(y * g_ref[...])

    z_ref[...] = _dot(h_scr[...], w_ref[...]).astype(z_ref.dtype)


def _inproj(x2, g, w, tm=1024, tn=2304):
    n, d = x2.shape
    cols = w.shape[1]
    return pl.pallas_call(
        _inproj_kernel,
        grid=(n // tm, cols // tn),
        in_specs=[
            pl.BlockSpec((tm, d), lambda i, j: (i, 0)),
            pl.BlockSpec((1, d), lambda i, j: (0, 0)),
            pl.BlockSpec((d, tn), lambda i, j: (0, j)),
        ],
        out_specs=pl.BlockSpec((tm, tn), lambda i, j: (i, j)),
        out_shape=jax.ShapeDtypeStruct((n, cols), jnp.bfloat16),
        scratch_shapes=[pltpu.VMEM((tm, d), jnp.bfloat16)],
        compiler_params=pltpu.CompilerParams(
            dimension_semantics=("parallel", "arbitrary"), vmem_limit_bytes=VMEM_LIMIT),
        name="inproj",
    )(x2, g, w)


def _chunk_cumsum(g, tb):
    pos = lax.broadcasted_iota(jnp.int32, (tb, 1), 0) % LA_CHUNK
    b = g
    shift = 1
    while shift < LA_CHUNK:
        b = b + jnp.where(pos >= shift, pltpu.roll(b, shift, axis=0), 0.0)
        shift *= 2
    return b


def _la_core(q, k, v, g, st_ref, tb):
    c, r = LA_CHUNK, LA_SUB
    n = tb // c
    nsub = c // r
    dk = q.shape[-1]
    b = _chunk_cumsum(g, tb)
    b3 = b.reshape(n, c, dk)
    q3 = q.reshape(n, c, dk)
    k3 = k.reshape(n, c, dk)
    v3 = _bf(v).reshape(n, c, dk)
    blast = b3[:, c - 1:c, :]
    q_state = _bf(q3 * jnp.exp(b3))
    k_state = _bf(k3 * jnp.exp(blast - b3))

    b4 = b.reshape(n * nsub, r, dk)
    ref = b4[:, 0:1, :]
    q_loc = _bf(q.reshape(n * nsub, r, dk) * jnp.exp(b4 - ref))
    s_idx = lax.broadcasted_iota(jnp.int32, (1, nsub, c, 1), 2)
    i_idx = lax.broadcasted_iota(jnp.int32, (1, nsub, c, 1), 1)
    e = jnp.where(s_idx < r * (i_idx + 1), ref.reshape(n, nsub, 1, dk) - b3[:, None, :, :], 0.0)
    k_loc = _bf(k3[:, None, :, :] * jnp.exp(e)).reshape(n * nsub, c, dk)
    a = jnp.einsum("utd,usd->uts", q_loc, k_loc, preferred_element_type=jnp.float32)
    a = a.reshape(n, c, c)
    t_i = lax.broadcasted_iota(jnp.int32, (1, c, c), 1)
    s_i = lax.broadcasted_iota(jnp.int32, (1, c, c), 2)
    a = jnp.where(s_i <= t_i, a, 0.0)
    o_intra = jnp.einsum("nts,nsv->ntv", _bf(a), v3, preferred_element_type=jnp.float32)

    st = st_ref[...]
    outs = []
    for ci in range(n):
        outs.append(_dot_nt(q_state[ci], _bf(st)) + o_intra[ci])
        st = st * jnp.exp(blast[ci]) + _dot_tn(v3[ci], k_state[ci])
    st_ref[...] = st
    return jnp.concatenate(outs, axis=0)


def _la_finish(o, og, gain, y_ref):
    o = o * lax.rsqrt(jnp.mean(o * o, axis=-1, keepdims=True) + EPS)
    y_ref[0] = _bf(o * gain * (og * _sigmoid(og)))


def _hgrn2_kernel(q_ref, f_ref, i_ref, og_ref, par_ref, y_ref, st_ref, *, tb):
    @pl.when(pl.program_id(2) == 0)
    def _():
        st_ref[...] = jnp.zeros_like(st_ref)

    log_lb = par_ref[0, 0:1, :]
    log1m_lb = par_ref[0, 1:2, :]
    one_m_lb = par_ref[0, 2:3, :]
    gain = par_ref[0, 3:4, :]
    hq = _f32(q_ref[0])
    hf = _f32(f_ref[0])
    q = hq * _sigmoid(hq)
    e = jnp.exp(-jnp.abs(hf))
    cterm = log1m_lb + (jnp.minimum(hf, 0.0) - jnp.log(1.0 + e))
    log_f = jnp.maximum(log_lb, cterm) + _softplus_neg_abs(log_lb - cterm)
    k = one_m_lb * (jnp.where(hf >= 0.0, e, 1.0) / (1.0 + e))
    o = _la_core(q, k, _f32(i_ref[0]), log_f, st_ref, tb)
    _la_finish(o, _f32(og_ref[0]), gain, y_ref)


def _gla_kernel(q_ref, k_ref, v_ref, og_ref, lr_ref, w2_ref, par_ref, y_ref, st_ref, *, tb):
    @pl.when(pl.program_id(2) == 0)
    def _():
        st_ref[...] = jnp.zeros_like(st_ref)

    bias = par_ref[0, 0:1, :]
    gain = par_ref[0, 1:2, :]
    zz = _dot(_bf(lr_ref[0]), w2_ref[0]) + bias
    log_a = _log_sigmoid(zz) * (1.0 / GLA_TAU)
    q = _f32(q_ref[0]) * (GLA_DK ** -0.5)
    o = _la_core(q, _f32(k_ref[0]), _f32(v_ref[0]), log_a, st_ref, tb)
    _la_finish(o, _f32(og_ref[0]), gain, y_ref)


def _zspec(tb, col0):
    base = col0 // LANES
    return pl.BlockSpec((1, tb, LANES), lambda b, h, t: (b, t, base + h))


def _la_params():
    return pltpu.CompilerParams(
        dimension_semantics=("parallel", "parallel", "arbitrary"), vmem_limit_bytes=VMEM_LIMIT)


def _hgrn2(z3, par, tb=512):
    bsz, t, _ = z3.shape
    return pl.pallas_call(
        functools.partial(_hgrn2_kernel, tb=tb),
        grid=(bsz, HG_HEADS, t // tb),
        in_specs=[_zspec(tb, C_HQ), _zspec(tb, C_HF), _zspec(tb, C_HI), _zspec(tb, C_HOG),
                  pl.BlockSpec((1, 8, LANES), lambda b, h, t: (h, 0, 0))],
        out_specs=pl.BlockSpec((1, tb, LANES), lambda b, h, t: (b, t, h)),
        out_shape=jax.ShapeDtypeStruct((bsz, t, HG_DIM), jnp.bfloat16),
        scratch_shapes=[pltpu.VMEM((LANES, LANES), jnp.float32)],
        compiler_params=_la_params(),
        name="hgrn2",
    )(z3, z3, z3, z3, par)


def _gla(z3, w2p, par, tb=512):
    bsz, t, _ = z3.shape
    lr_blk = C_GLR // LANES
    return pl.pallas_call(
        functools.partial(_gla_kernel, tb=tb),
        grid=(bsz, GLA_HEADS, t // tb),
        in_specs=[_zspec(tb, C_GQ), _zspec(tb, C_GK), _zspec(tb, C_GV), _zspec(tb, C_GOG),
                  pl.BlockSpec((1, tb, LANES), lambda b, h, t: (b, t, lr_blk)),
                  pl.BlockSpec((1, LANES, LANES), lambda b, h, t: (h, 0, 0)),
                  pl.BlockSpec((1, 8, LANES), lambda b, h, t: (h, 0, 0))],
        out_specs=pl.BlockSpec((1, tb, LANES), lambda b, h, t: (b, t, h)),
        out_shape=jax.ShapeDtypeStruct((bsz, t, GLA_VDIM), jnp.bfloat16),
        scratch_shapes=[pltpu.VMEM((LANES, LANES), jnp.float32)],
        compiler_params=_la_params(),
        name="gla",
    )(z3, z3, z3, z3, z3, w2p, par)


def _rope(x, cos_t, sin_a, sin_b):
    half = ROT_DIM // 2
    slabs = []
    for i in range(x.shape[-1] // LANES):
        xs = x[:, i * LANES:(i + 1) * LANES]
        slabs.append(xs * cos_t + pltpu.roll(xs, LANES - half, axis=1) * sin_a
                     + pltpu.roll(xs, half, axis=1) * sin_b)
    return slabs[0] if len(slabs) == 1 else jnp.concatenate(slabs, axis=1)


def _per_group(x, filler):
    swapped = pltpu.roll(x, NSA_DH, axis=1)
    if filler is None:
        return _bf(x), _bf(swapped)
    low = lax.broadcasted_iota(jnp.int32, x.shape, 1) < NSA_DH
    return _bf(jnp.where(low, x, filler)), _bf(jnp.where(low, swapped, filler))


def _nsa_prep_kernel(q_ref, kc_ref, vc_ref, ks_ref, vs_ref, kw_ref, vw_ref, cos_ref, sa_ref, sb_ref,
                     qo_ref, kco_ref, vco_ref, ks0_ref, ks1_ref, vs0_ref, vs1_ref, kw0_ref, kw1_ref,
                     vw0_ref, vw1_ref, *, tt):
    ct, sa, sb = cos_ref[...], sa_ref[...], sb_ref[...]
    qo_ref[0] = _bf(_rope(_f32(q_ref[0]), ct, sa, sb) * (NSA_DH ** -0.5 * LOG2E))
    kco_ref[0] = _rope(_f32(kc_ref[0]), ct, sa, sb)
    vco_ref[0] = _f32(vc_ref[0])
    tok = pl.program_id(1) * tt + lax.broadcasted_iota(jnp.int32, (tt, LANES), 0)
    lane = lax.broadcasted_iota(jnp.int32, (tt, LANES), 1)
    onehot = jnp.where(lane - NSA_DH == tok // SLC_BLK, 1.0, 0.0)
    ks0_ref[0], ks1_ref[0] = _per_group(_rope(_f32(ks_ref[0]), ct, sa, sb), onehot)
    kw0_ref[0], kw1_ref[0] = _per_group(_rope(_f32(kw_ref[0]), ct, sa, sb), None)
    vs0_ref[0], vs1_ref[0] = _per_group(_f32(vs_ref[0]), 1.0)
    vw0_ref[0], vw1_ref[0] = _per_group(_f32(vw_ref[0]), 1.0)


def _nsa_prep(z3, cos_t, sin_a, sin_b, tt=512):
    bsz, t, _ = z3.shape
    qw = NSA_HEADS * LANES

    def zs(col0, width):
        blk = col0 // width
        return pl.BlockSpec((1, tt, width), lambda b, i: (b, i, blk))

    def os_(width):
        return pl.BlockSpec((1, tt, width), lambda b, i: (b, i, 0))

    tab = pl.BlockSpec((tt, LANES), lambda b, i: (i, 0))
    kv_bf = jax.ShapeDtypeStruct((bsz, t, LANES), jnp.bfloat16)
    return pl.pallas_call(
        functools.partial(_nsa_prep_kernel, tt=tt),
        grid=(bsz, t // tt),
        in_specs=[zs(C_NQ, qw), zs(C_NKC, LANES), zs(C_NVC, LANES), zs(C_NKS, LANES),
                  zs(C_NVS, LANES), zs(C_NKW, LANES), zs(C_NVW, LANES), tab, tab, tab],
        out_specs=[os_(qw), os_(LANES), os_(LANES)] + [os_(LANES)] * 8,
        out_shape=[jax.ShapeDtypeStruct((bsz, t, qw), jnp.bfloat16),
                   jax.ShapeDtypeStruct((bsz, t, LANES), jnp.float32),
                   jax.ShapeDtypeStruct((bsz, t, LANES), jnp.float32)] + [kv_bf] * 8,
        compiler_params=pltpu.CompilerParams(
            dimension_semantics=("parallel", "parallel"), vmem_limit_bytes=VMEM_LIMIT),
        name="nsa_prep",
    )(z3, z3, z3, z3, z3, z3, z3, cos_t, sin_a, sin_b)


def _compress_one(src_ref, pos_ref, w1_ref, w2_ref, out0_ref, out1_ref, ncmp):
    half = CMP_BLK // 2
    u = jnp.zeros((ncmp, 2 * CMP_HID), jnp.float32)
    v = jnp.zeros((ncmp, 2 * CMP_HID), jnp.float32)
    for j in range(half):
        rows = src_ref[0, pl.ds(j, ncmp, stride=CMP_STRIDE), :]
        u = u + _dot(_bf(rows + pos_ref[j:j + 1, :]), w1_ref[j])
        v = v + _dot(_bf(rows + pos_ref[half + j:half + j + 1, :]), w1_ref[half + j])
    hid = u + pltpu.roll(v, ncmp - 1, axis=0)
    out0_ref[0], out1_ref[0] = _per_group(_dot(_bf(jax.nn.gelu(hid)), w2_ref[...]), None)


def _compress_kernel(k_ref, v_ref, pk_ref, pv_ref, w1k_ref, w2k_ref, w1v_ref, w2v_ref,
                     kc0_ref, kc1_ref, vc0_ref, vc1_ref, *, ncmp):
    _compress_one(k_ref, pk_ref, w1k_ref, w2k_ref, kc0_ref, kc1_ref, ncmp)
    _compress_one(v_ref, pv_ref, w1v_ref, w2v_ref, vc0_ref, vc1_ref, ncmp)


def _compress(kc_rot, vc_raw, pos_k, pos_v, w1k, w2k, w1v, w2v):
    bsz, t, _ = kc_rot.shape
    ncmp = t // CMP_STRIDE
    full = lambda shape: pl.BlockSpec(shape, lambda b: (0,) * len(shape))
    out = jax.ShapeDtypeStruct((bsz, ncmp, LANES), jnp.bfloat16)
    return pl.pallas_call(
        functools.partial(_compress_kernel, ncmp=ncmp),
        grid=(bsz,),
        in_specs=[pl.BlockSpec((1, t, LANES), lambda b: (b, 0, 0)),
                  pl.BlockSpec((1, t, LANES), lambda b: (b, 0, 0)),
                  full(pos_k.shape), full(pos_v.shape), full(w1k.shape), full(w2k.shape),
                  full(w1v.shape), full(w2v.shape)],
        out_specs=[pl.BlockSpec((1, ncmp, LANES), lambda b: (b, 0, 0))] * 4,
        out_shape=[out] * 4,
        compiler_params=pltpu.CompilerParams(
            dimension_semantics=("parallel",), vmem_limit_bytes=VMEM_LIMIT),
        name="nsa_compress",
    )(kc_rot, vc_raw, pos_k, pos_v, w1k, w2k, w1v, w2v)


NEG = -1e30
SEL_KB = 256


def _select_blocks(imp, t_col, nslc):
    qb = imp.shape[0]
    blk = lax.broadcasted_iota(jnp.int32, (qb, LANES), 1) - NSA_DH
    cur = t_col // SLC_BLK
    valid = (blk >= 0) & (blk <= cur)
    forced = (blk == 0) | (blk == cur) | (blk == cur - 1)
    score = jnp.where(valid & forced, 1e9, jnp.where(valid, imp, -1e9))
    sc_t = score.T
    sub = 8
    nslab = (LANES - NSA_DH) // sub
    slabs = [sc_t[NSA_DH + sub * k:NSA_DH + sub * (k + 1), :] for k in range(nslab)]
    jj = lax.broadcasted_iota(jnp.int32, (sub, qb), 0)
    ranks = [jnp.zeros((sub, qb), jnp.float32) for _ in range(nslab)]
    for i in range(nslc):
        row = sc_t[NSA_DH + i:NSA_DH + i + 1, :]
        for k in range(nslab):
            if sub * k > i:
                ahead = row >= slabs[k]
            elif sub * (k + 1) <= i:
                ahead = row > slabs[k]
            else:
                ahead = (row > slabs[k]) | ((row == slabs[k]) & (jj + sub * k > i))
            ranks[k] = ranks[k] + jnp.where(ahead, 1.0, 0.0)
    sel_t = jnp.concatenate(
        [jnp.zeros((NSA_DH, qb), jnp.float32)]
        + [jnp.where(r < float(N_SEL), 1.0, 0.0) for r in ranks], axis=0)
    return valid & (sel_t.T > 0.5)


def _nsa_attn_kernel(q_ref, kc0_ref, kc1_ref, vc0_ref, vc1_ref, ks0_ref, ks1_ref, vs0_ref, vs1_ref,
                     kw0_ref, kw1_ref, vw0_ref, vw1_ref, gate_ref, ovl_ref, y_ref,
                     qa_scr, m_scr, acc_scr, sa_scr, sb_scr, mxa_scr, mxb_scr, *, seq):
    c = pl.program_id(1)
    qb = Q_BLOCK
    rep = NSA_REP
    rows = rep * qb
    t0 = c * qb
    t_col = t0 + lax.broadcasted_iota(jnp.int32, (qb, 1), 0)
    gates = _sigmoid(_f32(gate_ref[0]))
    ncmp = kc0_ref.shape[1]
    lane_low = lax.broadcasted_iota(jnp.int32, (qb, LANES), 1) < NSA_DH
    n_idx = lax.broadcasted_iota(jnp.int32, (1, ncmp), 1)
    bias_c = jnp.where((n_idx * CMP_STRIDE + (CMP_BLK - 1)) <= t_col, 0.0, NEG)
    has_cmp = jnp.where(t_col >= CMP_BLK - 1, 1.0, 0.0)
    w_start = pl.multiple_of(jnp.maximum(t0 - WIN, 0), qb)
    wk = WIN + qb
    kp = w_start + lax.broadcasted_iota(jnp.int32, (1, wk), 1)
    bias_w = jnp.where((kp <= t_col) & (kp > t_col - WIN), 0.0, NEG)
    n_full = t0 // SEL_KB

    def with_bias(s, bias):
        return (s.reshape(rep, qb, s.shape[-1]) + bias[None]).reshape(s.shape)

    groups = ((kc0_ref, vc0_ref, ks0_ref, vs0_ref, kw0_ref, vw0_ref),
              (kc1_ref, vc1_ref, ks1_ref, vs1_ref, kw1_ref, vw1_ref))
    o_cs, acc_ws = [], []
    for g, (kc_ref, vc_ref, ks_ref, vs_ref, kw_ref, vw_ref) in enumerate(groups):
        q_heads = [q_ref[0, :, (g * rep + r) * LANES:(g * rep + r + 1) * LANES] for r in range(rep)]
        qg = jnp.concatenate(q_heads, axis=0)

        s = with_bias(_dot_nt(qg, kc_ref[0]), bias_c)
        m = jnp.max(s, axis=-1, keepdims=True)
        p = (jnp.exp2(s - m).reshape(rep, qb, ncmp) * has_cmp[None]).reshape(rows, ncmp)
        l = jnp.sum(p, axis=-1, keepdims=True)
        p_c = p / jnp.where(l > 0.0, l, 1.0)
        o_cs.append(_dot(_bf(p_c), vc_ref[0]))
        p_sum = p_c[0:qb] + p_c[qb:2 * qb] + p_c[2 * qb:3 * qb] + p_c[3 * qb:4 * qb]
        imp = jnp.dot(p_sum, ovl_ref[...], preferred_element_type=jnp.float32,
                      precision=lax.Precision.HIGHEST)
        sel = _select_blocks(imp, t_col, seq // SLC_BLK)
        sel_bias = _bf(jnp.where(sel, 0.0, NEG))
        for r, qh in enumerate(q_heads):
            qa_scr[g, r * qb:(r + 1) * qb, :] = jnp.where(lane_low, qh, sel_bias)
        m_scr[g] = jnp.full((rows, LANES), NEG, jnp.float32)
        acc_scr[g] = jnp.zeros((rows, LANES), jnp.float32)

    def chunk_start(j):
        return pl.multiple_of(jnp.minimum(j * SEL_KB, seq - SEL_KB), SEL_KB)

    def produce(j, dst_scr, dst_max_scr):
        k0 = chunk_start(j)
        for g in range(NSA_GROUPS):
            s = _dot_nt(qa_scr[g], groups[g][2][0, pl.ds(k0, SEL_KB), :])
            dst_scr[g] = s
            dst_max_scr[g] = jnp.broadcast_to(jnp.max(s, axis=-1, keepdims=True), (rows, LANES))

    def consume(j, src_scr, src_max_scr, causal):
        k0 = chunk_start(j)
        if causal:
            tok = j * SEL_KB + lax.broadcasted_iota(jnp.int32, (1, SEL_KB), 1)
            bias = jnp.where(tok <= t_col, 0.0, NEG)
        for g in range(NSA_GROUPS):
            vs_ref = groups[g][3]
            m = m_scr[g]
            if causal:
                s = with_bias(src_scr[g], bias)
                m_new = jnp.maximum(m, jnp.max(s, axis=-1, keepdims=True))
            else:
                s = src_scr[g]
                m_new = jnp.maximum(m, src_max_scr[g])
            p = jnp.exp2(s - jnp.concatenate([m_new] * (SEL_KB // LANES), axis=1))
            pv = _dot(_bf(p), vs_ref[0, pl.ds(k0, SEL_KB), :])
            acc_scr[g] = jnp.exp2(m - m_new) * acc_scr[g] + pv
            m_scr[g] = m_new

    def loop_body(i, carry):
        produce(2 * i + 1, sb_scr, mxb_scr)
        consume(2 * i, sa_scr, mxa_scr, False)
        produce(2 * i + 2, sa_scr, mxa_scr)
        consume(2 * i + 1, sb_scr, mxb_scr, False)
        return carry

    n_pairs = n_full // 2
    produce(0, sa_scr, mxa_scr)
    lax.fori_loop(0, n_pairs, loop_body, 0)
    produce(2 * n_pairs + 1, sb_scr, mxb_scr)
    consume(2 * n_pairs, sa_scr, mxa_scr, True)

    consume(2 * n_pairs + 1, sb_scr, mxb_scr, True)
    carry = (None, acc_scr[0], None, acc_scr[1])

    for g in range(NSA_GROUPS):
        kw_ref, vw_ref = groups[g][4], groups[g][5]
        qg = jnp.concatenate(
            [q_ref[0, :, (g * rep + r) * LANES:(g * rep + r + 1) * LANES] for r in range(rep)], axis=0)
        s = with_bias(_dot_nt(qg, kw_ref[0, pl.ds(w_start, wk), :]), bias_w)
        m = jnp.max(s, axis=-1, keepdims=True)
        acc_ws.append(_dot(_bf(jnp.exp2(s - m)), vw_ref[0, pl.ds(w_start, wk), :]))

    def normalized(acc):
        return acc / pltpu.roll(acc, NSA_DH, axis=1)

    for g in range(NSA_GROUPS):
        o_c, o_s, o_w = o_cs[g], normalized(carry[2 * g + 1]), normalized(acc_ws[g])
        for r in range(rep):
            hd = g * rep + r
            sl = slice(r * qb, (r + 1) * qb)
            out = (o_c[sl] * gates[:, 3 * hd:3 * hd + 1] + o_s[sl] * gates[:, 3 * hd + 1:3 * hd + 2]
                   + o_w[sl] * gates[:, 3 * hd + 2:3 * hd + 3])
            y_ref[0, :, hd * LANES:(hd + 1) * LANES] = _bf(jnp.where(lane_low, out, 0.0))


def _nsa_attn(q_rot, kcs, kvs, z3, ovl):
    bsz, t, qw = q_rot.shape
    ncmp = kcs[0].shape[1]
    assert t % SEL_KB == 0 and t >= WIN + Q_BLOCK and t // SLC_BLK <= LANES - NSA_DH
    gblk = C_NGATE // LANES
    per_b = lambda rows: pl.BlockSpec((1, rows, LANES), lambda b, c: (b, 0, 0))
    return pl.pallas_call(
        functools.partial(_nsa_attn_kernel, seq=t),
        grid=(bsz, t // Q_BLOCK),
        in_specs=[pl.BlockSpec((1, Q_BLOCK, qw), lambda b, c: (b, c, 0))]
        + [per_b(ncmp)] * 4 + [per_b(t)] * 8
        + [pl.BlockSpec((1, Q_BLOCK, LANES), lambda b, c: (b, c, gblk)),
           pl.BlockSpec(ovl.shape, lambda b, c: (0, 0))],
        out_specs=pl.BlockSpec((1, Q_BLOCK, qw), lambda b, c: (b, c, 0)),
        out_shape=jax.ShapeDtypeStruct((bsz, t, qw), jnp.bfloat16),
        scratch_shapes=[pltpu.VMEM((NSA_GROUPS, NSA_REP * Q_BLOCK, LANES), jnp.bfloat16),
                        pltpu.VMEM((NSA_GROUPS, NSA_REP * Q_BLOCK, LANES), jnp.float32),
                        pltpu.VMEM((NSA_GROUPS, NSA_REP * Q_BLOCK, LANES), jnp.float32),
                        pltpu.VMEM((NSA_GROUPS, NSA_REP * Q_BLOCK, SEL_KB), jnp.float32),
                        pltpu.VMEM((NSA_GROUPS, NSA_REP * Q_BLOCK, SEL_KB), jnp.float32),
                        pltpu.VMEM((NSA_GROUPS, NSA_REP * Q_BLOCK, LANES), jnp.float32),
                        pltpu.VMEM((NSA_GROUPS, NSA_REP * Q_BLOCK, LANES), jnp.float32)],
        compiler_params=pltpu.CompilerParams(
            dimension_semantics=("parallel", "arbitrary"), vmem_limit_bytes=VMEM_LIMIT),
        name="nsa_attn",
    )(q_rot, *kcs, *kvs, z3, ovl)


def _merge_kernel(x_ref, yh_ref, yg_ref, yn_ref, mg_ref, wh_ref, wg_ref, wn_ref, wo_ref, o_ref):
    d = D_MODEL
    m = _sigmoid(_f32(mg_ref[:, 0:d])) * _dot(yh_ref[...], wh_ref[...])
    m = m + _sigmoid(_f32(mg_ref[:, d:2 * d])) * _dot(yg_ref[...], wg_ref[...])
    m = m + _sigmoid(_f32(mg_ref[:, 2 * d:3 * d])) * _dot(yn_ref[...], wn_ref[...])
    o_ref[...] = x_ref[...] + _dot(_bf(m), wo_ref[...])


def _merge(x2, yh, yg, yn, z2, wh, wg, wn, wo, tm=512):
    n, d = x2.shape
    mgblk = C_MG // (3 * d)
    row = lambda w: pl.BlockSpec((tm, w), lambda i: (i, 0))
    full = lambda a: pl.BlockSpec(a.shape, lambda i: (0, 0))
    return pl.pallas_call(
        _merge_kernel,
        grid=(n // tm,),
        in_specs=[row(d), row(yh.shape[1]), row(yg.shape[1]), row(yn.shape[1]),
                  pl.BlockSpec((tm, 3 * d), lambda i: (i, mgblk)),
                  full(wh), full(wg), full(wn), full(wo)],
        out_specs=row(d),
        out_shape=jax.ShapeDtypeStruct((n, d), jnp.float32),
        compiler_params=pltpu.CompilerParams(
            dimension_semantics=("parallel",), vmem_limit_bytes=VMEM_LIMIT),
        name="merge",
    )(x2, yh, yg, yn, z2, wh, wg, wn, wo)


def _ffn_kernel(x_ref, g_ref, wg_ref, wu_ref, wd_ref, fg_ref, o_ref, h_scr, acc_scr, *, final_norm):
    j = pl.program_id(1)

    @pl.when(j == 0)
    def _():
        x = x_ref[...]
        y = x * lax.rsqrt(jnp.mean(x * x, axis=-1, keepdims=True) + EPS)
        h_scr[...] = _bf(y * g_ref[...])
        acc_scr[...] = jnp.zeros_like(acc_scr)

    h = h_scr[...]
    a = _dot(h, wg_ref[...])
    u = _dot(h, wu_ref[...])
    acc_scr[...] += _dot(_bf(a * _sigmoid(a) * u), wd_ref[...])

    @pl.when(j == pl.num_programs(1) - 1)
    def _():
        out = x_ref[...] + acc_scr[...]
        if final_norm:
            out = out * lax.rsqrt(jnp.mean(out * out, axis=-1, keepdims=True) + EPS) * fg_ref[...]
        o_ref[...] = out


def _ffn(x2, g, wg, wu, wd, fg, final_norm, tm=1024, tf=256):
    n, d = x2.shape
    ff = wg.shape[1]
    return pl.pallas_call(
        functools.partial(_ffn_kernel, final_norm=final_norm),
        grid=(n // tm, ff // tf),
        in_specs=[pl.BlockSpec((tm, d), lambda i, j: (i, 0)),
                  pl.BlockSpec((1, d), lambda i, j: (0, 0)),
                  pl.BlockSpec((d, tf), lambda i, j: (0, j)),
                  pl.BlockSpec((d, tf), lambda i, j: (0, j)),
                  pl.BlockSpec((tf, d), lambda i, j: (j, 0)),
                  pl.BlockSpec((1, d), lambda i, j: (0, 0))],
        out_specs=pl.BlockSpec((tm, d), lambda i, j: (i, 0)),
        out_shape=jax.ShapeDtypeStruct((n, d), jnp.float32),
        scratch_shapes=[pltpu.VMEM((tm, d), jnp.bfloat16), pltpu.VMEM((tm, d), jnp.float32)],
        compiler_params=pltpu.CompilerParams(
            dimension_semantics=("parallel", "arbitrary"), vmem_limit_bytes=VMEM_LIMIT),
        name="ffn",
    )(x2, g, wg, wu, wd, fg)


def _pad_heads(w, heads, width):
    lead = w.shape[:-1]
    w = w.reshape(lead + (heads, width))
    w = jnp.pad(w, [(0, 0)] * len(lead) + [(0, 0), (0, LANES - width)])
    return w.reshape(lead + (heads * LANES,))


def _layout_w_in(w):
    d = w.shape[0]
    o = 0
    parts = {}
    for name, n in (("hq", 512), ("hf", 512), ("hi", 512), ("hog", 512), ("gq", 256), ("gk", 256),
                    ("gv", 512), ("glr", 16), ("gog", 512), ("nq", 512), ("nkc", 128), ("nvc", 128),
                    ("nks", 128), ("nvs", 128), ("nkw", 128), ("nvw", 128), ("ngate", 24), ("mg", 3072)):
        parts[name] = w[:, o:o + n]
        o += n
    nq = _pad_heads(parts["nq"], NSA_HEADS, NSA_DH)
    padc = lambda a: jnp.pad(a, ((0, 0), (0, LANES - a.shape[1])))
    cols = [parts["hq"], parts["hf"], parts["hi"], parts["hog"],
            _pad_heads(parts["gq"], GLA_HEADS, GLA_DK), _pad_heads(parts["gk"], GLA_HEADS, GLA_DK),
            parts["gv"], parts["gog"], nq,
            parts["nkc"], parts["nvc"], parts["nks"], parts["nvs"], parts["nkw"], parts["nvw"],
            padc(parts["ngate"]), padc(parts["glr"]), parts["mg"]]
    out = jnp.concatenate(cols, axis=1)
    assert out.shape[1] == Z_COLS
    return _bf(out)


def _blockdiag2(w):
    z = jnp.zeros_like(w)
    top = jnp.concatenate([w, z], axis=-1)
    bot = jnp.concatenate([z, w], axis=-1)
    return jnp.concatenate([top, bot], axis=-2)


def _rope_tables(t):
    pos = jnp.arange(t, dtype=jnp.float32)
    inv_freq = ROPE_THETA ** (-jnp.arange(0, ROT_DIM, 2, dtype=jnp.float32) / ROT_DIM)
    ang = pos[:, None] * inv_freq[None, :]
    cos, sin = jnp.cos(ang), jnp.sin(ang)
    half = ROT_DIM // 2
    lane = jnp.arange(LANES) % NSA_DH
    first = lane < half
    second = (lane >= half) & (lane < ROT_DIM)
    idx = jnp.where(first, lane, jnp.where(second, lane - half, 0))
    cos_t = jnp.where(first | second, cos[:, idx], 1.0)
    sin_a = jnp.where(first, -sin[:, idx], 0.0)
    sin_b = jnp.where(second, sin[:, idx], 0.0)
    return cos_t, sin_a, sin_b


def kernel(x, norm1_g, w_in, hg_lb, hg_norm_g, gla_w2, gla_b, gla_norm_g, cmp_pos_k, cmp_pos_v,
           cmp_w1_k, cmp_w2_k, cmp_w1_v, cmp_w2_v, w_br_hg, w_br_gla, w_br_nsa, w_out,
           norm2_g, w_ffn_gate, w_ffn_up, w_ffn_down, final_norm_g):
    bsz, t, d = x.shape
    depth = w_in.shape[0]
    n = bsz * t
    f32 = jnp.float32

    cos_t, sin_a, sin_b = _rope_tables(t)
    lbs = jnp.cumsum(jax.nn.softmax(hg_lb.astype(f32), axis=0), axis=0)
    lbs = lbs - lbs[0]

    nslc = t // SLC_BLK
    ncmp_pad = t // CMP_STRIDE
    cmp_start = jnp.arange(ncmp_pad) * CMP_STRIDE
    blk = jnp.arange(LANES) - NSA_DH
    ovl = ((cmp_start[:, None] < (blk[None, :] + 1) * SLC_BLK)
           & (cmp_start[:, None] + CMP_BLK - 1 >= blk[None, :] * SLC_BLK)
           & (blk[None, :] >= 0) & (blk[None, :] < nslc)
           & (jnp.arange(ncmp_pad)[:, None] < ncmp_pad - 1)).astype(f32)

    x2 = x.reshape(n, d)
    for l in range(depth):
        z2 = _inproj(x2, norm1_g[l][None, :], _layout_w_in(w_in[l]))
        z3 = z2.reshape(bsz, t, Z_COLS)

        lb = lbs[l].reshape(HG_HEADS, LANES)
        zeros = jnp.zeros_like(lb)
        hg_par = jnp.stack([jnp.log(lb), jnp.log1p(-lb), 1.0 - lb,
                            hg_norm_g[l].reshape(HG_HEADS, LANES), zeros, zeros, zeros, zeros], axis=1)
        y_hg = _hgrn2(z3, hg_par)

        w2p = jnp.pad(_pad_heads(gla_w2[l], GLA_HEADS, GLA_DK), ((0, LANES - GLA_RANK), (0, 0)))
        w2p = _bf(w2p.reshape(LANES, GLA_HEADS, LANES).transpose(1, 0, 2))
        gb = _pad_heads(gla_b[l], GLA_HEADS, GLA_DK).reshape(GLA_HEADS, LANES)
        gla_par = jnp.stack([gb, gla_norm_g[l].reshape(GLA_HEADS, LANES),
                             zeros, zeros, zeros, zeros, zeros, zeros], axis=1)
        y_gla = _gla(z3, w2p, gla_par)

        q_rot, kc_rot, vc_raw, *kvs = _nsa_prep(z3, cos_t, sin_a, sin_b)
        tile2 = lambda p: jnp.concatenate([p, p], axis=1)
        w1k = _bf(_blockdiag2(cmp_w1_k[l].reshape(CMP_BLK, NSA_DH, CMP_HID)))
        w1v = _bf(_blockdiag2(cmp_w1_v[l].reshape(CMP_BLK, NSA_DH, CMP_HID)))
        kcs = _compress(kc_rot, vc_raw, tile2(cmp_pos_k[l]), tile2(cmp_pos_v[l]),
                        w1k, _bf(_blockdiag2(cmp_w2_k[l])), w1v, _bf(_blockdiag2(cmp_w2_v[l])))
        y_nsa = _nsa_attn(q_rot, kcs, kvs, z3, ovl)

        wn = _pad_heads(w_br_nsa[l].T, NSA_HEADS, NSA_DH).T
        x2 = _merge(x2, y_hg.reshape(n, HG_DIM), y_gla.reshape(n, GLA_VDIM),
                    y_nsa.reshape(n, NSA_HEADS * LANES), z2,
                    _bf(w_br_hg[l]), _bf(w_br_gla[l]), _bf(wn), _bf(w_out[l]))

        x2 = _ffn(x2, norm2_g[l][None, :], _bf(w_ffn_gate[l]), _bf(w_ffn_up[l]), _bf(w_ffn_down[l]),
                  final_norm_g[None, :], final_norm=(l == depth - 1))
    return x2.reshape(bsz, t, d)
```

```python
import functools

import jax
import jax.numpy as jnp
from jax import lax
from jax.experimental import pallas as pl
from jax.experimental.pallas import tpu as pltpu

EPS = 1e-6
D_MODEL = 1024
LANES = 128

HG_HEADS = 4
HG_DIM = 512
GLA_HEADS = 4
GLA_DK = 64
GLA_KDIM = 256
GLA_VDIM = 512
GLA_RANK = 16
GLA_TAU = 16.0
LA_CHUNK = 64
LA_SUB = 16

NSA_HEADS = 8
NSA_GROUPS = 2
NSA_REP = 4
NSA_DH = 64
CMP_STRIDE = 16
CMP_BLK = 32
CMP_HID = 128
SLC_BLK = 64
N_SEL = 16
WIN = 512
Q_BLOCK = 128
ROPE_THETA = 500000.0
ROT_DIM = 16
FF_DIM = 2816

C_HQ, C_HF, C_HI, C_HOG = 0, 512, 1024, 1536
C_GQ, C_GK, C_GV, C_GOG = 2048, 2560, 3072, 3584
C_NQ = 4096
C_NKC, C_NVC, C_NKS, C_NVS, C_NKW, C_NVW = 5120, 5248, 5376, 5504, 5632, 5760
C_NGATE, C_GLR = 5888, 6016
C_MG = 6144
Z_COLS = 9216

VMEM_LIMIT = 56 * 1024 * 1024
LOG2E = 1.4426950408889634

_NT = (((1,), (1,)), ((), ()))
_TN = (((0,), (0,)), ((), ()))


def _bf(x):
    return x.astype(jnp.bfloat16)


def _dot(a, b):
    return jnp.dot(a, b, preferred_element_type=jnp.float32)


def _dot_nt(a, b):
    return lax.dot_general(a, b, _NT, preferred_element_type=jnp.float32)


def _dot_tn(a, b):
    return lax.dot_general(a, b, _TN, preferred_element_type=jnp.float32)


def _f32(x):
    return x.astype(jnp.float32)


def _sigmoid(x):
    return 1.0 / (1.0 + jnp.exp(-x))


def _softplus_neg_abs(x):
    return jnp.log(1.0 + jnp.exp(-jnp.abs(x)))


def _log_sigmoid(x):
    return jnp.minimum(x, 0.0) - _softplus_neg_abs(x)


def _inproj_kernel(x_ref, g_ref, w_ref, z_ref, h_scr):
    @pl.when(pl.program_id(1) == 0)
    def _():
        x = x_ref[...]
        y = x * lax.rsqrt(jnp.mean(x * x, axis=-1, keepdims=True) + EPS)
        h_scr[...] = _bf(y * g_ref[...])

    z_ref[...] = _dot(h_scr[...], w_ref[...]).astype(z_ref.dtype)


def _inproj(x2, g, w, tm=1024, tn=2304):
    n, d = x2.shape
    cols = w.shape[1]
    return pl.pallas_call(
        _inproj_kernel,
        grid=(n // tm, cols // tn),
        in_specs=[
            pl.BlockSpec((tm, d), lambda i, j: (i, 0)),
            pl.BlockSpec((1, d), lambda i, j: (0, 0)),
            pl.BlockSpec((d, tn), lambda i, j: (0, j)),
        ],
        out_specs=pl.BlockSpec((tm, tn), lambda i, j: (i, j)),
        out_shape=jax.ShapeDtypeStruct((n, cols), jnp.bfloat16),
        scratch_shapes=[pltpu.VMEM((tm, d), jnp.bfloat16)],
        compiler_params=pltpu.CompilerParams(
            dimension_semantics=("parallel", "arbitrary"), vmem_limit_bytes=VMEM_LIMIT),
        name="inproj",
    )(x2, g, w)


def _chunk_cumsum(g, tb):
    pos = lax.broadcasted_iota(jnp.int32, (tb, 1), 0) % LA_CHUNK
    b = g
    shift = 1
    while shift < LA_CHUNK:
        b = b + jnp.where(pos >= shift, pltpu.roll(b, shift, axis=0), 0.0)
        shift *= 2
    return b


def _la_core(q, k, v, g, st_ref, tb):
    c, r = LA_CHUNK, LA_SUB
    n = tb // c
    nsub = c // r
    dk = q.shape[-1]
    b = _chunk_cumsum(g, tb)
    b3 = b.reshape(n, c, dk)
    q3 = q.reshape(n, c, dk)
    k3 = k.reshape(n, c, dk)
    v3 = _bf(v).reshape(n, c, dk)
    blast = b3[:, c - 1:c, :]
    q_state = _bf(q3 * jnp.exp(b3))
    k_state = _bf(k3 * jnp.exp(blast - b3))

    b4 = b.reshape(n * nsub, r, dk)
    ref = b4[:, 0:1, :]
    q_loc = _bf(q.reshape(n * nsub, r, dk) * jnp.exp(b4 - ref))
    s_idx = lax.broadcasted_iota(jnp.int32, (1, nsub, c, 1), 2)
    i_idx = lax.broadcasted_iota(jnp.int32, (1, nsub, c, 1), 1)
    e = jnp.where(s_idx < r * (i_idx + 1), ref.reshape(n, nsub, 1, dk) - b3[:, None, :, :], 0.0)
    k_loc = _bf(k3[:, None, :, :] * jnp.exp(e)).reshape(n * nsub, c, dk)
    a = jnp.einsum("utd,usd->uts", q_loc, k_loc, preferred_element_type=jnp.float32)
    a = a.reshape(n, c, c)
    t_i = lax.broadcasted_iota(jnp.int32, (1, c, c), 1)
    s_i = lax.broadcasted_iota(jnp.int32, (1, c, c), 2)
    a = jnp.where(s_i <= t_i, a, 0.0)
    o_intra = jnp.einsum("nts,nsv->ntv", _bf(a), v3, preferred_element_type=jnp.float32)

    st = st_ref[...]
    outs = []
    for ci in range(n):
        outs.append(_dot_nt(q_state[ci], _bf(st)) + o_intra[ci])
        st = st * jnp.exp(blast[ci]) + _dot_tn(v3[ci], k_state[ci])
    st_ref[...] = st
    return jnp.concatenate(outs, axis=0)


def _la_finish(o, og, gain, y_ref):
    o = o * lax.rsqrt(jnp.mean(o * o, axis=-1, keepdims=True) + EPS)
    y_ref[0] = _bf(o * gain * (og * _sigmoid(og)))


def _hgrn2_kernel(q_ref, f_ref, i_ref, og_ref, par_ref, y_ref, st_ref, *, tb):
    @pl.when(pl.program_id(2) == 0)
    def _():
        st_ref[...] = jnp.zeros_like(st_ref)

    log_lb = par_ref[0, 0:1, :]
    log1m_lb = par_ref[0, 1:2, :]
    one_m_lb = par_ref[0, 2:3, :]
    gain = par_ref[0, 3:4, :]
    hq = _f32(q_ref[0])
    hf = _f32(f_ref[0])
    q = hq * _sigmoid(hq)
    e = jnp.exp(-jnp.abs(hf))
    cterm = log1m_lb + (jnp.minimum(hf, 0.0) - jnp.log(1.0 + e))
    log_f = jnp.maximum(log_lb, cterm) + _softplus_neg_abs(log_lb - cterm)
    k = one_m_lb * (jnp.where(hf >= 0.0, e, 1.0) / (1.0 + e))
    o = _la_core(q, k, _f32(i_ref[0]), log_f, st_ref, tb)
    _la_finish(o, _f32(og_ref[0]), gain, y_ref)


def _gla_kernel(q_ref, k_ref, v_ref, og_ref, lr_ref, w2_ref, par_ref, y_ref, st_ref, *, tb):
    @pl.when(pl.program_id(2) == 0)
    def _():
        st_ref[...] = jnp.zeros_like(st_ref)

    bias = par_ref[0, 0:1, :]
    gain = par_ref[0, 1:2, :]
    zz = _dot(_bf(lr_ref[0]), w2_ref[0]) + bias
    log_a = _log_sigmoid(zz) * (1.0 / GLA_TAU)
    q = _f32(q_ref[0]) * (GLA_DK ** -0.5)
    o = _la_core(q, _f32(k_ref[0]), _f32(v_ref[0]), log_a, st_ref, tb)
    _la_finish(o, _f32(og_ref[0]), gain, y_ref)


def _zspec(tb, col0):
    base = col0 // LANES
    return pl.BlockSpec((1, tb, LANES), lambda b, h, t: (b, t, base + h))


def _la_params():
    return pltpu.CompilerParams(
        dimension_semantics=("parallel", "parallel", "arbitrary"), vmem_limit_bytes=VMEM_LIMIT)


def _hgrn2(z3, par, tb=2048):
    bsz, t, _ = z3.shape
    return pl.pallas_call(
        functools.partial(_hgrn2_kernel, tb=tb),
        grid=(bsz, HG_HEADS, t // tb),
        in_specs=[_zspec(tb, C_HQ), _zspec(tb, C_HF), _zspec(tb, C_HI), _zspec(tb, C_HOG),
                  pl.BlockSpec((1, 8, LANES), lambda b, h, t: (h, 0, 0))],
        out_specs=pl.BlockSpec((1, tb, LANES), lambda b, h, t: (b, t, h)),
        out_shape=jax.ShapeDtypeStruct((bsz, t, HG_DIM), jnp.bfloat16),
        scratch_shapes=[pltpu.VMEM((LANES, LANES), jnp.float32)],
        compiler_params=_la_params(),
        name="hgrn2",
    )(z3, z3, z3, z3, par)


def _gla(z3, w2p, par, tb=2048):
    bsz, t, _ = z3.shape
    lr_blk = C_GLR // LANES
    return pl.pallas_call(
        functools.partial(_gla_kernel, tb=tb),
        grid=(bsz, GLA_HEADS, t // tb),
        in_specs=[_zspec(tb, C_GQ), _zspec(tb, C_GK), _zspec(tb, C_GV), _zspec(tb, C_GOG),
                  pl.BlockSpec((1, tb, LANES), lambda b, h, t: (b, t, lr_blk)),
                  pl.BlockSpec((1, LANES, LANES), lambda b, h, t: (h, 0, 0)),
                  pl.BlockSpec((1, 8, LANES), lambda b, h, t: (h, 0, 0))],
        out_specs=pl.BlockSpec((1, tb, LANES), lambda b, h, t: (b, t, h)),
        out_shape=jax.ShapeDtypeStruct((bsz, t, GLA_VDIM), jnp.bfloat16),
        scratch_shapes=[pltpu.VMEM((LANES, LANES), jnp.float32)],
        compiler_params=_la_params(),
        name="gla",
    )(z3, z3, z3, z3, z3, w2p, par)


def _rope(x, cos_t, sin_a, sin_b):
    half = ROT_DIM // 2
    slabs = []
    for i in range(x.shape[-1] // LANES):
        xs = x[:, i * LANES:(i + 1) * LANES]
        slabs.append(xs * cos_t + pltpu.roll(xs, LANES - half, axis=1) * sin_a
                     + pltpu.roll(xs, half, axis=1) * sin_b)
    return slabs[0] if len(slabs) == 1 else jnp.concatenate(slabs, axis=1)


def _per_group(x, filler):
    swapped = pltpu.roll(x, NSA_DH, axis=1)
    if filler is None:
        return _bf(x), _bf(swapped)
    low = lax.broadcasted_iota(jnp.int32, x.shape, 1) < NSA_DH
    return _bf(jnp.where(low, x, filler)), _bf(jnp.where(low, swapped, filler))


def _nsa_prep_kernel(q_ref, kc_ref, vc_ref, ks_ref, vs_ref, kw_ref, vw_ref, cos_ref, sa_ref, sb_ref,
                     qo_ref, kco_ref, vco_ref, ks0_ref, ks1_ref, vs0_ref, vs1_ref, kw0_ref, kw1_ref,
                     vw0_ref, vw1_ref, *, tt):
    ct, sa, sb = cos_ref[...], sa_ref[...], sb_ref[...]
    qo_ref[0] = _bf(_rope(_f32(q_ref[0]), ct, sa, sb) * (NSA_DH ** -0.5 * LOG2E))
    kco_ref[0] = _rope(_f32(kc_ref[0]), ct, sa, sb)
    vco_ref[0] = _f32(vc_ref[0])
    tok = pl.program_id(1) * tt + lax.broadcasted_iota(jnp.int32, (tt, LANES), 0)
    lane = lax.broadcasted_iota(jnp.int32, (tt, LANES), 1)
    onehot = jnp.where(lane - NSA_DH == tok // SLC_BLK, 1.0, 0.0)
    ks0_ref[0], ks1_ref[0] = _per_group(_rope(_f32(ks_ref[0]), ct, sa, sb), onehot)
    kw0_ref[0], kw1_ref[0] = _per_group(_rope(_f32(kw_ref[0]), ct, sa, sb), None)
    vs0_ref[0], vs1_ref[0] = _per_group(_f32(vs_ref[0]), 1.0)
    vw0_ref[0], vw1_ref[0] = _per_group(_f32(vw_ref[0]), 1.0)


def _nsa_prep(z3, cos_t, sin_a, sin_b, tt=512):
    bsz, t, _ = z3.shape
    qw = NSA_HEADS * LANES

    def zs(col0, width):
        blk = col0 // width
        return pl.BlockSpec((1, tt, width), lambda b, i: (b, i, blk))

    def os_(width):
        return pl.BlockSpec((1, tt, width), lambda b, i: (b, i, 0))

    tab = pl.BlockSpec((tt, LANES), lambda b, i: (i, 0))
    kv_bf = jax.ShapeDtypeStruct((bsz, t, LANES), jnp.bfloat16)
    return pl.pallas_call(
        functools.partial(_nsa_prep_kernel, tt=tt),
        grid=(bsz, t // tt),
        in_specs=[zs(C_NQ, qw), zs(C_NKC, LANES), zs(C_NVC, LANES), zs(C_NKS, LANES),
                  zs(C_NVS, LANES), zs(C_NKW, LANES), zs(C_NVW, LANES), tab, tab, tab],
        out_specs=[os_(qw), os_(LANES), os_(LANES)] + [os_(LANES)] * 8,
        out_shape=[jax.ShapeDtypeStruct((bsz, t, qw), jnp.bfloat16),
                   jax.ShapeDtypeStruct((bsz, t, LANES), jnp.float32),
                   jax.ShapeDtypeStruct((bsz, t, LANES), jnp.float32)] + [kv_bf] * 8,
        compiler_params=pltpu.CompilerParams(
            dimension_semantics=("parallel", "parallel"), vmem_limit_bytes=VMEM_LIMIT),
        name="nsa_prep",
    )(z3, z3, z3, z3, z3, z3, z3, cos_t, sin_a, sin_b)


def _compress_one(src_ref, pos_ref, w1_ref, w2_ref, out0_ref, out1_ref, ncmp):
    half = CMP_BLK // 2
    u = jnp.zeros((ncmp, 2 * CMP_HID), jnp.float32)
    v = jnp.zeros((ncmp, 2 * CMP_HID), jnp.float32)
    for j in range(half):
        rows = src_ref[0, pl.ds(j, ncmp, stride=CMP_STRIDE), :]
        u = u + _dot(_bf(rows + pos_ref[j:j + 1, :]), w1_ref[j])
        v = v + _dot(_bf(rows + pos_ref[half + j:half + j + 1, :]), w1_ref[half + j])
    hid = u + pltpu.roll(v, ncmp - 1, axis=0)
    out0_ref[0], out1_ref[0] = _per_group(_dot(_bf(jax.nn.gelu(hid)), w2_ref[...]), None)


def _compress_kernel(k_ref, v_ref, pk_ref, pv_ref, w1k_ref, w2k_ref, w1v_ref, w2v_ref,
                     kc0_ref, kc1_ref, vc0_ref, vc1_ref, *, ncmp):
    _compress_one(k_ref, pk_ref, w1k_ref, w2k_ref, kc0_ref, kc1_ref, ncmp)
    _compress_one(v_ref, pv_ref, w1v_ref, w2v_ref, vc0_ref, vc1_ref, ncmp)


def _compress(kc_rot, vc_raw, pos_k, pos_v, w1k, w2k, w1v, w2v):
    bsz, t, _ = kc_rot.shape
    ncmp = t // CMP_STRIDE
    full = lambda shape: pl.BlockSpec(shape, lambda b: (0,) * len(shape))
    out = jax.ShapeDtypeStruct((bsz, ncmp, LANES), jnp.bfloat16)
    return pl.pallas_call(
        functools.partial(_compress_kernel, ncmp=ncmp),
        grid=(bsz,),
        in_specs=[pl.BlockSpec((1, t, LANES), lambda b: (b, 0, 0)),
                  pl.BlockSpec((1, t, LANES), lambda b: (b, 0, 0)),
                  full(pos_k.shape), full(pos_v.shape), full(w1k.shape), full(w2k.shape),
                  full(w1v.shape), full(w2v.shape)],
        out_specs=[pl.BlockSpec((1, ncmp, LANES), lambda b: (b, 0, 0))] * 4,
        out_shape=[out] * 4,
        compiler_params=pltpu.CompilerParams(
            dimension_semantics=("parallel",), vmem_limit_bytes=VMEM_LIMIT),
        name="nsa_compress",
    )(kc_rot, vc_raw, pos_k, pos_v, w1k, w2k, w1v, w2v)


NEG = -1e30
SEL_KB = 256


def _select_blocks(imp, t_col, nslc):
    qb = imp.shape[0]
    blk = lax.broadcasted_iota(jnp.int32, (qb, LANES), 1) - NSA_DH
    cur = t_col // SLC_BLK
    valid = (blk >= 0) & (blk <= cur)
    forced = (blk == 0) | (blk == cur) | (blk == cur - 1)
    score = jnp.where(valid & forced, 1e9, jnp.where(valid, imp, -1e9))
    sc_t = score.T
    sub = 8
    nslab = (LANES - NSA_DH) // sub
    slabs = [sc_t[NSA_DH + sub * k:NSA_DH + sub * (k + 1), :] for k in range(nslab)]
    jj = lax.broadcasted_iota(jnp.int32, (sub, qb), 0)
    ranks = [jnp.zeros((sub, qb), jnp.float32) for _ in range(nslab)]
    for i in range(nslc):
        row = sc_t[NSA_DH + i:NSA_DH + i + 1, :]
        for k in range(nslab):
            if sub * k > i:
                ahead = row >= slabs[k]
            elif sub * (k + 1) <= i:
                ahead = row > slabs[k]
            else:
                ahead = (row > slabs[k]) | ((row == slabs[k]) & (jj + sub * k > i))
            ranks[k] = ranks[k] + jnp.where(ahead, 1.0, 0.0)
    sel_t = jnp.concatenate(
        [jnp.zeros((NSA_DH, qb), jnp.float32)]
        + [jnp.where(r < float(N_SEL), 1.0, 0.0) for r in ranks], axis=0)
    return valid & (sel_t.T > 0.5)


def _nsa_attn_kernel(q_ref, kc0_ref, kc1_ref, vc0_ref, vc1_ref, ks0_ref, ks1_ref, vs0_ref, vs1_ref,
                     kw0_ref, kw1_ref, vw0_ref, vw1_ref, gate_ref, ovl_ref, y_ref,
                     qa_scr, m_scr, acc_scr, sa_scr, sb_scr, mxa_scr, mxb_scr, *, seq):
    c = pl.program_id(1)
    qb = Q_BLOCK
    rep = NSA_REP
    rows = rep * qb
    t0 = c * qb
    t_col = t0 + lax.broadcasted_iota(jnp.int32, (qb, 1), 0)
    gates = _sigmoid(_f32(gate_ref[0]))
    ncmp = kc0_ref.shape[1]
    lane_low = lax.broadcasted_iota(jnp.int32, (qb, LANES), 1) < NSA_DH
    n_idx = lax.broadcasted_iota(jnp.int32, (1, ncmp), 1)
    bias_c = jnp.where((n_idx * CMP_STRIDE + (CMP_BLK - 1)) <= t_col, 0.0, NEG)
    has_cmp = jnp.where(t_col >= CMP_BLK - 1, 1.0, 0.0)
    w_start = pl.multiple_of(jnp.maximum(t0 - WIN, 0), qb)
    wk = WIN + qb
    kp = w_start + lax.broadcasted_iota(jnp.int32, (1, wk), 1)
    bias_w = jnp.where((kp <= t_col) & (kp > t_col - WIN), 0.0, NEG)
    n_full = t0 // SEL_KB

    def with_bias(s, bias):
        return (s.reshape(rep, qb, s.shape[-1]) + bias[None]).reshape(s.shape)

    groups = ((kc0_ref, vc0_ref, ks0_ref, vs0_ref, kw0_ref, vw0_ref),
              (kc1_ref, vc1_ref, ks1_ref, vs1_ref, kw1_ref, vw1_ref))
    o_cs, acc_ws = [], []
    for g, (kc_ref, vc_ref, ks_ref, vs_ref, kw_ref, vw_ref) in enumerate(groups):
        q_heads = [q_ref[0, :, (g * rep + r) * LANES:(g * rep + r + 1) * LANES] for r in range(rep)]
        qg = jnp.concatenate(q_heads, axis=0)

        s = with_bias(_dot_nt(qg, kc_ref[0]), bias_c)
        m = jnp.max(s, axis=-1, keepdims=True)
        p = (jnp.exp2(s - m).reshape(rep, qb, ncmp) * has_cmp[None]).reshape(rows, ncmp)
        l = jnp.sum(p, axis=-1, keepdims=True)
        p_c = p / jnp.where(l > 0.0, l, 1.0)
        o_cs.append(_dot(_bf(p_c), vc_ref[0]))
        p_sum = p_c[0:qb] + p_c[qb:2 * qb] + p_c[2 * qb:3 * qb] + p_c[3 * qb:4 * qb]
        imp = jnp.dot(p_sum, ovl_ref[...], preferred_element_type=jnp.float32,
                      precision=lax.Precision.HIGHEST)
        sel = _select_blocks(imp, t_col, seq // SLC_BLK)
        sel_bias = _bf(jnp.where(sel, 0.0, NEG))
        for r, qh in enumerate(q_heads):
            qa_scr[g, r * qb:(r + 1) * qb, :] = jnp.where(lane_low, qh, sel_bias)
        m_scr[g] = jnp.full((rows, LANES), NEG, jnp.float32)
        acc_scr[g] = jnp.zeros((rows, LANES), jnp.float32)

    def chunk_start(j):
        return pl.multiple_of(jnp.minimum(j * SEL_KB, seq - SEL_KB), SEL_KB)

    def produce(j, dst_scr, dst_max_scr):
        k0 = chunk_start(j)
        for g in range(NSA_GROUPS):
            s = _dot_nt(qa_scr[g], groups[g][2][0, pl.ds(k0, SEL_KB), :])
            dst_scr[g] = s
            dst_max_scr[g] = jnp.broadcast_to(jnp.max(s, axis=-1, keepdims=True), (rows, LANES))

    def consume(j, src_scr, src_max_scr, causal):
        k0 = chunk_start(j)
        if causal:
            tok = j * SEL_KB + lax.broadcasted_iota(jnp.int32, (1, SEL_KB), 1)
            bias = jnp.where(tok <= t_col, 0.0, NEG)
        for g in range(NSA_GROUPS):
            vs_ref = groups[g][3]
            m = m_scr[g]
            if causal:
                s = with_bias(src_scr[g], bias)
                m_new = jnp.maximum(m, jnp.max(s, axis=-1, keepdims=True))
            else:
                s = src_scr[g]
                m_new = jnp.maximum(m, src_max_scr[g])
            p = jnp.exp2(s - jnp.concatenate([m_new] * (SEL_KB // LANES), axis=1))
            pv = _dot(_bf(p), vs_ref[0, pl.ds(k0, SEL_KB), :])
            acc_scr[g] = jnp.exp2(m - m_new) * acc_scr[g] + pv
            m_scr[g] = m_new

    def loop_body(i, carry):
        produce(2 * i + 1, sb_scr, mxb_scr)
        consume(2 * i, sa_scr, mxa_scr, False)
        produce(2 * i + 2, sa_scr, mxa_scr)
        consume(2 * i + 1, sb_scr, mxb_scr, False)
        return carry

    n_pairs = n_full // 2
    produce(0, sa_scr, mxa_scr)
    lax.fori_loop(0, n_pairs, loop_body, 0)
    produce(2 * n_pairs + 1, sb_scr, mxb_scr)
    consume(2 * n_pairs, sa_scr, mxa_scr, True)

    consume(2 * n_pairs + 1, sb_scr, mxb_scr, True)
    carry = (None, acc_scr[0], None, acc_scr[1])

    for g in range(NSA_GROUPS):
        kw_ref, vw_ref = groups[g][4], groups[g][5]
        qg = jnp.concatenate(
            [q_ref[0, :, (g * rep + r) * LANES:(g * rep + r + 1) * LANES] for r in range(rep)], axis=0)
        s = with_bias(_dot_nt(qg, kw_ref[0, pl.ds(w_start, wk), :]), bias_w)
        m = jnp.max(s, axis=-1, keepdims=True)
        acc_ws.append(_dot(_bf(jnp.exp2(s - m)), vw_ref[0, pl.ds(w_start, wk), :]))

    def normalized(acc):
        return acc / pltpu.roll(acc, NSA_DH, axis=1)

    for g in range(NSA_GROUPS):
        o_c, o_s, o_w = o_cs[g], normalized(carry[2 * g + 1]), normalized(acc_ws[g])
        for r in range(rep):
            hd = g * rep + r
            sl = slice(r * qb, (r + 1) * qb)
            out = (o_c[sl] * gates[:, 3 * hd:3 * hd + 1] + o_s[sl] * gates[:, 3 * hd + 1:3 * hd + 2]
                   + o_w[sl] * gates[:, 3 * hd + 2:3 * hd + 3])
            y_ref[0, :, hd * LANES:(hd + 1) * LANES] = _bf(jnp.where(lane_low, out, 0.0))


def _nsa_attn(q_rot, kcs, kvs, z3, ovl):
    bsz, t, qw = q_rot.shape
    ncmp = kcs[0].shape[1]
    assert t % SEL_KB == 0 and t >= WIN + Q_BLOCK and t // SLC_BLK <= LANES - NSA_DH
    gblk = C_NGATE // LANES
    per_b = lambda rows: pl.BlockSpec((1, rows, LANES), lambda b, c: (b, 0, 0))
    return pl.pallas_call(
        functools.partial(_nsa_attn_kernel, seq=t),
        grid=(bsz, t // Q_BLOCK),
        in_specs=[pl.BlockSpec((1, Q_BLOCK, qw), lambda b, c: (b, c, 0))]
        + [per_b(ncmp)] * 4 + [per_b(t)] * 8
        + [pl.BlockSpec((1, Q_BLOCK, LANES), lambda b, c: (b, c, gblk)),
           pl.BlockSpec(ovl.shape, lambda b, c: (0, 0))],
        out_specs=pl.BlockSpec((1, Q_BLOCK, qw), lambda b, c: (b, c, 0)),
        out_shape=jax.ShapeDtypeStruct((bsz, t, qw), jnp.bfloat16),
        scratch_shapes=[pltpu.VMEM((NSA_GROUPS, NSA_REP * Q_BLOCK, LANES), jnp.bfloat16),
                        pltpu.VMEM((NSA_GROUPS, NSA_REP * Q_BLOCK, LANES), jnp.float32),
                        pltpu.VMEM((NSA_GROUPS, NSA_REP * Q_BLOCK, LANES), jnp.float32),
                        pltpu.VMEM((NSA_GROUPS, NSA_REP * Q_BLOCK, SEL_KB), jnp.float32),
                        pltpu.VMEM((NSA_GROUPS, NSA_REP * Q_BLOCK, SEL_KB), jnp.float32),
                        pltpu.VMEM((NSA_GROUPS, NSA_REP * Q_BLOCK, LANES), jnp.float32),
                        pltpu.VMEM((NSA_GROUPS, NSA_REP * Q_BLOCK, LANES), jnp.float32)],
        compiler_params=pltpu.CompilerParams(
            dimension_semantics=("parallel", "arbitrary"), vmem_limit_bytes=VMEM_LIMIT),
        name="nsa_attn",
    )(q_rot, *kcs, *kvs, z3, ovl)


def _merge_kernel(x_ref, yh_ref, yg_ref, yn_ref, mg_ref, wh_ref, wg_ref, wn_ref, wo_ref, o_ref):
    d = D_MODEL
    m = _sigmoid(_f32(mg_ref[:, 0:d])) * _dot(yh_ref[...], wh_ref[...])
    m = m + _sigmoid(_f32(mg_ref[:, d:2 * d])) * _dot(yg_ref[...], wg_ref[...])
    m = m + _sigmoid(_f32(mg_ref[:, 2 * d:3 * d])) * _dot(yn_ref[...], wn_ref[...])
    o_ref[...] = x_ref[...] + _dot(_bf(m), wo_ref[...])


def _merge(x2, yh, yg, yn, z2, wh, wg, wn, wo, tm=512):
    n, d = x2.shape
    mgblk = C_MG // (3 * d)
    row = lambda w: pl.BlockSpec((tm, w), lambda i: (i, 0))
    full = lambda a: pl.BlockSpec(a.shape, lambda i: (0, 0))
    return pl.pallas_call(
        _merge_kernel,
        grid=(n // tm,),
        in_specs=[row(d), row(yh.shape[1]), row(yg.shape[1]), row(yn.shape[1]),
                  pl.BlockSpec((tm, 3 * d), lambda i: (i, mgblk)),
                  full(wh), full(wg), full(wn), full(wo)],
        out_specs=row(d),
        out_shape=jax.ShapeDtypeStruct((n, d), jnp.float32),
        compiler_params=pltpu.CompilerParams(
            dimension_semantics=("parallel",), vmem_limit_bytes=VMEM_LIMIT),
        name="merge",
    )(x2, yh, yg, yn, z2, wh, wg, wn, wo)


def _ffn_kernel(x_ref, g_ref, wg_ref, wu_ref, wd_ref, fg_ref, o_ref, h_scr, acc_scr, *, final_norm):
    j = pl.program_id(1)

    @pl.when(j == 0)
    def _():
        x = x_ref[...]
        y = x * lax.rsqrt(jnp.mean(x * x, axis=-1, keepdims=True) + EPS)
        h_scr[...] = _bf(y * g_ref[...])
        acc_scr[...] = jnp.zeros_like(acc_scr)

    h = h_scr[...]
    a = _dot(h, wg_ref[...])
    u = _dot(h, wu_ref[...])
    acc_scr[...] += _dot(_bf(a * _sigmoid(a) * u), wd_ref[...])

    @pl.when(j == pl.num_programs(1) - 1)
    def _():
        out = x_ref[...] + acc_scr[...]
        if final_norm:
            out = out * lax.rsqrt(jnp.mean(out * out, axis=-1, keepdims=True) + EPS) * fg_ref[...]
        o_ref[...] = out


def _ffn(x2, g, wg, wu, wd, fg, final_norm, tm=1024, tf=256):
    n, d = x2.shape
    ff = wg.shape[1]
    return pl.pallas_call(
        functools.partial(_ffn_kernel, final_norm=final_norm),
        grid=(n // tm, ff // tf),
        in_specs=[pl.BlockSpec((tm, d), lambda i, j: (i, 0)),
                  pl.BlockSpec((1, d), lambda i, j: (0, 0)),
                  pl.BlockSpec((d, tf), lambda i, j: (0, j)),
                  pl.BlockSpec((d, tf), lambda i, j: (0, j)),
                  pl.BlockSpec((tf, d), lambda i, j: (j, 0)),
                  pl.BlockSpec((1, d), lambda i, j: (0, 0))],
        out_specs=pl.BlockSpec((tm, d), lambda i, j: (i, 0)),
        out_shape=jax.ShapeDtypeStruct((n, d), jnp.float32),
        scratch_shapes=[pltpu.VMEM((tm, d), jnp.bfloat16), pltpu.VMEM((tm, d), jnp.float32)],
        compiler_params=pltpu.CompilerParams(
            dimension_semantics=("parallel", "arbitrary"), vmem_limit_bytes=VMEM_LIMIT),
        name="ffn",
    )(x2, g, wg, wu, wd, fg)


def _pad_heads(w, heads, width):
    lead = w.shape[:-1]
    w = w.reshape(lead + (heads, width))
    w = jnp.pad(w, [(0, 0)] * len(lead) + [(0, 0), (0, LANES - width)])
    return w.reshape(lead + (heads * LANES,))


def _layout_w_in(w):
    d = w.shape[0]
    o = 0
    parts = {}
    for name, n in (("hq", 512), ("hf", 512), ("hi", 512), ("hog", 512), ("gq", 256), ("gk", 256),
                    ("gv", 512), ("glr", 16), ("gog", 512), ("nq", 512), ("nkc", 128), ("nvc", 128),
                    ("nks", 128), ("nvs", 128), ("nkw", 128), ("nvw", 128), ("ngate", 24), ("mg", 3072)):
        parts[name] = w[:, o:o + n]
        o += n
    nq = _pad_heads(parts["nq"], NSA_HEADS, NSA_DH)
    padc = lambda a: jnp.pad(a, ((0, 0), (0, LANES - a.shape[1])))
    cols = [parts["hq"], parts["hf"], parts["hi"], parts["hog"],
            _pad_heads(parts["gq"], GLA_HEADS, GLA_DK), _pad_heads(parts["gk"], GLA_HEADS, GLA_DK),
            parts["gv"], parts["gog"], nq,
            parts["nkc"], parts["nvc"], parts["nks"], parts["nvs"], parts["nkw"], parts["nvw"],
            padc(parts["ngate"]), padc(parts["glr"]), parts["mg"]]
    out = jnp.concatenate(cols, axis=1)
    assert out.shape[1] == Z_COLS
    return _bf(out)


def _blockdiag2(w):
    z = jnp.zeros_like(w)
    top = jnp.concatenate([w, z], axis=-1)
    bot = jnp.concatenate([z, w], axis=-1)
    return jnp.concatenate([top, bot], axis=-2)


def _rope_tables(t):
    pos = jnp.arange(t, dtype=jnp.float32)
    inv_freq = ROPE_THETA ** (-jnp.arange(0, ROT_DIM, 2, dtype=jnp.float32) / ROT_DIM)
    ang = pos[:, None] * inv_freq[None, :]
    cos, sin = jnp.cos(ang), jnp.sin(ang)
    half = ROT_DIM // 2
    lane = jnp.arange(LANES) % NSA_DH
    first = lane < half
    second = (lane >= half) & (lane < ROT_DIM)
    idx = jnp.where(first, lane, jnp.where(second, lane - half, 0))
    cos_t = jnp.where(first | second, cos[:, idx], 1.0)
    sin_a = jnp.where(first, -sin[:, idx], 0.0)
    sin_b = jnp.where(second, sin[:, idx], 0.0)
    return cos_t, sin_a, sin_b


def kernel(x, norm1_g, w_in, hg_lb, hg_norm_g, gla_w2, gla_b, gla_norm_g, cmp_pos_k, cmp_pos_v,
           cmp_w1_k, cmp_w2_k, cmp_w1_v, cmp_w2_v, w_br_hg, w_br_gla, w_br_nsa, w_out,
           norm2_g, w_ffn_gate, w_ffn_up, w_ffn_down, final_norm_g):
    bsz, t, d = x.shape
    depth = w_in.shape[0]
    n = bsz * t
    f32 = jnp.float32

    cos_t, sin_a, sin_b = _rope_tables(t)
    lbs = jnp.cumsum(jax.nn.softmax(hg_lb.astype(f32), axis=0), axis=0)
    lbs = lbs - lbs[0]

    nslc = t // SLC_BLK
    ncmp_pad = t // CMP_STRIDE
    cmp_start = jnp.arange(ncmp_pad) * CMP_STRIDE
    blk = jnp.arange(LANES) - NSA_DH
    ovl = ((cmp_start[:, None] < (blk[None, :] + 1) * SLC_BLK)
           & (cmp_start[:, None] + CMP_BLK - 1 >= blk[None, :] * SLC_BLK)
           & (blk[None, :] >= 0) & (blk[None, :] < nslc)
           & (jnp.arange(ncmp_pad)[:, None] < ncmp_pad - 1)).astype(f32)

    x2 = x.reshape(n, d)
    for l in range(depth):
        z2 = _inproj(x2, norm1_g[l][None, :], _layout_w_in(w_in[l]))
        z3 = z2.reshape(bsz, t, Z_COLS)

        lb = lbs[l].reshape(HG_HEADS, LANES)
        zeros = jnp.zeros_like(lb)
        hg_par = jnp.stack([jnp.log(lb), jnp.log1p(-lb), 1.0 - lb,
                            hg_norm_g[l].reshape(HG_HEADS, LANES), zeros, zeros, zeros, zeros], axis=1)
        y_hg = _hgrn2(z3, hg_par)

        w2p = jnp.pad(_pad_heads(gla_w2[l], GLA_HEADS, GLA_DK), ((0, LANES - GLA_RANK), (0, 0)))
        w2p = _bf(w2p.reshape(LANES, GLA_HEADS, LANES).transpose(1, 0, 2))
        gb = _pad_heads(gla_b[l], GLA_HEADS, GLA_DK).reshape(GLA_HEADS, LANES)
        gla_par = jnp.stack([gb, gla_norm_g[l].reshape(GLA_HEADS, LANES),
                             zeros, zeros, zeros, zeros, zeros, zeros], axis=1)
        y_gla = _gla(z3, w2p, gla_par)

        q_rot, kc_rot, vc_raw, *kvs = _nsa_prep(z3, cos_t, sin_a, sin_b)
        tile2 = lambda p: jnp.concatenate([p, p], axis=1)
        w1k = _bf(_blockdiag2(cmp_w1_k[l].reshape(CMP_BLK, NSA_DH, CMP_HID)))
        w1v = _bf(_blockdiag2(cmp_w1_v[l].reshape(CMP_BLK, NSA_DH, CMP_HID)))
        kcs = _compress(kc_rot, vc_raw, tile2(cmp_pos_k[l]), tile2(cmp_pos_v[l]),
                        w1k, _bf(_blockdiag2(cmp_w2_k[l])), w1v, _bf(_blockdiag2(cmp_w2_v[l])))
        y_nsa = _nsa_attn(q_rot, kcs, kvs, z3, ovl)

        wn = _pad_heads(w_br_nsa[l].T, NSA_HEADS, NSA_DH).T
        x2 = _merge(x2, y_hg.reshape(n, HG_DIM), y_gla.reshape(n, GLA_VDIM),
                    y_nsa.reshape(n, NSA_HEADS * LANES), z2,
                    _bf(w_br_hg[l]), _bf(w_br_gla[l]), _bf(wn), _bf(w_out[l]))

        x2 = _ffn(x2, norm2_g[l][None, :], _bf(w_ffn_gate[l]), _bf(w_ffn_up[l]), _bf(w_ffn_down[l]),
                  final_norm_g[None, :], final_norm=(l == depth - 1))
    return x2.reshape(bsz, t, d)
```

```python
import functools

import jax
import jax.numpy as jnp
from jax import lax
from jax.experimental import pallas as pl
from jax.experimental.pallas import tpu as pltpu

EPS = 1e-6
D_MODEL = 1024
LANES = 128

HG_HEADS = 4
HG_DIM = 512
GLA_HEADS = 4
GLA_DK = 64
GLA_KDIM = 256
GLA_VDIM = 512
GLA_RANK = 16
GLA_TAU = 16.0
LA_CHUNK = 64
LA_SUB = 16

NSA_HEADS = 8
NSA_GROUPS = 2
NSA_REP = 4
NSA_DH = 64
CMP_STRIDE = 16
CMP_BLK = 32
CMP_HID = 128
SLC_BLK = 64
N_SEL = 16
WIN = 512
Q_BLOCK = 128
ROPE_THETA = 500000.0
ROT_DIM = 16
FF_DIM = 2816

C_HQ, C_HF, C_HI, C_HOG = 0, 512, 1024, 1536
C_GQ, C_GK, C_GV, C_GOG = 2048, 2560, 3072, 3584
C_NQ = 4096
C_NKC, C_NVC, C_NKS, C_NVS, C_NKW, C_NVW = 5120, 5248, 5376, 5504, 5632, 5760
C_NGATE, C_GLR = 5888, 6016
C_MG = 6144
Z_COLS = 9216

VMEM_LIMIT = 56 * 1024 * 1024
LOG2E = 1.4426950408889634

_NT = (((1,), (1,)), ((), ()))
_TN = (((0,), (0,)), ((), ()))


def _bf(x):
    return x.astype(jnp.bfloat16)


def _dot(a, b):
    return jnp.dot(a, b, preferred_element_type=jnp.float32)


def _dot_nt(a, b):
    return lax.dot_general(a, b, _NT, preferred_element_type=jnp.float32)


def _dot_tn(a, b):
    return lax.dot_general(a, b, _TN, preferred_element_type=jnp.float32)


def _f32(x):
    return x.astype(jnp.float32)


def _sigmoid(x):
    return 1.0 / (1.0 + jnp.exp(-x))


def _softplus_neg_abs(x):
    return jnp.log(1.0 + jnp.exp(-jnp.abs(x)))


def _log_sigmoid(x):
    return jnp.minimum(x, 0.0) - _softplus_neg_abs(x)


def _inproj_kernel(x_ref, g_ref, w_ref, z_ref, h_scr):
    @pl.when(pl.program_id(1) == 0)
    def _():
        x = x_ref[...]
        y = x * lax.rsqrt(jnp.mean(x * x, axis=-1, keepdims=True) + EPS)
        h_scr[...] = _bf(y * g_ref[...])

    z_ref[...] = _dot(h_scr[...], w_ref[...]).astype(z_ref.dtype)


def _inproj(x2, g, w, tm=1024, tn=3072):
    n, d = x2.shape
    cols = w.shape[1]
    return pl.pallas_call(
        _inproj_kernel,
        grid=(n // tm, cols // tn),
        in_specs=[
            pl.BlockSpec((tm, d), lambda i, j: (i, 0)),
            pl.BlockSpec((1, d), lambda i, j: (0, 0)),
            pl.BlockSpec((d, tn), lambda i, j: (0, j)),
        ],
        out_specs=pl.BlockSpec((tm, tn), lambda i, j: (i, j)),
        out_shape=jax.ShapeDtypeStruct((n, cols), jnp.bfloat16),
        scratch_shapes=[pltpu.VMEM((tm, d), jnp.bfloat16)],
        compiler_params=pltpu.CompilerParams(
            dimension_semantics=("parallel", "arbitrary"), vmem_limit_bytes=VMEM_LIMIT),
        name="inproj",
    )(x2, g, w)


def _chunk_cumsum(g, tb):
    pos = lax.broadcasted_iota(jnp.int32, (tb, 1), 0) % LA_CHUNK
    b = g
    shift = 1
    while shift < LA_CHUNK:
        b = b + jnp.where(pos >= shift, pltpu.roll(b, shift, axis=0), 0.0)
        shift *= 2
    return b


def _la_core(q, k, v, g, st_ref, tb):
    c, r = LA_CHUNK, LA_SUB
    n = tb // c
    nsub = c // r
    dk = q.shape[-1]
    b = _chunk_cumsum(g, tb)
    b3 = b.reshape(n, c, dk)
    q3 = q.reshape(n, c, dk)
    k3 = k.reshape(n, c, dk)
    v3 = _bf(v).reshape(n, c, dk)
    blast = b3[:, c - 1:c, :]
    q_state = _bf(q3 * jnp.exp(b3))
    k_state = _bf(k3 * jnp.exp(blast - b3))

    b4 = b.reshape(n * nsub, r, dk)
    ref = b4[:, 0:1, :]
    q_loc = _bf(q.reshape(n * nsub, r, dk) * jnp.exp(b4 - ref))
    s_idx = lax.broadcasted_iota(jnp.int32, (1, nsub, c, 1), 2)
    i_idx = lax.broadcasted_iota(jnp.int32, (1, nsub, c, 1), 1)
    e = jnp.where(s_idx < r * (i_idx + 1), ref.reshape(n, nsub, 1, dk) - b3[:, None, :, :], 0.0)
    k_loc = _bf(k3[:, None, :, :] * jnp.exp(e)).reshape(n * nsub, c, dk)
    a = jnp.einsum("utd,usd->uts", q_loc, k_loc, preferred_element_type=jnp.float32)
    a = a.reshape(n, c, c)
    t_i = lax.broadcasted_iota(jnp.int32, (1, c, c), 1)
    s_i = lax.broadcasted_iota(jnp.int32, (1, c, c), 2)
    a = jnp.where(s_i <= t_i, a, 0.0)
    o_intra = jnp.einsum("nts,nsv->ntv", _bf(a), v3, preferred_element_type=jnp.float32)

    st = st_ref[...]
    outs = []
    for ci in range(n):
        outs.append(_dot_nt(q_state[ci], _bf(st)) + o_intra[ci])
        st = st * jnp.exp(blast[ci]) + _dot_tn(v3[ci], k_state[ci])
    st_ref[...] = st
    return jnp.concatenate(outs, axis=0)


def _la_finish(o, og, gain, y_ref):
    o = o * lax.rsqrt(jnp.mean(o * o, axis=-1, keepdims=True) + EPS)
    y_ref[0] = _bf(o * gain * (og * _sigmoid(og)))


def _hgrn2_kernel(q_ref, f_ref, i_ref, og_ref, par_ref, y_ref, st_ref, *, tb):
    @pl.when(pl.program_id(2) == 0)
    def _():
        st_ref[...] = jnp.zeros_like(st_ref)

    log_lb = par_ref[0, 0:1, :]
    log1m_lb = par_ref[0, 1:2, :]
    one_m_lb = par_ref[0, 2:3, :]
    gain = par_ref[0, 3:4, :]
    hq = _f32(q_ref[0])
    hf = _f32(f_ref[0])
    q = hq * _sigmoid(hq)
    e = jnp.exp(-jnp.abs(hf))
    cterm = log1m_lb + (jnp.minimum(hf, 0.0) - jnp.log(1.0 + e))
    log_f = jnp.maximum(log_lb, cterm) + _softplus_neg_abs(log_lb - cterm)
    k = one_m_lb * (jnp.where(hf >= 0.0, e, 1.0) / (1.0 + e))
    o = _la_core(q, k, _f32(i_ref[0]), log_f, st_ref, tb)
    _la_finish(o, _f32(og_ref[0]), gain, y_ref)


def _gla_kernel(q_ref, k_ref, v_ref, og_ref, lr_ref, w2_ref, par_ref, y_ref, st_ref, *, tb):
    @pl.when(pl.program_id(2) == 0)
    def _():
        st_ref[...] = jnp.zeros_like(st_ref)

    bias = par_ref[0, 0:1, :]
    gain = par_ref[0, 1:2, :]
    zz = _dot(_bf(lr_ref[0]), w2_ref[0]) + bias
    log_a = _log_sigmoid(zz) * (1.0 / GLA_TAU)
    q = _f32(q_ref[0]) * (GLA_DK ** -0.5)
    o = _la_core(q, _f32(k_ref[0]), _f32(v_ref[0]), log_a, st_ref, tb)
    _la_finish(o, _f32(og_ref[0]), gain, y_ref)


def _zspec(tb, col0):
    base = col0 // LANES
    return pl.BlockSpec((1, tb, LANES), lambda b, h, t: (b, t, base + h))


def _la_params():
    return pltpu.CompilerParams(
        dimension_semantics=("parallel", "parallel", "arbitrary"), vmem_limit_bytes=VMEM_LIMIT)


def _hgrn2(z3, par, tb=4096):
    bsz, t, _ = z3.shape
    return pl.pallas_call(
        functools.partial(_hgrn2_kernel, tb=tb),
        grid=(bsz, HG_HEADS, t // tb),
        in_specs=[_zspec(tb, C_HQ), _zspec(tb, C_HF), _zspec(tb, C_HI), _zspec(tb, C_HOG),
                  pl.BlockSpec((1, 8, LANES), lambda b, h, t: (h, 0, 0))],
        out_specs=pl.BlockSpec((1, tb, LANES), lambda b, h, t: (b, t, h)),
        out_shape=jax.ShapeDtypeStruct((bsz, t, HG_DIM), jnp.bfloat16),
        scratch_shapes=[pltpu.VMEM((LANES, LANES), jnp.float32)],
        compiler_params=_la_params(),
        name="hgrn2",
    )(z3, z3, z3, z3, par)


def _gla(z3, w2p, par, tb=4096):
    bsz, t, _ = z3.shape
    lr_blk = C_GLR // LANES
    return pl.pallas_call(
        functools.partial(_gla_kernel, tb=tb),
        grid=(bsz, GLA_HEADS, t // tb),
        in_specs=[_zspec(tb, C_GQ), _zspec(tb, C_GK), _zspec(tb, C_GV), _zspec(tb, C_GOG),
                  pl.BlockSpec((1, tb, LANES), lambda b, h, t: (b, t, lr_blk)),
                  pl.BlockSpec((1, LANES, LANES), lambda b, h, t: (h, 0, 0)),
                  pl.BlockSpec((1, 8, LANES), lambda b, h, t: (h, 0, 0))],
        out_specs=pl.BlockSpec((1, tb, LANES), lambda b, h, t: (b, t, h)),
        out_shape=jax.ShapeDtypeStruct((bsz, t, GLA_VDIM), jnp.bfloat16),
        scratch_shapes=[pltpu.VMEM((LANES, LANES), jnp.float32)],
        compiler_params=_la_params(),
        name="gla",
    )(z3, z3, z3, z3, z3, w2p, par)


def _rope(x, cos_t, sin_a, sin_b):
    half = ROT_DIM // 2
    slabs = []
    for i in range(x.shape[-1] // LANES):
        xs = x[:, i * LANES:(i + 1) * LANES]
        slabs.append(xs * cos_t + pltpu.roll(xs, LANES - half, axis=1) * sin_a
                     + pltpu.roll(xs, half, axis=1) * sin_b)
    return slabs[0] if len(slabs) == 1 else jnp.concatenate(slabs, axis=1)


def _per_group(x, filler):
    swapped = pltpu.roll(x, NSA_DH, axis=1)
    if filler is None:
        return _bf(x), _bf(swapped)
    low = lax.broadcasted_iota(jnp.int32, x.shape, 1) < NSA_DH
    return _bf(jnp.where(low, x, filler)), _bf(jnp.where(low, swapped, filler))


def _nsa_prep_kernel(q_ref, kc_ref, vc_ref, ks_ref, vs_ref, kw_ref, vw_ref, cos_ref, sa_ref, sb_ref,
                     qo_ref, kco_ref, vco_ref, ks0_ref, ks1_ref, vs0_ref, vs1_ref, kw0_ref, kw1_ref,
                     vw0_ref, vw1_ref, *, tt):
    ct, sa, sb = cos_ref[...], sa_ref[...], sb_ref[...]
    qo_ref[0] = _bf(_rope(_f32(q_ref[0]), ct, sa, sb) * (NSA_DH ** -0.5 * LOG2E))
    kco_ref[0] = _rope(_f32(kc_ref[0]), ct, sa, sb)
    vco_ref[0] = _f32(vc_ref[0])
    tok = pl.program_id(1) * tt + lax.broadcasted_iota(jnp.int32, (tt, LANES), 0)
    lane = lax.broadcasted_iota(jnp.int32, (tt, LANES), 1)
    onehot = jnp.where(lane - NSA_DH == tok // SLC_BLK, 1.0, 0.0)
    ks0_ref[0], ks1_ref[0] = _per_group(_rope(_f32(ks_ref[0]), ct, sa, sb), onehot)
    kw0_ref[0], kw1_ref[0] = _per_group(_rope(_f32(kw_ref[0]), ct, sa, sb), None)
    vs0_ref[0], vs1_ref[0] = _per_group(_f32(vs_ref[0]), 1.0)
    vw0_ref[0], vw1_ref[0] = _per_group(_f32(vw_ref[0]), 1.0)


def _nsa_prep(z3, cos_t, sin_a, sin_b, tt=512):
    bsz, t, _ = z3.shape
    qw = NSA_HEADS * LANES

    def zs(col0, width):
        blk = col0 // width
        return pl.BlockSpec((1, tt, width), lambda b, i: (b, i, blk))

    def os_(width):
        return pl.BlockSpec((1, tt, width), lambda b, i: (b, i, 0))

    tab = pl.BlockSpec((tt, LANES), lambda b, i: (i, 0))
    kv_bf = jax.ShapeDtypeStruct((bsz, t, LANES), jnp.bfloat16)
    return pl.pallas_call(
        functools.partial(_nsa_prep_kernel, tt=tt),
        grid=(bsz, t // tt),
        in_specs=[zs(C_NQ, qw), zs(C_NKC, LANES), zs(C_NVC, LANES), zs(C_NKS, LANES),
                  zs(C_NVS, LANES), zs(C_NKW, LANES), zs(C_NVW, LANES), tab, tab, tab],
        out_specs=[os_(qw), os_(LANES), os_(LANES)] + [os_(LANES)] * 8,
        out_shape=[jax.ShapeDtypeStruct((bsz, t, qw), jnp.bfloat16),
                   jax.ShapeDtypeStruct((bsz, t, LANES), jnp.float32),
                   jax.ShapeDtypeStruct((bsz, t, LANES), jnp.float32)] + [kv_bf] * 8,
        compiler_params=pltpu.CompilerParams(
            dimension_semantics=("parallel", "parallel"), vmem_limit_bytes=VMEM_LIMIT),
        name="nsa_prep",
    )(z3, z3, z3, z3, z3, z3, z3, cos_t, sin_a, sin_b)


def _compress_one(src_ref, pos_ref, w1_ref, w2_ref, out0_ref, out1_ref, ncmp):
    half = CMP_BLK // 2
    u = jnp.zeros((ncmp, 2 * CMP_HID), jnp.float32)
    v = jnp.zeros((ncmp, 2 * CMP_HID), jnp.float32)
    for j in range(half):
        rows = src_ref[0, pl.ds(j, ncmp, stride=CMP_STRIDE), :]
        u = u + _dot(_bf(rows + pos_ref[j:j + 1, :]), w1_ref[j])
        v = v + _dot(_bf(rows + pos_ref[half + j:half + j + 1, :]), w1_ref[half + j])
    hid = u + pltpu.roll(v, ncmp - 1, axis=0)
    out0_ref[0], out1_ref[0] = _per_group(_dot(_bf(jax.nn.gelu(hid)), w2_ref[...]), None)


def _compress_kernel(k_ref, v_ref, pk_ref, pv_ref, w1k_ref, w2k_ref, w1v_ref, w2v_ref,
                     kc0_ref, kc1_ref, vc0_ref, vc1_ref, *, ncmp):
    _compress_one(k_ref, pk_ref, w1k_ref, w2k_ref, kc0_ref, kc1_ref, ncmp)
    _compress_one(v_ref, pv_ref, w1v_ref, w2v_ref, vc0_ref, vc1_ref, ncmp)


def _compress(kc_rot, vc_raw, pos_k, pos_v, w1k, w2k, w1v, w2v):
    bsz, t, _ = kc_rot.shape
    ncmp = t // CMP_STRIDE
    full = lambda shape: pl.BlockSpec(shape, lambda b: (0,) * len(shape))
    out = jax.ShapeDtypeStruct((bsz, ncmp, LANES), jnp.bfloat16)
    return pl.pallas_call(
        functools.partial(_compress_kernel, ncmp=ncmp),
        grid=(bsz,),
        in_specs=[pl.BlockSpec((1, t, LANES), lambda b: (b, 0, 0)),
                  pl.BlockSpec((1, t, LANES), lambda b: (b, 0, 0)),
                  full(pos_k.shape), full(pos_v.shape), full(w1k.shape), full(w2k.shape),
                  full(w1v.shape), full(w2v.shape)],
        out_specs=[pl.BlockSpec((1, ncmp, LANES), lambda b: (b, 0, 0))] * 4,
        out_shape=[out] * 4,
        compiler_params=pltpu.CompilerParams(
            dimension_semantics=("parallel",), vmem_limit_bytes=VMEM_LIMIT),
        name="nsa_compress",
    )(kc_rot, vc_raw, pos_k, pos_v, w1k, w2k, w1v, w2v)


NEG = -1e30
SEL_KB = 256


def _select_blocks(imp, t_col, nslc):
    qb = imp.shape[0]
    blk = lax.broadcasted_iota(jnp.int32, (qb, LANES), 1) - NSA_DH
    cur = t_col // SLC_BLK
    valid = (blk >= 0) & (blk <= cur)
    forced = (blk == 0) | (blk == cur) | (blk == cur - 1)
    score = jnp.where(valid & forced, 1e9, jnp.where(valid, imp, -1e9))
    sc_t = score.T
    sub = 8
    nslab = (LANES - NSA_DH) // sub
    slabs = [sc_t[NSA_DH + sub * k:NSA_DH + sub * (k + 1), :] for k in range(nslab)]
    jj = lax.broadcasted_iota(jnp.int32, (sub, qb), 0)
    ranks = [jnp.zeros((sub, qb), jnp.float32) for _ in range(nslab)]
    for i in range(nslc):
        row = sc_t[NSA_DH + i:NSA_DH + i + 1, :]
        for k in range(nslab):
            if sub * k > i:
                ahead = row >= slabs[k]
            elif sub * (k + 1) <= i:
                ahead = row > slabs[k]
            else:
                ahead = (row > slabs[k]) | ((row == slabs[k]) & (jj + sub * k > i))
            ranks[k] = ranks[k] + jnp.where(ahead, 1.0, 0.0)
    sel_t = jnp.concatenate(
        [jnp.zeros((NSA_DH, qb), jnp.float32)]
        + [jnp.where(r < float(N_SEL), 1.0, 0.0) for r in ranks], axis=0)
    return valid & (sel_t.T > 0.5)


def _nsa_attn_kernel(q_ref, kc0_ref, kc1_ref, vc0_ref, vc1_ref, ks0_ref, ks1_ref, vs0_ref, vs1_ref,
                     kw0_ref, kw1_ref, vw0_ref, vw1_ref, gate_ref, ovl_ref, y_ref,
                     qa_scr, m_scr, acc_scr, sa_scr, sb_scr, mxa_scr, mxb_scr, *, seq):
    c = pl.program_id(1)
    qb = Q_BLOCK
    rep = NSA_REP
    rows = rep * qb
    t0 = c * qb
    t_col = t0 + lax.broadcasted_iota(jnp.int32, (qb, 1), 0)
    gates = _sigmoid(_f32(gate_ref[0]))
    ncmp = kc0_ref.shape[1]
    lane_low = lax.broadcasted_iota(jnp.int32, (qb, LANES), 1) < NSA_DH
    n_idx = lax.broadcasted_iota(jnp.int32, (1, ncmp), 1)
    bias_c = jnp.where((n_idx * CMP_STRIDE + (CMP_BLK - 1)) <= t_col, 0.0, NEG)
    has_cmp = jnp.where(t_col >= CMP_BLK - 1, 1.0, 0.0)
    w_start = pl.multiple_of(jnp.maximum(t0 - WIN, 0), qb)
    wk = WIN + qb
    kp = w_start + lax.broadcasted_iota(jnp.int32, (1, wk), 1)
    bias_w = jnp.where((kp <= t_col) & (kp > t_col - WIN), 0.0, NEG)
    n_full = t0 // SEL_KB

    def with_bias(s, bias):
        return (s.reshape(rep, qb, s.shape[-1]) + bias[None]).reshape(s.shape)

    groups = ((kc0_ref, vc0_ref, ks0_ref, vs0_ref, kw0_ref, vw0_ref),
              (kc1_ref, vc1_ref, ks1_ref, vs1_ref, kw1_ref, vw1_ref))
    o_cs, acc_ws = [], []
    for g, (kc_ref, vc_ref, ks_ref, vs_ref, kw_ref, vw_ref) in enumerate(groups):
        q_heads = [q_ref[0, :, (g * rep + r) * LANES:(g * rep + r + 1) * LANES] for r in range(rep)]
        qg = jnp.concatenate(q_heads, axis=0)

        s = with_bias(_dot_nt(qg, kc_ref[0]), bias_c)
        m = jnp.max(s, axis=-1, keepdims=True)
        p = (jnp.exp2(s - m).reshape(rep, qb, ncmp) * has_cmp[None]).reshape(rows, ncmp)
        l = jnp.sum(p, axis=-1, keepdims=True)
        p_c = p / jnp.where(l > 0.0, l, 1.0)
        o_cs.append(_dot(_bf(p_c), vc_ref[0]))
        p_sum = p_c[0:qb] + p_c[qb:2 * qb] + p_c[2 * qb:3 * qb] + p_c[3 * qb:4 * qb]
        imp = jnp.dot(p_sum, ovl_ref[...], preferred_element_type=jnp.float32,
                      precision=lax.Precision.HIGHEST)
        sel = _select_blocks(imp, t_col, seq // SLC_BLK)
        sel_bias = _bf(jnp.where(sel, 0.0, NEG))
        for r, qh in enumerate(q_heads):
            qa_scr[g, r * qb:(r + 1) * qb, :] = jnp.where(lane_low, qh, sel_bias)
        m_scr[g] = jnp.full((rows, LANES), NEG, jnp.float32)
        acc_scr[g] = jnp.zeros((rows, LANES), jnp.float32)

    def chunk_start(j):
        return pl.multiple_of(jnp.minimum(j * SEL_KB, seq - SEL_KB), SEL_KB)

    def produce(j, dst_scr, dst_max_scr):
        k0 = chunk_start(j)
        for g in range(NSA_GROUPS):
            s = _dot_nt(qa_scr[g], groups[g][2][0, pl.ds(k0, SEL_KB), :])
            dst_scr[g] = s
            dst_max_scr[g] = jnp.broadcast_to(jnp.max(s, axis=-1, keepdims=True), (rows, LANES))

    def consume(j, src_scr, src_max_scr, causal):
        k0 = chunk_start(j)
        if causal:
            tok = j * SEL_KB + lax.broadcasted_iota(jnp.int32, (1, SEL_KB), 1)
            bias = jnp.where(tok <= t_col, 0.0, NEG)
        for g in range(NSA_GROUPS):
            vs_ref = groups[g][3]
            m = m_scr[g]
            if causal:
                s = with_bias(src_scr[g], bias)
                m_new = jnp.maximum(m, jnp.max(s, axis=-1, keepdims=True))
            else:
                s = src_scr[g]
                m_new = jnp.maximum(m, src_max_scr[g])
            p = jnp.exp2(s - jnp.concatenate([m_new] * (SEL_KB // LANES), axis=1))
            pv = _dot(_bf(p), vs_ref[0, pl.ds(k0, SEL_KB), :])
            acc_scr[g] = jnp.exp2(m - m_new) * acc_scr[g] + pv
            m_scr[g] = m_new

    def loop_body(i, carry):
        produce(2 * i + 1, sb_scr, mxb_scr)
        consume(2 * i, sa_scr, mxa_scr, False)
        produce(2 * i + 2, sa_scr, mxa_scr)
        consume(2 * i + 1, sb_scr, mxb_scr, False)
        return carry

    n_pairs = n_full // 2
    produce(0, sa_scr, mxa_scr)
    lax.fori_loop(0, n_pairs, loop_body, 0)
    produce(2 * n_pairs + 1, sb_scr, mxb_scr)
    consume(2 * n_pairs, sa_scr, mxa_scr, True)

    consume(2 * n_pairs + 1, sb_scr, mxb_scr, True)
    carry = (None, acc_scr[0], None, acc_scr[1])

    for g in range(NSA_GROUPS):
        kw_ref, vw_ref = groups[g][4], groups[g][5]
        qg = jnp.concatenate(
            [q_ref[0, :, (g * rep + r) * LANES:(g * rep + r + 1) * LANES] for r in range(rep)], axis=0)
        s = with_bias(_dot_nt(qg, kw_ref[0, pl.ds(w_start, wk), :]), bias_w)
        m = jnp.max(s, axis=-1, keepdims=True)
        acc_ws.append(_dot(_bf(jnp.exp2(s - m)), vw_ref[0, pl.ds(w_start, wk), :]))

    def normalized(acc):
        return acc / pltpu.roll(acc, NSA_DH, axis=1)

    for g in range(NSA_GROUPS):
        o_c, o_s, o_w = o_cs[g], normalized(carry[2 * g + 1]), normalized(acc_ws[g])
        for r in range(rep):
            hd = g * rep + r
            sl = slice(r * qb, (r + 1) * qb)
            out = (o_c[sl] * gates[:, 3 * hd:3 * hd + 1] + o_s[sl] * gates[:, 3 * hd + 1:3 * hd + 2]
                   + o_w[sl] * gates[:, 3 * hd + 2:3 * hd + 3])
            y_ref[0, :, hd * LANES:(hd + 1) * LANES] = _bf(jnp.where(lane_low, out, 0.0))


def _nsa_attn(q_rot, kcs, kvs, z3, ovl):
    bsz, t, qw = q_rot.shape
    ncmp = kcs[0].shape[1]
    assert t % SEL_KB == 0 and t >= WIN + Q_BLOCK and t // SLC_BLK <= LANES - NSA_DH
    gblk = C_NGATE // LANES
    per_b = lambda rows: pl.BlockSpec((1, rows, LANES), lambda b, c: (b, 0, 0))
    return pl.pallas_call(
        functools.partial(_nsa_attn_kernel, seq=t),
        grid=(bsz, t // Q_BLOCK),
        in_specs=[pl.BlockSpec((1, Q_BLOCK, qw), lambda b, c: (b, c, 0))]
        + [per_b(ncmp)] * 4 + [per_b(t)] * 8
        + [pl.BlockSpec((1, Q_BLOCK, LANES), lambda b, c: (b, c, gblk)),
           pl.BlockSpec(ovl.shape, lambda b, c: (0, 0))],
        out_specs=pl.BlockSpec((1, Q_BLOCK, qw), lambda b, c: (b, c, 0)),
        out_shape=jax.ShapeDtypeStruct((bsz, t, qw), jnp.bfloat16),
        scratch_shapes=[pltpu.VMEM((NSA_GROUPS, NSA_REP * Q_BLOCK, LANES), jnp.bfloat16),
                        pltpu.VMEM((NSA_GROUPS, NSA_REP * Q_BLOCK, LANES), jnp.float32),
                        pltpu.VMEM((NSA_GROUPS, NSA_REP * Q_BLOCK, LANES), jnp.float32),
                        pltpu.VMEM((NSA_GROUPS, NSA_REP * Q_BLOCK, SEL_KB), jnp.float32),
                        pltpu.VMEM((NSA_GROUPS, NSA_REP * Q_BLOCK, SEL_KB), jnp.float32),
                        pltpu.VMEM((NSA_GROUPS, NSA_REP * Q_BLOCK, LANES), jnp.float32),
                        pltpu.VMEM((NSA_GROUPS, NSA_REP * Q_BLOCK, LANES), jnp.float32)],
        compiler_params=pltpu.CompilerParams(
            dimension_semantics=("parallel", "arbitrary"), vmem_limit_bytes=VMEM_LIMIT),
        name="nsa_attn",
    )(q_rot, *kcs, *kvs, z3, ovl)


def _merge_kernel(x_ref, yh_ref, yg_ref, yn_ref, mg_ref, wh_ref, wg_ref, wn_ref, wo_ref, o_ref):
    d = D_MODEL
    m = _sigmoid(_f32(mg_ref[:, 0:d])) * _dot(yh_ref[...], wh_ref[...])
    m = m + _sigmoid(_f32(mg_ref[:, d:2 * d])) * _dot(yg_ref[...], wg_ref[...])
    m = m + _sigmoid(_f32(mg_ref[:, 2 * d:3 * d])) * _dot(yn_ref[...], wn_ref[...])
    o_ref[...] = x_ref[...] + _dot(_bf(m), wo_ref[...])


def _merge(x2, yh, yg, yn, z2, wh, wg, wn, wo, tm=512):
    n, d = x2.shape
    mgblk = C_MG // (3 * d)
    row = lambda w: pl.BlockSpec((tm, w), lambda i: (i, 0))
    full = lambda a: pl.BlockSpec(a.shape, lambda i: (0, 0))
    return pl.pallas_call(
        _merge_kernel,
        grid=(n // tm,),
        in_specs=[row(d), row(yh.shape[1]), row(yg.shape[1]), row(yn.shape[1]),
                  pl.BlockSpec((tm, 3 * d), lambda i: (i, mgblk)),
                  full(wh), full(wg), full(wn), full(wo)],
        out_specs=row(d),
        out_shape=jax.ShapeDtypeStruct((n, d), jnp.float32),
        compiler_params=pltpu.CompilerParams(
            dimension_semantics=("parallel",), vmem_limit_bytes=VMEM_LIMIT),
        name="merge",
    )(x2, yh, yg, yn, z2, wh, wg, wn, wo)


def _ffn_kernel(x_ref, g_ref, wg_ref, wu_ref, wd_ref, fg_ref, o_ref, h_scr, acc_scr, *, final_norm):
    j = pl.program_id(1)

    @pl.when(j == 0)
    def _():
        x = x_ref[...]
        y = x * lax.rsqrt(jnp.mean(x * x, axis=-1, keepdims=True) + EPS)
        h_scr[...] = _bf(y * g_ref[...])
        acc_scr[...] = jnp.zeros_like(acc_scr)

    h = h_scr[...]
    a = _dot(h, wg_ref[...])
    u = _dot(h, wu_ref[...])
    acc_scr[...] += _dot(_bf(a * _sigmoid(a) * u), wd_ref[...])

    @pl.when(j == pl.num_programs(1) - 1)
    def _():
        out = x_ref[...] + acc_scr[...]
        if final_norm:
            out = out * lax.rsqrt(jnp.mean(out * out, axis=-1, keepdims=True) + EPS) * fg_ref[...]
        o_ref[...] = out


def _ffn(x2, g, wg, wu, wd, fg, final_norm, tm=1024, tf=256):
    n, d = x2.shape
    ff = wg.shape[1]
    return pl.pallas_call(
        functools.partial(_ffn_kernel, final_norm=final_norm),
        grid=(n // tm, ff // tf),
        in_specs=[pl.BlockSpec((tm, d), lambda i, j: (i, 0)),
                  pl.BlockSpec((1, d), lambda i, j: (0, 0)),
                  pl.BlockSpec((d, tf), lambda i, j: (0, j)),
                  pl.BlockSpec((d, tf), lambda i, j: (0, j)),
                  pl.BlockSpec((tf, d), lambda i, j: (j, 0)),
                  pl.BlockSpec((1, d), lambda i, j: (0, 0))],
        out_specs=pl.BlockSpec((tm, d), lambda i, j: (i, 0)),
        out_shape=jax.ShapeDtypeStruct((n, d), jnp.float32),
        scratch_shapes=[pltpu.VMEM((tm, d), jnp.bfloat16), pltpu.VMEM((tm, d), jnp.float32)],
        compiler_params=pltpu.CompilerParams(
            dimension_semantics=("parallel", "arbitrary"), vmem_limit_bytes=VMEM_LIMIT),
        name="ffn",
    )(x2, g, wg, wu, wd, fg)


def _pad_heads(w, heads, width):
    lead = w.shape[:-1]
    w = w.reshape(lead + (heads, width))
    w = jnp.pad(w, [(0, 0)] * len(lead) + [(0, 0), (0, LANES - width)])
    return w.reshape(lead + (heads * LANES,))


def _layout_w_in(w):
    d = w.shape[0]
    o = 0
    parts = {}
    for name, n in (("hq", 512), ("hf", 512), ("hi", 512), ("hog", 512), ("gq", 256), ("gk", 256),
                    ("gv", 512), ("glr", 16), ("gog", 512), ("nq", 512), ("nkc", 128), ("nvc", 128),
                    ("nks", 128), ("nvs", 128), ("nkw", 128), ("nvw", 128), ("ngate", 24), ("mg", 3072)):
        parts[name] = w[:, o:o + n]
        o += n
    nq = _pad_heads(parts["nq"], NSA_HEADS, NSA_DH)
    padc = lambda a: jnp.pad(a, ((0, 0), (0, LANES - a.shape[1])))
    cols = [parts["hq"], parts["hf"], parts["hi"], parts["hog"],
            _pad_heads(parts["gq"], GLA_HEADS, GLA_DK), _pad_heads(parts["gk"], GLA_HEADS, GLA_DK),
            parts["gv"], parts["gog"], nq,
            parts["nkc"], parts["nvc"], parts["nks"], parts["nvs"], parts["nkw"], parts["nvw"],
            padc(parts["ngate"]), padc(parts["glr"]), parts["mg"]]
    out = jnp.concatenate(cols, axis=1)
    assert out.shape[1] == Z_COLS
    return _bf(out)


def _blockdiag2(w):
    z = jnp.zeros_like(w)
    top = jnp.concatenate([w, z], axis=-1)
    bot = jnp.concatenate([z, w], axis=-1)
    return jnp.concatenate([top, bot], axis=-2)


def _rope_tables(t):
    pos = jnp.arange(t, dtype=jnp.float32)
    inv_freq = ROPE_THETA ** (-jnp.arange(0, ROT_DIM, 2, dtype=jnp.float32) / ROT_DIM)
    ang = pos[:, None] * inv_freq[None, :]
    cos, sin = jnp.cos(ang), jnp.sin(ang)
    half = ROT_DIM // 2
    lane = jnp.arange(LANES) % NSA_DH
    first = lane < half
    second = (lane >= half) & (lane < ROT_DIM)
    idx = jnp.where(first, lane, jnp.where(second, lane - half, 0))
    cos_t = jnp.where(first | second, cos[:, idx], 1.0)
    sin_a = jnp.where(first, -sin[:, idx], 0.0)
    sin_b = jnp.where(second, sin[:, idx], 0.0)
    return cos_t, sin_a, sin_b


def kernel(x, norm1_g, w_in, hg_lb, hg_norm_g, gla_w2, gla_b, gla_norm_g, cmp_pos_k, cmp_pos_v,
           cmp_w1_k, cmp_w2_k, cmp_w1_v, cmp_w2_v, w_br_hg, w_br_gla, w_br_nsa, w_out,
           norm2_g, w_ffn_gate, w_ffn_up, w_ffn_down, final_norm_g):
    bsz, t, d = x.shape
    depth = w_in.shape[0]
    n = bsz * t
    f32 = jnp.float32

    cos_t, sin_a, sin_b = _rope_tables(t)
    lbs = jnp.cumsum(jax.nn.softmax(hg_lb.astype(f32), axis=0), axis=0)
    lbs = lbs - lbs[0]

    nslc = t // SLC_BLK
    ncmp_pad = t // CMP_STRIDE
    cmp_start = jnp.arange(ncmp_pad) * CMP_STRIDE
    blk = jnp.arange(LANES) - NSA_DH
    ovl = ((cmp_start[:, None] < (blk[None, :] + 1) * SLC_BLK)
           & (cmp_start[:, None] + CMP_BLK - 1 >= blk[None, :] * SLC_BLK)
           & (blk[None, :] >= 0) & (blk[None, :] < nslc)
           & (jnp.arange(ncmp_pad)[:, None] < ncmp_pad - 1)).astype(f32)

    x2 = x.reshape(n, d)
    for l in range(depth):
        z2 = _inproj(x2, norm1_g[l][None, :], _layout_w_in(w_in[l]))
        z3 = z2.reshape(bsz, t, Z_COLS)

        lb = lbs[l].reshape(HG_HEADS, LANES)
        zeros = jnp.zeros_like(lb)
        hg_par = jnp.stack([jnp.log(lb), jnp.log1p(-lb), 1.0 - lb,
                            hg_norm_g[l].reshape(HG_HEADS, LANES), zeros, zeros, zeros, zeros], axis=1)
        y_hg = _hgrn2(z3, hg_par)

        w2p = jnp.pad(_pad_heads(gla_w2[l], GLA_HEADS, GLA_DK), ((0, LANES - GLA_RANK), (0, 0)))
        w2p = _bf(w2p.reshape(LANES, GLA_HEADS, LANES).transpose(1, 0, 2))
        gb = _pad_heads(gla_b[l], GLA_HEADS, GLA_DK).reshape(GLA_HEADS, LANES)
        gla_par = jnp.stack([gb, gla_norm_g[l].reshape(GLA_HEADS, LANES),
                             zeros, zeros, zeros, zeros, zeros, zeros], axis=1)
        y_gla = _gla(z3, w2p, gla_par)

        q_rot, kc_rot, vc_raw, *kvs = _nsa_prep(z3, cos_t, sin_a, sin_b)
        tile2 = lambda p: jnp.concatenate([p, p], axis=1)
        w1k = _bf(_blockdiag2(cmp_w1_k[l].reshape(CMP_BLK, NSA_DH, CMP_HID)))
        w1v = _bf(_blockdiag2(cmp_w1_v[l].reshape(CMP_BLK, NSA_DH, CMP_HID)))
        kcs = _compress(kc_rot, vc_raw, tile2(cmp_pos_k[l]), tile2(cmp_pos_v[l]),
                        w1k, _bf(_blockdiag2(cmp_w2_k[l])), w1v, _bf(_blockdiag2(cmp_w2_v[l])))
        y_nsa = _nsa_attn(q_rot, kcs, kvs, z3, ovl)

        wn = _pad_heads(w_br_nsa[l].T, NSA_HEADS, NSA_DH).T
        x2 = _merge(x2, y_hg.reshape(n, HG_DIM), y_gla.reshape(n, GLA_VDIM),
                    y_nsa.reshape(n, NSA_HEADS * LANES), z2,
                    _bf(w_br_hg[l]), _bf(w_br_gla[l]), _bf(wn), _bf(w_out[l]))

        x2 = _ffn(x2, norm2_g[l][None, :], _bf(w_ffn_gate[l]), _bf(w_ffn_up[l]), _bf(w_ffn_down[l]),
                  final_norm_g[None, :], final_norm=(l == depth - 1))
    return x2.reshape(bsz, t, d)
```

```python
import functools

import jax
import jax.numpy as jnp
from jax import lax
from jax.experimental import pallas as pl
from jax.experimental.pallas import tpu as pltpu

EPS = 1e-6
D_MODEL = 1024
LANES = 128

HG_HEADS = 4
HG_DIM = 512
GLA_HEADS = 4
GLA_DK = 64
GLA_KDIM = 256
GLA_VDIM = 512
GLA_RANK = 16
GLA_TAU = 16.0
LA_CHUNK = 64
LA_SUB = 16

NSA_HEADS = 8
NSA_GROUPS = 2
NSA_REP = 4
NSA_DH = 64
CMP_STRIDE = 16
CMP_BLK = 32
CMP_HID = 128
SLC_BLK = 64
N_SEL = 16
WIN = 512
Q_BLOCK = 128
ROPE_THETA = 500000.0
ROT_DIM = 16
FF_DIM = 2816

C_HQ, C_HF, C_HI, C_HOG = 0, 512, 1024, 1536
C_GQ, C_GK, C_GV, C_GOG = 2048, 2560, 3072, 3584
C_NQ = 4096
C_NKC, C_NVC, C_NKS, C_NVS, C_NKW, C_NVW = 5120, 5248, 5376, 5504, 5632, 5760
C_NGATE, C_GLR = 5888, 6016
C_MG = 6144
Z_COLS = 9216

VMEM_LIMIT = 56 * 1024 * 1024
LOG2E = 1.4426950408889634

_NT = (((1,), (1,)), ((), ()))
_TN = (((0,), (0,)), ((), ()))


def _bf(x):
    return x.astype(jnp.bfloat16)


def _dot(a, b):
    return jnp.dot(a, b, preferred_element_type=jnp.float32)


def _dot_nt(a, b):
    return lax.dot_general(a, b, _NT, preferred_element_type=jnp.float32)


def _dot_tn(a, b):
    return lax.dot_general(a, b, _TN, preferred_element_type=jnp.float32)


def _f32(x):
    return x.astype(jnp.float32)


def _sigmoid(x):
    return 0.5 * jnp.tanh(0.5 * x) + 0.5


def _softplus_neg_abs(x):
    return jnp.log(1.0 + jnp.exp(-jnp.abs(x)))


def _log_sigmoid(x):
    return jnp.minimum(x, 0.0) - _softplus_neg_abs(x)


def _inproj_kernel(x_ref, g_ref, w_ref, z_ref, h_scr):
    @pl.when(pl.program_id(1) == 0)
    def _():
        x = x_ref[...]
        y = x * lax.rsqrt(jnp.mean(x * x, axis=-1, keepdims=True) + EPS)
        h_scr[...] = _bf(y * g_ref[...])

    z_ref[...] = _dot(h_scr[...], w_ref[...]).astype(z_ref.dtype)


def _inproj(x2, g, w, tm=1024, tn=3072):
    n, d = x2.shape
    cols = w.shape[1]
    return pl.pallas_call(
        _inproj_kernel,
        grid=(n // tm, cols // tn),
        in_specs=[
            pl.BlockSpec((tm, d), lambda i, j: (i, 0)),
            pl.BlockSpec((1, d), lambda i, j: (0, 0)),
            pl.BlockSpec((d, tn), lambda i, j: (0, j)),
        ],
        out_specs=pl.BlockSpec((tm, tn), lambda i, j: (i, j)),
        out_shape=jax.ShapeDtypeStruct((n, cols), jnp.bfloat16),
        scratch_shapes=[pltpu.VMEM((tm, d), jnp.bfloat16)],
        compiler_params=pltpu.CompilerParams(
            dimension_semantics=("parallel", "arbitrary"), vmem_limit_bytes=VMEM_LIMIT),
        name="inproj",
    )(x2, g, w)


def _chunk_cumsum(g, tb):
    pos = lax.broadcasted_iota(jnp.int32, (tb, 1), 0) % LA_CHUNK
    b = g
    shift = 1
    while shift < LA_CHUNK:
        b = b + jnp.where(pos >= shift, pltpu.roll(b, shift, axis=0), 0.0)
        shift *= 2
    return b


def _la_core(q, k, v, g, st_ref, tb):
    c, r = LA_CHUNK, LA_SUB
    n = tb // c
    nsub = c // r
    dk = q.shape[-1]
    b = _chunk_cumsum(g, tb)
    b3 = b.reshape(n, c, dk)
    q3 = q.reshape(n, c, dk)
    k3 = k.reshape(n, c, dk)
    v3 = _bf(v).reshape(n, c, dk)
    blast = b3[:, c - 1:c, :]
    q_state = _bf(q3 * jnp.exp(b3))
    k_state = _bf(k3 * jnp.exp(blast - b3))

    b4 = b.reshape(n * nsub, r, dk)
    ref = b4[:, 0:1, :]
    q_loc = _bf(q.reshape(n * nsub, r, dk) * jnp.exp(b4 - ref))
    s_idx = lax.broadcasted_iota(jnp.int32, (1, nsub, c, 1), 2)
    i_idx = lax.broadcasted_iota(jnp.int32, (1, nsub, c, 1), 1)
    e = jnp.where(s_idx < r * (i_idx + 1), ref.reshape(n, nsub, 1, dk) - b3[:, None, :, :], 0.0)
    k_loc = _bf(k3[:, None, :, :] * jnp.exp(e)).reshape(n * nsub, c, dk)
    a = jnp.einsum("utd,usd->uts", q_loc, k_loc, preferred_element_type=jnp.float32)
    a = a.reshape(n, c, c)
    t_i = lax.broadcasted_iota(jnp.int32, (1, c, c), 1)
    s_i = lax.broadcasted_iota(jnp.int32, (1, c, c), 2)
    a = jnp.where(s_i <= t_i, a, 0.0)
    o_intra = jnp.einsum("nts,nsv->ntv", _bf(a), v3, preferred_element_type=jnp.float32)

    st = st_ref[...]
    outs = []
    for ci in range(n):
        outs.append(_dot_nt(q_state[ci], _bf(st)) + o_intra[ci])
        st = st * jnp.exp(blast[ci]) + _dot_tn(v3[ci], k_state[ci])
    st_ref[...] = st
    return jnp.concatenate(outs, axis=0)


def _la_finish(o, og, gain, y_ref):
    o = o * lax.rsqrt(jnp.mean(o * o, axis=-1, keepdims=True) + EPS)
    y_ref[0] = _bf(o * gain * (og * _sigmoid(og)))


def _hgrn2_kernel(q_ref, f_ref, i_ref, og_ref, par_ref, y_ref, st_ref, *, tb):
    @pl.when(pl.program_id(2) == 0)
    def _():
        st_ref[...] = jnp.zeros_like(st_ref)

    log_lb = par_ref[0, 0:1, :]
    log1m_lb = par_ref[0, 1:2, :]
    one_m_lb = par_ref[0, 2:3, :]
    gain = par_ref[0, 3:4, :]
    hq = _f32(q_ref[0])
    hf = _f32(f_ref[0])
    q = hq * _sigmoid(hq)
    e = jnp.exp(-jnp.abs(hf))
    cterm = log1m_lb + (jnp.minimum(hf, 0.0) - jnp.log(1.0 + e))
    log_f = jnp.maximum(log_lb, cterm) + _softplus_neg_abs(log_lb - cterm)
    k = one_m_lb * (jnp.where(hf >= 0.0, e, 1.0) / (1.0 + e))
    o = _la_core(q, k, _f32(i_ref[0]), log_f, st_ref, tb)
    _la_finish(o, _f32(og_ref[0]), gain, y_ref)


def _gla_kernel(q_ref, k_ref, v_ref, og_ref, lr_ref, w2_ref, par_ref, y_ref, st_ref, *, tb):
    @pl.when(pl.program_id(2) == 0)
    def _():
        st_ref[...] = jnp.zeros_like(st_ref)

    bias = par_ref[0, 0:1, :]
    gain = par_ref[0, 1:2, :]
    zz = _dot(_bf(lr_ref[0]), w2_ref[0]) + bias
    log_a = _log_sigmoid(zz) * (1.0 / GLA_TAU)
    q = _f32(q_ref[0]) * (GLA_DK ** -0.5)
    o = _la_core(q, _f32(k_ref[0]), _f32(v_ref[0]), log_a, st_ref, tb)
    _la_finish(o, _f32(og_ref[0]), gain, y_ref)


def _zspec(tb, col0):
    base = col0 // LANES
    return pl.BlockSpec((1, tb, LANES), lambda b, h, t: (b, t, base + h))


def _la_params():
    return pltpu.CompilerParams(
        dimension_semantics=("parallel", "parallel", "arbitrary"), vmem_limit_bytes=VMEM_LIMIT)


def _hgrn2(z3, par, tb=4096):
    bsz, t, _ = z3.shape
    return pl.pallas_call(
        functools.partial(_hgrn2_kernel, tb=tb),
        grid=(bsz, HG_HEADS, t // tb),
        in_specs=[_zspec(tb, C_HQ), _zspec(tb, C_HF), _zspec(tb, C_HI), _zspec(tb, C_HOG),
                  pl.BlockSpec((1, 8, LANES), lambda b, h, t: (h, 0, 0))],
        out_specs=pl.BlockSpec((1, tb, LANES), lambda b, h, t: (b, t, h)),
        out_shape=jax.ShapeDtypeStruct((bsz, t, HG_DIM), jnp.bfloat16),
        scratch_shapes=[pltpu.VMEM((LANES, LANES), jnp.float32)],
        compiler_params=_la_params(),
        name="hgrn2",
    )(z3, z3, z3, z3, par)


def _gla(z3, w2p, par, tb=4096):
    bsz, t, _ = z3.shape
    lr_blk = C_GLR // LANES
    return pl.pallas_call(
        functools.partial(_gla_kernel, tb=tb),
        grid=(bsz, GLA_HEADS, t // tb),
        in_specs=[_zspec(tb, C_GQ), _zspec(tb, C_GK), _zspec(tb, C_GV), _zspec(tb, C_GOG),
                  pl.BlockSpec((1, tb, LANES), lambda b, h, t: (b, t, lr_blk)),
                  pl.BlockSpec((1, LANES, LANES), lambda b, h, t: (h, 0, 0)),
                  pl.BlockSpec((1, 8, LANES), lambda b, h, t: (h, 0, 0))],
        out_specs=pl.BlockSpec((1, tb, LANES), lambda b, h, t: (b, t, h)),
        out_shape=jax.ShapeDtypeStruct((bsz, t, GLA_VDIM), jnp.bfloat16),
        scratch_shapes=[pltpu.VMEM((LANES, LANES), jnp.float32)],
        compiler_params=_la_params(),
        name="gla",
    )(z3, z3, z3, z3, z3, w2p, par)


def _rope(x, cos_t, sin_a, sin_b):
    half = ROT_DIM // 2
    slabs = []
    for i in range(x.shape[-1] // LANES):
        xs = x[:, i * LANES:(i + 1) * LANES]
        slabs.append(xs * cos_t + pltpu.roll(xs, LANES - half, axis=1) * sin_a
                     + pltpu.roll(xs, half, axis=1) * sin_b)
    return slabs[0] if len(slabs) == 1 else jnp.concatenate(slabs, axis=1)


def _per_group(x, filler):
    swapped = pltpu.roll(x, NSA_DH, axis=1)
    if filler is None:
        return _bf(x), _bf(swapped)
    low = lax.broadcasted_iota(jnp.int32, x.shape, 1) < NSA_DH
    return _bf(jnp.where(low, x, filler)), _bf(jnp.where(low, swapped, filler))


def _nsa_prep_kernel(q_ref, kc_ref, vc_ref, ks_ref, vs_ref, kw_ref, vw_ref, cos_ref, sa_ref, sb_ref,
                     qo_ref, kco_ref, vco_ref, ks0_ref, ks1_ref, vs0_ref, vs1_ref, kw0_ref, kw1_ref,
                     vw0_ref, vw1_ref, *, tt):
    ct, sa, sb = cos_ref[...], sa_ref[...], sb_ref[...]
    qo_ref[0] = _bf(_rope(_f32(q_ref[0]), ct, sa, sb) * (NSA_DH ** -0.5 * LOG2E))
    kco_ref[0] = _rope(_f32(kc_ref[0]), ct, sa, sb)
    vco_ref[0] = _f32(vc_ref[0])
    tok = pl.program_id(1) * tt + lax.broadcasted_iota(jnp.int32, (tt, LANES), 0)
    lane = lax.broadcasted_iota(jnp.int32, (tt, LANES), 1)
    onehot = jnp.where(lane - NSA_DH == tok // SLC_BLK, 1.0, 0.0)
    ks0_ref[0], ks1_ref[0] = _per_group(_rope(_f32(ks_ref[0]), ct, sa, sb), onehot)
    kw0_ref[0], kw1_ref[0] = _per_group(_rope(_f32(kw_ref[0]), ct, sa, sb), None)
    vs0_ref[0], vs1_ref[0] = _per_group(_f32(vs_ref[0]), 1.0)
    vw0_ref[0], vw1_ref[0] = _per_group(_f32(vw_ref[0]), 1.0)


def _nsa_prep(z3, cos_t, sin_a, sin_b, tt=512):
    bsz, t, _ = z3.shape
    qw = NSA_HEADS * LANES

    def zs(col0, width):
        blk = col0 // width
        return pl.BlockSpec((1, tt, width), lambda b, i: (b, i, blk))

    def os_(width):
        return pl.BlockSpec((1, tt, width), lambda b, i: (b, i, 0))

    tab = pl.BlockSpec((tt, LANES), lambda b, i: (i, 0))
    kv_bf = jax.ShapeDtypeStruct((bsz, t, LANES), jnp.bfloat16)
    return pl.pallas_call(
        functools.partial(_nsa_prep_kernel, tt=tt),
        grid=(bsz, t // tt),
        in_specs=[zs(C_NQ, qw), zs(C_NKC, LANES), zs(C_NVC, LANES), zs(C_NKS, LANES),
                  zs(C_NVS, LANES), zs(C_NKW, LANES), zs(C_NVW, LANES), tab, tab, tab],
        out_specs=[os_(qw), os_(LANES), os_(LANES)] + [os_(LANES)] * 8,
        out_shape=[jax.ShapeDtypeStruct((bsz, t, qw), jnp.bfloat16),
                   jax.ShapeDtypeStruct((bsz, t, LANES), jnp.float32),
                   jax.ShapeDtypeStruct((bsz, t, LANES), jnp.float32)] + [kv_bf] * 8,
        compiler_params=pltpu.CompilerParams(
            dimension_semantics=("parallel", "parallel"), vmem_limit_bytes=VMEM_LIMIT),
        name="nsa_prep",
    )(z3, z3, z3, z3, z3, z3, z3, cos_t, sin_a, sin_b)


def _compress_one(src_ref, pos_ref, w1_ref, w2_ref, out0_ref, out1_ref, ncmp):
    half = CMP_BLK // 2
    u = jnp.zeros((ncmp, 2 * CMP_HID), jnp.float32)
    v = jnp.zeros((ncmp, 2 * CMP_HID), jnp.float32)
    for j in range(half):
        rows = src_ref[0, pl.ds(j, ncmp, stride=CMP_STRIDE), :]
        u = u + _dot(_bf(rows + pos_ref[j:j + 1, :]), w1_ref[j])
        v = v + _dot(_bf(rows + pos_ref[half + j:half + j + 1, :]), w1_ref[half + j])
    hid = u + pltpu.roll(v, ncmp - 1, axis=0)
    out0_ref[0], out1_ref[0] = _per_group(_dot(_bf(jax.nn.gelu(hid)), w2_ref[...]), None)


def _compress_kernel(k_ref, v_ref, pk_ref, pv_ref, w1k_ref, w2k_ref, w1v_ref, w2v_ref,
                     kc0_ref, kc1_ref, vc0_ref, vc1_ref, *, ncmp):
    _compress_one(k_ref, pk_ref, w1k_ref, w2k_ref, kc0_ref, kc1_ref, ncmp)
    _compress_one(v_ref, pv_ref, w1v_ref, w2v_ref, vc0_ref, vc1_ref, ncmp)


def _compress(kc_rot, vc_raw, pos_k, pos_v, w1k, w2k, w1v, w2v):
    bsz, t, _ = kc_rot.shape
    ncmp = t // CMP_STRIDE
    full = lambda shape: pl.BlockSpec(shape, lambda b: (0,) * len(shape))
    out = jax.ShapeDtypeStruct((bsz, ncmp, LANES), jnp.bfloat16)
    return pl.pallas_call(
        functools.partial(_compress_kernel, ncmp=ncmp),
        grid=(bsz,),
        in_specs=[pl.BlockSpec((1, t, LANES), lambda b: (b, 0, 0)),
                  pl.BlockSpec((1, t, LANES), lambda b: (b, 0, 0)),
                  full(pos_k.shape), full(pos_v.shape), full(w1k.shape), full(w2k.shape),
                  full(w1v.shape), full(w2v.shape)],
        out_specs=[pl.BlockSpec((1, ncmp, LANES), lambda b: (b, 0, 0))] * 4,
        out_shape=[out] * 4,
        compiler_params=pltpu.CompilerParams(
            dimension_semantics=("parallel",), vmem_limit_bytes=VMEM_LIMIT),
        name="nsa_compress",
    )(kc_rot, vc_raw, pos_k, pos_v, w1k, w2k, w1v, w2v)


NEG = -1e30
SEL_KB = 256


def _select_blocks(imp, t_col, nslc):
    qb = imp.shape[0]
    blk = lax.broadcasted_iota(jnp.int32, (qb, LANES), 1) - NSA_DH
    cur = t_col // SLC_BLK
    valid = (blk >= 0) & (blk <= cur)
    forced = (blk == 0) | (blk == cur) | (blk == cur - 1)
    score = jnp.where(valid & forced, 1e9, jnp.where(valid, imp, -1e9))
    sc_t = score.T
    sub = 8
    nslab = (LANES - NSA_DH) // sub
    slabs = [sc_t[NSA_DH + sub * k:NSA_DH + sub * (k + 1), :] for k in range(nslab)]
    jj = lax.broadcasted_iota(jnp.int32, (sub, qb), 0)
    ranks = [jnp.zeros((sub, qb), jnp.float32) for _ in range(nslab)]
    for i in range(nslc):
        row = sc_t[NSA_DH + i:NSA_DH + i + 1, :]
        for k in range(nslab):
            if sub * k > i:
                ahead = row >= slabs[k]
            elif sub * (k + 1) <= i:
                ahead = row > slabs[k]
            else:
                ahead = (row > slabs[k]) | ((row == slabs[k]) & (jj + sub * k > i))
            ranks[k] = ranks[k] + jnp.where(ahead, 1.0, 0.0)
    sel_t = jnp.concatenate(
        [jnp.zeros((NSA_DH, qb), jnp.float32)]
        + [jnp.where(r < float(N_SEL), 1.0, 0.0) for r in ranks], axis=0)
    return valid & (sel_t.T > 0.5)


def _nsa_attn_kernel(q_ref, kc0_ref, kc1_ref, vc0_ref, vc1_ref, ks0_ref, ks1_ref, vs0_ref, vs1_ref,
                     kw0_ref, kw1_ref, vw0_ref, vw1_ref, gate_ref, ovl_ref, y_ref,
                     qa_scr, m_scr, acc_scr, sa_scr, sb_scr, mxa_scr, mxb_scr, *, seq):
    c = pl.program_id(1)
    qb = Q_BLOCK
    rep = NSA_REP
    rows = rep * qb
    t0 = c * qb
    t_col = t0 + lax.broadcasted_iota(jnp.int32, (qb, 1), 0)
    gates = _sigmoid(_f32(gate_ref[0]))
    ncmp = kc0_ref.shape[1]
    lane_low = lax.broadcasted_iota(jnp.int32, (qb, LANES), 1) < NSA_DH
    n_idx = lax.broadcasted_iota(jnp.int32, (1, ncmp), 1)
    bias_c = jnp.where((n_idx * CMP_STRIDE + (CMP_BLK - 1)) <= t_col, 0.0, NEG)
    has_cmp = jnp.where(t_col >= CMP_BLK - 1, 1.0, 0.0)
    w_start = pl.multiple_of(jnp.maximum(t0 - WIN, 0), qb)
    wk = WIN + qb
    kp = w_start + lax.broadcasted_iota(jnp.int32, (1, wk), 1)
    bias_w = jnp.where((kp <= t_col) & (kp > t_col - WIN), 0.0, NEG)
    n_full = t0 // SEL_KB

    def with_bias(s, bias):
        return (s.reshape(rep, qb, s.shape[-1]) + bias[None]).reshape(s.shape)

    groups = ((kc0_ref, vc0_ref, ks0_ref, vs0_ref, kw0_ref, vw0_ref),
              (kc1_ref, vc1_ref, ks1_ref, vs1_ref, kw1_ref, vw1_ref))
    o_cs, acc_ws = [], []
    for g, (kc_ref, vc_ref, ks_ref, vs_ref, kw_ref, vw_ref) in enumerate(groups):
        q_heads = [q_ref[0, :, (g * rep + r) * LANES:(g * rep + r + 1) * LANES] for r in range(rep)]
        qg = jnp.concatenate(q_heads, axis=0)

        s = with_bias(_dot_nt(qg, kc_ref[0]), bias_c)
        m = jnp.max(s, axis=-1, keepdims=True)
        p = (jnp.exp2(s - m).reshape(rep, qb, ncmp) * has_cmp[None]).reshape(rows, ncmp)
        l = jnp.sum(p, axis=-1, keepdims=True)
        p_c = p / jnp.where(l > 0.0, l, 1.0)
        o_cs.append(_dot(_bf(p_c), vc_ref[0]))
        p_sum = p_c[0:qb] + p_c[qb:2 * qb] + p_c[2 * qb:3 * qb] + p_c[3 * qb:4 * qb]
        imp = jnp.dot(p_sum, ovl_ref[...], preferred_element_type=jnp.float32,
                      precision=lax.Precision.HIGHEST)
        sel = _select_blocks(imp, t_col, seq // SLC_BLK)
        sel_bias = _bf(jnp.where(sel, 0.0, NEG))
        for r, qh in enumerate(q_heads):
            qa_scr[g, r * qb:(r + 1) * qb, :] = jnp.where(lane_low, qh, sel_bias)
        m_scr[g] = jnp.full((rows, LANES), NEG, jnp.float32)
        acc_scr[g] = jnp.zeros((rows, LANES), jnp.float32)

    def chunk_start(j):
        return pl.multiple_of(jnp.minimum(j * SEL_KB, seq - SEL_KB), SEL_KB)

    def produce(j, dst_scr, dst_max_scr):
        k0 = chunk_start(j)
        for g in range(NSA_GROUPS):
            s = _dot_nt(qa_scr[g], groups[g][2][0, pl.ds(k0, SEL_KB), :])
            dst_scr[g] = s
            dst_max_scr[g] = jnp.broadcast_to(jnp.max(s, axis=-1, keepdims=True), (rows, LANES))

    def consume(j, src_scr, src_max_scr, causal):
        k0 = chunk_start(j)
        if causal:
            tok = j * SEL_KB + lax.broadcasted_iota(jnp.int32, (1, SEL_KB), 1)
            bias = jnp.where(tok <= t_col, 0.0, NEG)
        for g in range(NSA_GROUPS):
            vs_ref = groups[g][3]
            m = m_scr[g]
            if causal:
                s = with_bias(src_scr[g], bias)
                m_new = jnp.maximum(m, jnp.max(s, axis=-1, keepdims=True))
            else:
                s = src_scr[g]
                m_new = jnp.maximum(m, src_max_scr[g])
            p = jnp.exp2(s - jnp.concatenate([m_new] * (SEL_KB // LANES), axis=1))
            pv = _dot(_bf(p), vs_ref[0, pl.ds(k0, SEL_KB), :])
            acc_scr[g] = jnp.exp2(m - m_new) * acc_scr[g] + pv
            m_scr[g] = m_new

    def loop_body(i, carry):
        produce(2 * i + 1, sb_scr, mxb_scr)
        consume(2 * i, sa_scr, mxa_scr, False)
        produce(2 * i + 2, sa_scr, mxa_scr)
        consume(2 * i + 1, sb_scr, mxb_scr, False)
        return carry

    n_pairs = n_full // 2
    produce(0, sa_scr, mxa_scr)
    lax.fori_loop(0, n_pairs, loop_body, 0)
    produce(2 * n_pairs + 1, sb_scr, mxb_scr)
    consume(2 * n_pairs, sa_scr, mxa_scr, True)

    consume(2 * n_pairs + 1, sb_scr, mxb_scr, True)
    carry = (None, acc_scr[0], None, acc_scr[1])

    for g in range(NSA_GROUPS):
        kw_ref, vw_ref = groups[g][4], groups[g][5]
        qg = jnp.concatenate(
            [q_ref[0, :, (g * rep + r) * LANES:(g * rep + r + 1) * LANES] for r in range(rep)], axis=0)
        s = with_bias(_dot_nt(qg, kw_ref[0, pl.ds(w_start, wk), :]), bias_w)
        m = jnp.max(s, axis=-1, keepdims=True)
        acc_ws.append(_dot(_bf(jnp.exp2(s - m)), vw_ref[0, pl.ds(w_start, wk), :]))

    def normalized(acc):
        return acc / pltpu.roll(acc, NSA_DH, axis=1)

    for g in range(NSA_GROUPS):
        o_c, o_s, o_w = o_cs[g], normalized(carry[2 * g + 1]), normalized(acc_ws[g])
        for r in range(rep):
            hd = g * rep + r
            sl = slice(r * qb, (r + 1) * qb)
            out = (o_c[sl] * gates[:, 3 * hd:3 * hd + 1] + o_s[sl] * gates[:, 3 * hd + 1:3 * hd + 2]
                   + o_w[sl] * gates[:, 3 * hd + 2:3 * hd + 3])
            y_ref[0, :, hd * LANES:(hd + 1) * LANES] = _bf(jnp.where(lane_low, out, 0.0))


def _nsa_attn(q_rot, kcs, kvs, z3, ovl):
    bsz, t, qw = q_rot.shape
    ncmp = kcs[0].shape[1]
    assert t % SEL_KB == 0 and t >= WIN + Q_BLOCK and t // SLC_BLK <= LANES - NSA_DH
    gblk = C_NGATE // LANES
    per_b = lambda rows: pl.BlockSpec((1, rows, LANES), lambda b, c: (b, 0, 0))
    return pl.pallas_call(
        functools.partial(_nsa_attn_kernel, seq=t),
        grid=(bsz, t // Q_BLOCK),
        in_specs=[pl.BlockSpec((1, Q_BLOCK, qw), lambda b, c: (b, c, 0))]
        + [per_b(ncmp)] * 4 + [per_b(t)] * 8
        + [pl.BlockSpec((1, Q_BLOCK, LANES), lambda b, c: (b, c, gblk)),
           pl.BlockSpec(ovl.shape, lambda b, c: (0, 0))],
        out_specs=pl.BlockSpec((1, Q_BLOCK, qw), lambda b, c: (b, c, 0)),
        out_shape=jax.ShapeDtypeStruct((bsz, t, qw), jnp.bfloat16),
        scratch_shapes=[pltpu.VMEM((NSA_GROUPS, NSA_REP * Q_BLOCK, LANES), jnp.bfloat16),
                        pltpu.VMEM((NSA_GROUPS, NSA_REP * Q_BLOCK, LANES), jnp.float32),
                        pltpu.VMEM((NSA_GROUPS, NSA_REP * Q_BLOCK, LANES), jnp.float32),
                        pltpu.VMEM((NSA_GROUPS, NSA_REP * Q_BLOCK, SEL_KB), jnp.float32),
                        pltpu.VMEM((NSA_GROUPS, NSA_REP * Q_BLOCK, SEL_KB), jnp.float32),
                        pltpu.VMEM((NSA_GROUPS, NSA_REP * Q_BLOCK, LANES), jnp.float32),
                        pltpu.VMEM((NSA_GROUPS, NSA_REP * Q_BLOCK, LANES), jnp.float32)],
        compiler_params=pltpu.CompilerParams(
            dimension_semantics=("parallel", "arbitrary"), vmem_limit_bytes=VMEM_LIMIT),
        name="nsa_attn",
    )(q_rot, *kcs, *kvs, z3, ovl)


def _merge_kernel(x_ref, yh_ref, yg_ref, yn_ref, mg_ref, wh_ref, wg_ref, wn_ref, wo_ref, o_ref):
    d = D_MODEL
    m = _sigmoid(_f32(mg_ref[:, 0:d])) * _dot(yh_ref[...], wh_ref[...])
    m = m + _sigmoid(_f32(mg_ref[:, d:2 * d])) * _dot(yg_ref[...], wg_ref[...])
    m = m + _sigmoid(_f32(mg_ref[:, 2 * d:3 * d])) * _dot(yn_ref[...], wn_ref[...])
    o_ref[...] = x_ref[...] + _dot(_bf(m), wo_ref[...])


def _merge(x2, yh, yg, yn, z2, wh, wg, wn, wo, tm=512):
    n, d = x2.shape
    mgblk = C_MG // (3 * d)
    row = lambda w: pl.BlockSpec((tm, w), lambda i: (i, 0))
    full = lambda a: pl.BlockSpec(a.shape, lambda i: (0, 0))
    return pl.pallas_call(
        _merge_kernel,
        grid=(n // tm,),
        in_specs=[row(d), row(yh.shape[1]), row(yg.shape[1]), row(yn.shape[1]),
                  pl.BlockSpec((tm, 3 * d), lambda i: (i, mgblk)),
                  full(wh), full(wg), full(wn), full(wo)],
        out_specs=row(d),
        out_shape=jax.ShapeDtypeStruct((n, d), jnp.float32),
        compiler_params=pltpu.CompilerParams(
            dimension_semantics=("parallel",), vmem_limit_bytes=VMEM_LIMIT),
        name="merge",
    )(x2, yh, yg, yn, z2, wh, wg, wn, wo)


def _ffn_kernel(x_ref, g_ref, wg_ref, wu_ref, wd_ref, fg_ref, o_ref, h_scr, acc_scr, *, final_norm):
    j = pl.program_id(1)

    @pl.when(j == 0)
    def _():
        x = x_ref[...]
        y = x * lax.rsqrt(jnp.mean(x * x, axis=-1, keepdims=True) + EPS)
        h_scr[...] = _bf(y * g_ref[...])
        acc_scr[...] = jnp.zeros_like(acc_scr)

    h = h_scr[...]
    a = _dot(h, wg_ref[...])
    u = _dot(h, wu_ref[...])
    acc_scr[...] += _dot(_bf(a * _sigmoid(a) * u), wd_ref[...])

    @pl.when(j == pl.num_programs(1) - 1)
    def _():
        out = x_ref[...] + acc_scr[...]
        if final_norm:
            out = out * lax.rsqrt(jnp.mean(out * out, axis=-1, keepdims=True) + EPS) * fg_ref[...]
        o_ref[...] = out


def _ffn(x2, g, wg, wu, wd, fg, final_norm, tm=1024, tf=256):
    n, d = x2.shape
    ff = wg.shape[1]
    return pl.pallas_call(
        functools.partial(_ffn_kernel, final_norm=final_norm),
        grid=(n // tm, ff // tf),
        in_specs=[pl.BlockSpec((tm, d), lambda i, j: (i, 0)),
                  pl.BlockSpec((1, d), lambda i, j: (0, 0)),
                  pl.BlockSpec((d, tf), lambda i, j: (0, j)),
                  pl.BlockSpec((d, tf), lambda i, j: (0, j)),
                  pl.BlockSpec((tf, d), lambda i, j: (j, 0)),
                  pl.BlockSpec((1, d), lambda i, j: (0, 0))],
        out_specs=pl.BlockSpec((tm, d), lambda i, j: (i, 0)),
        out_shape=jax.ShapeDtypeStruct((n, d), jnp.float32),
        scratch_shapes=[pltpu.VMEM((tm, d), jnp.bfloat16), pltpu.VMEM((tm, d), jnp.float32)],
        compiler_params=pltpu.CompilerParams(
            dimension_semantics=("parallel", "arbitrary"), vmem_limit_bytes=VMEM_LIMIT),
        name="ffn",
    )(x2, g, wg, wu, wd, fg)


def _pad_heads(w, heads, width):
    lead = w.shape[:-1]
    w = w.reshape(lead + (heads, width))
    w = jnp.pad(w, [(0, 0)] * len(lead) + [(0, 0), (0, LANES - width)])
    return w.reshape(lead + (heads * LANES,))


def _layout_w_in(w):
    d = w.shape[0]
    o = 0
    parts = {}
    for name, n in (("hq", 512), ("hf", 512), ("hi", 512), ("hog", 512), ("gq", 256), ("gk", 256),
                    ("gv", 512), ("glr", 16), ("gog", 512), ("nq", 512), ("nkc", 128), ("nvc", 128),
                    ("nks", 128), ("nvs", 128), ("nkw", 128), ("nvw", 128), ("ngate", 24), ("mg", 3072)):
        parts[name] = w[:, o:o + n]
        o += n
    nq = _pad_heads(parts["nq"], NSA_HEADS, NSA_DH)
    padc = lambda a: jnp.pad(a, ((0, 0), (0, LANES - a.shape[1])))
    cols = [parts["hq"], parts["hf"], parts["hi"], parts["hog"],
            _pad_heads(parts["gq"], GLA_HEADS, GLA_DK), _pad_heads(parts["gk"], GLA_HEADS, GLA_DK),
            parts["gv"], parts["gog"], nq,
            parts["nkc"], parts["nvc"], parts["nks"], parts["nvs"], parts["nkw"], parts["nvw"],
            padc(parts["ngate"]), padc(parts["glr"]), parts["mg"]]
    out = jnp.concatenate(cols, axis=1)
    assert out.shape[1] == Z_COLS
    return _bf(out)


def _blockdiag2(w):
    z = jnp.zeros_like(w)
    top = jnp.concatenate([w, z], axis=-1)
    bot = jnp.concatenate([z, w], axis=-1)
    return jnp.concatenate([top, bot], axis=-2)


def _rope_tables(t):
    pos = jnp.arange(t, dtype=jnp.float32)
    inv_freq = ROPE_THETA ** (-jnp.arange(0, ROT_DIM, 2, dtype=jnp.float32) / ROT_DIM)
    ang = pos[:, None] * inv_freq[None, :]
    cos, sin = jnp.cos(ang), jnp.sin(ang)
    half = ROT_DIM // 2
    lane = jnp.arange(LANES) % NSA_DH
    first = lane < half
    second = (lane >= half) & (lane < ROT_DIM)
    idx = jnp.where(first, lane, jnp.where(second, lane - half, 0))
    cos_t = jnp.where(first | second, cos[:, idx], 1.0)
    sin_a = jnp.where(first, -sin[:, idx], 0.0)
    sin_b = jnp.where(second, sin[:, idx], 0.0)
    return cos_t, sin_a, sin_b


def kernel(x, norm1_g, w_in, hg_lb, hg_norm_g, gla_w2, gla_b, gla_norm_g, cmp_pos_k, cmp_pos_v,
           cmp_w1_k, cmp_w2_k, cmp_w1_v, cmp_w2_v, w_br_hg, w_br_gla, w_br_nsa, w_out,
           norm2_g, w_ffn_gate, w_ffn_up, w_ffn_down, final_norm_g):
    bsz, t, d = x.shape
    depth = w_in.shape[0]
    n = bsz * t
    f32 = jnp.float32

    cos_t, sin_a, sin_b = _rope_tables(t)
    lbs = jnp.cumsum(jax.nn.softmax(hg_lb.astype(f32), axis=0), axis=0)
    lbs = lbs - lbs[0]

    nslc = t // SLC_BLK
    ncmp_pad = t // CMP_STRIDE
    cmp_start = jnp.arange(ncmp_pad) * CMP_STRIDE
    blk = jnp.arange(LANES) - NSA_DH
    ovl = ((cmp_start[:, None] < (blk[None, :] + 1) * SLC_BLK)
           & (cmp_start[:, None] + CMP_BLK - 1 >= blk[None, :] * SLC_BLK)
           & (blk[None, :] >= 0) & (blk[None, :] < nslc)
           & (jnp.arange(ncmp_pad)[:, None] < ncmp_pad - 1)).astype(f32)

    x2 = x.reshape(n, d)
    for l in range(depth):
        z2 = _inproj(x2, norm1_g[l][None, :], _layout_w_in(w_in[l]))
        z3 = z2.reshape(bsz, t, Z_COLS)

        lb = lbs[l].reshape(HG_HEADS, LANES)
        zeros = jnp.zeros_like(lb)
        hg_par = jnp.stack([jnp.log(lb), jnp.log1p(-lb), 1.0 - lb,
                            hg_norm_g[l].reshape(HG_HEADS, LANES), zeros, zeros, zeros, zeros], axis=1)
        y_hg = _hgrn2(z3, hg_par)

        w2p = jnp.pad(_pad_heads(gla_w2[l], GLA_HEADS, GLA_DK), ((0, LANES - GLA_RANK), (0, 0)))
        w2p = _bf(w2p.reshape(LANES, GLA_HEADS, LANES).transpose(1, 0, 2))
        gb = _pad_heads(gla_b[l], GLA_HEADS, GLA_DK).reshape(GLA_HEADS, LANES)
        gla_par = jnp.stack([gb, gla_norm_g[l].reshape(GLA_HEADS, LANES),
                             zeros, zeros, zeros, zeros, zeros, zeros], axis=1)
        y_gla = _gla(z3, w2p, gla_par)

        q_rot, kc_rot, vc_raw, *kvs = _nsa_prep(z3, cos_t, sin_a, sin_b)
        tile2 = lambda p: jnp.concatenate([p, p], axis=1)
        w1k = _bf(_blockdiag2(cmp_w1_k[l].reshape(CMP_BLK, NSA_DH, CMP_HID)))
        w1v = _bf(_blockdiag2(cmp_w1_v[l].reshape(CMP_BLK, NSA_DH, CMP_HID)))
        kcs = _compress(kc_rot, vc_raw, tile2(cmp_pos_k[l]), tile2(cmp_pos_v[l]),
                        w1k, _bf(_blockdiag2(cmp_w2_k[l])), w1v, _bf(_blockdiag2(cmp_w2_v[l])))
        y_nsa = _nsa_attn(q_rot, kcs, kvs, z3, ovl)

        wn = _pad_heads(w_br_nsa[l].T, NSA_HEADS, NSA_DH).T
        x2 = _merge(x2, y_hg.reshape(n, HG_DIM), y_gla.reshape(n, GLA_VDIM),
                    y_nsa.reshape(n, NSA_HEADS * LANES), z2,
                    _bf(w_br_hg[l]), _bf(w_br_gla[l]), _bf(wn), _bf(w_out[l]))

        x2 = _ffn(x2, norm2_g[l][None, :], _bf(w_ffn_gate[l]), _bf(w_ffn_up[l]), _bf(w_ffn_down[l]),
                  final_norm_g[None, :], final_norm=(l == depth - 1))
    return x2.reshape(bsz, t, d)
```

```python
import functools

import jax
import jax.numpy as jnp
from jax import lax
from jax.experimental import pallas as pl
from jax.experimental.pallas import tpu as pltpu

EPS = 1e-6
D_MODEL = 1024
LANES = 128

HG_HEADS = 4
HG_DIM = 512
GLA_HEADS = 4
GLA_DK = 64
GLA_KDIM = 256
GLA_VDIM = 512
GLA_RANK = 16
GLA_TAU = 16.0
LA_CHUNK = 64
LA_SUB = 16

NSA_HEADS = 8
NSA_GROUPS = 2
NSA_REP = 4
NSA_DH = 64
CMP_STRIDE = 16
CMP_BLK = 32
CMP_HID = 128
SLC_BLK = 64
N_SEL = 16
WIN = 512
Q_BLOCK = 128
ROPE_THETA = 500000.0
ROT_DIM = 16
FF_DIM = 2816

C_HQ, C_HF, C_HI, C_HOG = 0, 512, 1024, 1536
C_GQ, C_GK, C_GV, C_GOG = 2048, 2560, 3072, 3584
C_NQ = 4096
C_NKC, C_NVC, C_NKS, C_NVS, C_NKW, C_NVW = 5120, 5248, 5376, 5504, 5632, 5760
C_NGATE, C_GLR = 5888, 6016
C_MG = 6144
Z_COLS = 9216

VMEM_LIMIT = 56 * 1024 * 1024
LOG2E = 1.4426950408889634

_NT = (((1,), (1,)), ((), ()))
_TN = (((0,), (0,)), ((), ()))


def _bf(x):
    return x.astype(jnp.bfloat16)


def _dot(a, b):
    return jnp.dot(a, b, preferred_element_type=jnp.float32)


def _dot_nt(a, b):
    return lax.dot_general(a, b, _NT, preferred_element_type=jnp.float32)


def _dot_tn(a, b):
    return lax.dot_general(a, b, _TN, preferred_element_type=jnp.float32)


def _f32(x):
    return x.astype(jnp.float32)


def _sigmoid(x):
    return 1.0 / (1.0 + jnp.exp(-x))


def _softplus_neg_abs(x):
    return jnp.log(1.0 + jnp.exp(-jnp.abs(x)))


def _log_sigmoid(x):
    return jnp.minimum(x, 0.0) - _softplus_neg_abs(x)


def _inproj_kernel(x_ref, g_ref, w_ref, z_ref, h_scr):
    @pl.when(pl.program_id(1) == 0)
    def _():
        x = x_ref[...]
        y = x * lax.rsqrt(jnp.mean(x * x, axis=-1, keepdims=True) + EPS)
        h_scr[...] = _bf(y * g_ref[...])

    z_ref[...] = _dot(h_scr[...], w_ref[...]).astype(z_ref.dtype)


def _inproj(x2, g, w, tm=1024, tn=3072):
    n, d = x2.shape
    cols = w.shape[1]
    return pl.pallas_call(
        _inproj_kernel,
        grid=(n // tm, cols // tn),
        in_specs=[
            pl.BlockSpec((tm, d), lambda i, j: (i, 0)),
            pl.BlockSpec((1, d), lambda i, j: (0, 0)),
            pl.BlockSpec((d, tn), lambda i, j: (0, j)),
        ],
        out_specs=pl.BlockSpec((tm, tn), lambda i, j: (i, j)),
        out_shape=jax.ShapeDtypeStruct((n, cols), jnp.bfloat16),
        scratch_shapes=[pltpu.VMEM((tm, d), jnp.bfloat16)],
        compiler_params=pltpu.CompilerParams(
            dimension_semantics=("parallel", "arbitrary"), vmem_limit_bytes=VMEM_LIMIT),
        name="inproj",
    )(x2, g, w)


def _chunk_cumsum(g, tb):
    pos = lax.broadcasted_iota(jnp.int32, (tb, 1), 0) % LA_CHUNK
    b = g
    shift = 1
    while shift < LA_CHUNK:
        b = b + jnp.where(pos >= shift, pltpu.roll(b, shift, axis=0), 0.0)
        shift *= 2
    return b


def _la_core(q, k, v, g, st_ref, tb):
    c, r = LA_CHUNK, LA_SUB
    n = tb // c
    nsub = c // r
    dk = q.shape[-1]
    b = _chunk_cumsum(g, tb)
    b3 = b.reshape(n, c, dk)
    q3 = q.reshape(n, c, dk)
    k3 = k.reshape(n, c, dk)
    v3 = _bf(v).reshape(n, c, dk)
    blast = b3[:, c - 1:c, :]
    q_state = _bf(q3 * jnp.exp(b3))
    k_state = _bf(k3 * jnp.exp(blast - b3))

    b4 = b.reshape(n * nsub, r, dk)
    ref = b4[:, 0:1, :]
    q_loc = _bf(q.reshape(n * nsub, r, dk) * jnp.exp(b4 - ref))
    s_idx = lax.broadcasted_iota(jnp.int32, (1, nsub, c, 1), 2)
    i_idx = lax.broadcasted_iota(jnp.int32, (1, nsub, c, 1), 1)
    e = jnp.where(s_idx < r * (i_idx + 1), ref.reshape(n, nsub, 1, dk) - b3[:, None, :, :], 0.0)
    k_loc = _bf(k3[:, None, :, :] * jnp.exp(e)).reshape(n * nsub, c, dk)
    a = jnp.einsum("utd,usd->uts", q_loc, k_loc, preferred_element_type=jnp.float32)
    a = a.reshape(n, c, c)
    t_i = lax.broadcasted_iota(jnp.int32, (1, c, c), 1)
    s_i = lax.broadcasted_iota(jnp.int32, (1, c, c), 2)
    a = jnp.where(s_i <= t_i, a, 0.0)
    o_intra = jnp.einsum("nts,nsv->ntv", _bf(a), v3, preferred_element_type=jnp.float32)

    st = st_ref[...]
    outs = []
    for ci in range(n):
        outs.append(_dot_nt(q_state[ci], _bf(st)) + o_intra[ci])
        st = st * jnp.exp(blast[ci]) + _dot_tn(v3[ci], k_state[ci])
    st_ref[...] = st
    return jnp.concatenate(outs, axis=0)


def _la_finish(o, og, gain, y_ref):
    o = o * lax.rsqrt(jnp.mean(o * o, axis=-1, keepdims=True) + EPS)
    y_ref[0] = _bf(o * gain * (og * _sigmoid(og)))


def _hgrn2_kernel(q_ref, f_ref, i_ref, og_ref, par_ref, y_ref, st_ref, *, tb):
    @pl.when(pl.program_id(2) == 0)
    def _():
        st_ref[...] = jnp.zeros_like(st_ref)

    log_lb = par_ref[0, 0:1, :]
    log1m_lb = par_ref[0, 1:2, :]
    one_m_lb = par_ref[0, 2:3, :]
    gain = par_ref[0, 3:4, :]
    hq = _f32(q_ref[0])
    hf = _f32(f_ref[0])
    q = hq * _sigmoid(hq)
    e = jnp.exp(-jnp.abs(hf))
    cterm = log1m_lb + (jnp.minimum(hf, 0.0) - jnp.log(1.0 + e))
    log_f = jnp.maximum(log_lb, cterm) + _softplus_neg_abs(log_lb - cterm)
    k = one_m_lb * (jnp.where(hf >= 0.0, e, 1.0) / (1.0 + e))
    o = _la_core(q, k, _f32(i_ref[0]), log_f, st_ref, tb)
    _la_finish(o, _f32(og_ref[0]), gain, y_ref)


def _gla_kernel(q_ref, k_ref, v_ref, og_ref, lr_ref, w2_ref, par_ref, y_ref, st_ref, *, tb):
    @pl.when(pl.program_id(2) == 0)
    def _():
        st_ref[...] = jnp.zeros_like(st_ref)

    bias = par_ref[0, 0:1, :]
    gain = par_ref[0, 1:2, :]
    zz = _dot(_bf(lr_ref[0]), w2_ref[0]) + bias
    log_a = _log_sigmoid(zz) * (1.0 / GLA_TAU)
    q = _f32(q_ref[0]) * (GLA_DK ** -0.5)
    o = _la_core(q, _f32(k_ref[0]), _f32(v_ref[0]), log_a, st_ref, tb)
    _la_finish(o, _f32(og_ref[0]), gain, y_ref)


def _zspec(tb, col0):
    base = col0 // LANES
    return pl.BlockSpec((1, tb, LANES), lambda b, h, t: (b, t, base + h))


def _la_params():
    return pltpu.CompilerParams(
        dimension_semantics=("parallel", "parallel", "arbitrary"), vmem_limit_bytes=VMEM_LIMIT)


def _hgrn2(z3, par, tb=4096):
    bsz, t, _ = z3.shape
    return pl.pallas_call(
        functools.partial(_hgrn2_kernel, tb=tb),
        grid=(bsz, HG_HEADS, t // tb),
        in_specs=[_zspec(tb, C_HQ), _zspec(tb, C_HF), _zspec(tb, C_HI), _zspec(tb, C_HOG),
                  pl.BlockSpec((1, 8, LANES), lambda b, h, t: (h, 0, 0))],
        out_specs=pl.BlockSpec((1, tb, LANES), lambda b, h, t: (b, t, h)),
        out_shape=jax.ShapeDtypeStruct((bsz, t, HG_DIM), jnp.bfloat16),
        scratch_shapes=[pltpu.VMEM((LANES, LANES), jnp.float32)],
        compiler_params=_la_params(),
        name="hgrn2",
    )(z3, z3, z3, z3, par)


def _gla(z3, w2p, par, tb=4096):
    bsz, t, _ = z3.shape
    lr_blk = C_GLR // LANES
    return pl.pallas_call(
        functools.partial(_gla_kernel, tb=tb),
        grid=(bsz, GLA_HEADS, t // tb),
        in_specs=[_zspec(tb, C_GQ), _zspec(tb, C_GK), _zspec(tb, C_GV), _zspec(tb, C_GOG),
                  pl.BlockSpec((1, tb, LANES), lambda b, h, t: (b, t, lr_blk)),
                  pl.BlockSpec((1, LANES, LANES), lambda b, h, t: (h, 0, 0)),
                  pl.BlockSpec((1, 8, LANES), lambda b, h, t: (h, 0, 0))],
        out_specs=pl.BlockSpec((1, tb, LANES), lambda b, h, t: (b, t, h)),
        out_shape=jax.ShapeDtypeStruct((bsz, t, GLA_VDIM), jnp.bfloat16),
        scratch_shapes=[pltpu.VMEM((LANES, LANES), jnp.float32)],
        compiler_params=_la_params(),
        name="gla",
    )(z3, z3, z3, z3, z3, w2p, par)


def _rope(x, cos_t, sin_a, sin_b):
    half = ROT_DIM // 2
    slabs = []
    for i in range(x.shape[-1] // LANES):
        xs = x[:, i * LANES:(i + 1) * LANES]
        slabs.append(xs * cos_t + pltpu.roll(xs, LANES - half, axis=1) * sin_a
                     + pltpu.roll(xs, half, axis=1) * sin_b)
    return slabs[0] if len(slabs) == 1 else jnp.concatenate(slabs, axis=1)


def _per_group(x, filler):
    swapped = pltpu.roll(x, NSA_DH, axis=1)
    if filler is None:
        return _bf(x), _bf(swapped)
    low = lax.broadcasted_iota(jnp.int32, x.shape, 1) < NSA_DH
    return _bf(jnp.where(low, x, filler)), _bf(jnp.where(low, swapped, filler))


def _nsa_prep_kernel(q_ref, kc_ref, vc_ref, ks_ref, vs_ref, kw_ref, vw_ref, cos_ref, sa_ref, sb_ref,
                     qo_ref, kco_ref, vco_ref, ks0_ref, ks1_ref, vs0_ref, vs1_ref, kw0_ref, kw1_ref,
                     vw0_ref, vw1_ref, *, tt):
    ct, sa, sb = cos_ref[...], sa_ref[...], sb_ref[...]
    qo_ref[0] = _bf(_rope(_f32(q_ref[0]), ct, sa, sb) * (NSA_DH ** -0.5 * LOG2E))
    kco_ref[0] = _rope(_f32(kc_ref[0]), ct, sa, sb)
    vco_ref[0] = _f32(vc_ref[0])
    tok = pl.program_id(1) * tt + lax.broadcasted_iota(jnp.int32, (tt, LANES), 0)
    lane = lax.broadcasted_iota(jnp.int32, (tt, LANES), 1)
    onehot = jnp.where(lane - NSA_DH == tok // SLC_BLK, 1.0, 0.0)
    ks0_ref[0], ks1_ref[0] = _per_group(_rope(_f32(ks_ref[0]), ct, sa, sb), onehot)
    kw0_ref[0], kw1_ref[0] = _per_group(_rope(_f32(kw_ref[0]), ct, sa, sb), None)
    vs0_ref[0], vs1_ref[0] = _per_group(_f32(vs_ref[0]), 1.0)
    vw0_ref[0], vw1_ref[0] = _per_group(_f32(vw_ref[0]), 1.0)


def _nsa_prep(z3, cos_t, sin_a, sin_b, tt=512):
    bsz, t, _ = z3.shape
    qw = NSA_HEADS * LANES

    def zs(col0, width):
        blk = col0 // width
        return pl.BlockSpec((1, tt, width), lambda b, i: (b, i, blk))

    def os_(width):
        return pl.BlockSpec((1, tt, width), lambda b, i: (b, i, 0))

    tab = pl.BlockSpec((tt, LANES), lambda b, i: (i, 0))
    kv_bf = jax.ShapeDtypeStruct((bsz, t, LANES), jnp.bfloat16)
    return pl.pallas_call(
        functools.partial(_nsa_prep_kernel, tt=tt),
        grid=(bsz, t // tt),
        in_specs=[zs(C_NQ, qw), zs(C_NKC, LANES), zs(C_NVC, LANES), zs(C_NKS, LANES),
                  zs(C_NVS, LANES), zs(C_NKW, LANES), zs(C_NVW, LANES), tab, tab, tab],
        out_specs=[os_(qw), os_(LANES), os_(LANES)] + [os_(LANES)] * 8,
        out_shape=[jax.ShapeDtypeStruct((bsz, t, qw), jnp.bfloat16),
                   jax.ShapeDtypeStruct((bsz, t, LANES), jnp.float32),
                   jax.ShapeDtypeStruct((bsz, t, LANES), jnp.float32)] + [kv_bf] * 8,
        compiler_params=pltpu.CompilerParams(
            dimension_semantics=("parallel", "parallel"), vmem_limit_bytes=VMEM_LIMIT),
        name="nsa_prep",
    )(z3, z3, z3, z3, z3, z3, z3, cos_t, sin_a, sin_b)


def _compress_one(src_ref, pos_ref, w1_ref, w2_ref, out0_ref, out1_ref, ncmp):
    half = CMP_BLK // 2
    u = jnp.zeros((ncmp, 2 * CMP_HID), jnp.float32)
    v = jnp.zeros((ncmp, 2 * CMP_HID), jnp.float32)
    for j in range(half):
        rows = src_ref[0, pl.ds(j, ncmp, stride=CMP_STRIDE), :]
        u = u + _dot(_bf(rows + pos_ref[j:j + 1, :]), w1_ref[j])
        v = v + _dot(_bf(rows + pos_ref[half + j:half + j + 1, :]), w1_ref[half + j])
    hid = u + pltpu.roll(v, ncmp - 1, axis=0)
    out0_ref[0], out1_ref[0] = _per_group(_dot(_bf(jax.nn.gelu(hid)), w2_ref[...]), None)


def _compress_kernel(k_ref, v_ref, pk_ref, pv_ref, w1k_ref, w2k_ref, w1v_ref, w2v_ref,
                     kc0_ref, kc1_ref, vc0_ref, vc1_ref, *, ncmp):
    _compress_one(k_ref, pk_ref, w1k_ref, w2k_ref, kc0_ref, kc1_ref, ncmp)
    _compress_one(v_ref, pv_ref, w1v_ref, w2v_ref, vc0_ref, vc1_ref, ncmp)


def _compress(kc_rot, vc_raw, pos_k, pos_v, w1k, w2k, w1v, w2v):
    bsz, t, _ = kc_rot.shape
    ncmp = t // CMP_STRIDE
    full = lambda shape: pl.BlockSpec(shape, lambda b: (0,) * len(shape))
    out = jax.ShapeDtypeStruct((bsz, ncmp, LANES), jnp.bfloat16)
    return pl.pallas_call(
        functools.partial(_compress_kernel, ncmp=ncmp),
        grid=(bsz,),
        in_specs=[pl.BlockSpec((1, t, LANES), lambda b: (b, 0, 0)),
                  pl.BlockSpec((1, t, LANES), lambda b: (b, 0, 0)),
                  full(pos_k.shape), full(pos_v.shape), full(w1k.shape), full(w2k.shape),
                  full(w1v.shape), full(w2v.shape)],
        out_specs=[pl.BlockSpec((1, ncmp, LANES), lambda b: (b, 0, 0))] * 4,
        out_shape=[out] * 4,
        compiler_params=pltpu.CompilerParams(
            dimension_semantics=("parallel",), vmem_limit_bytes=VMEM_LIMIT),
        name="nsa_compress",
    )(kc_rot, vc_raw, pos_k, pos_v, w1k, w2k, w1v, w2v)


NEG = -1e30
SEL_KB = 256


def _select_blocks(imp, t_col, nslc):
    qb = imp.shape[0]
    blk = lax.broadcasted_iota(jnp.int32, (qb, LANES), 1) - NSA_DH
    cur = t_col // SLC_BLK
    valid = (blk >= 0) & (blk <= cur)
    forced = (blk == 0) | (blk == cur) | (blk == cur - 1)
    score = jnp.where(valid & forced, 1e9, jnp.where(valid, imp, -1e9))
    sc_t = score.T
    sub = 8
    nslab = (LANES - NSA_DH) // sub
    slabs = [sc_t[NSA_DH + sub * k:NSA_DH + sub * (k + 1), :] for k in range(nslab)]
    jj = lax.broadcasted_iota(jnp.int32, (sub, qb), 0)
    ranks = [jnp.zeros((sub, qb), jnp.float32) for _ in range(nslab)]
    for i in range(nslc):
        row = sc_t[NSA_DH + i:NSA_DH + i + 1, :]
        for k in range(nslab):
            if sub * k > i:
                ahead = row >= slabs[k]
            elif sub * (k + 1) <= i:
                ahead = row > slabs[k]
            else:
                ahead = (row > slabs[k]) | ((row == slabs[k]) & (jj + sub * k > i))
            ranks[k] = ranks[k] + jnp.where(ahead, 1.0, 0.0)
    sel_t = jnp.concatenate(
        [jnp.zeros((NSA_DH, qb), jnp.float32)]
        + [jnp.where(r < float(N_SEL), 1.0, 0.0) for r in ranks], axis=0)
    return valid & (sel_t.T > 0.5)


def _nsa_attn_kernel(q_ref, kc0_ref, kc1_ref, vc0_ref, vc1_ref, ks0_ref, ks1_ref, vs0_ref, vs1_ref,
                     kw0_ref, kw1_ref, vw0_ref, vw1_ref, gate_ref, ovl_ref, y_ref,
                     qa_scr, m_scr, acc_scr, sa_scr, sb_scr, mxa_scr, mxb_scr, *, seq):
    c = pl.program_id(1)
    qb = Q_BLOCK
    rep = NSA_REP
    rows = rep * qb
    t0 = c * qb
    t_col = t0 + lax.broadcasted_iota(jnp.int32, (qb, 1), 0)
    gates = _sigmoid(_f32(gate_ref[0]))
    ncmp = kc0_ref.shape[1]
    lane_low = lax.broadcasted_iota(jnp.int32, (qb, LANES), 1) < NSA_DH
    n_idx = lax.broadcasted_iota(jnp.int32, (1, ncmp), 1)
    bias_c = jnp.where((n_idx * CMP_STRIDE + (CMP_BLK - 1)) <= t_col, 0.0, NEG)
    has_cmp = jnp.where(t_col >= CMP_BLK - 1, 1.0, 0.0)
    w_start = pl.multiple_of(jnp.maximum(t0 - WIN, 0), qb)
    wk = WIN + qb
    kp = w_start + lax.broadcasted_iota(jnp.int32, (1, wk), 1)
    bias_w = jnp.where((kp <= t_col) & (kp > t_col - WIN), 0.0, NEG)
    n_full = t0 // SEL_KB

    def with_bias(s, bias):
        return (s.reshape(rep, qb, s.shape[-1]) + bias[None]).reshape(s.shape)

    groups = ((kc0_ref, vc0_ref, ks0_ref, vs0_ref, kw0_ref, vw0_ref),
              (kc1_ref, vc1_ref, ks1_ref, vs1_ref, kw1_ref, vw1_ref))
    o_cs, acc_ws = [], []
    for g, (kc_ref, vc_ref, ks_ref, vs_ref, kw_ref, vw_ref) in enumerate(groups):
        q_heads = [q_ref[0, :, (g * rep + r) * LANES:(g * rep + r + 1) * LANES] for r in range(rep)]
        qg = jnp.concatenate(q_heads, axis=0)

        s = with_bias(_dot_nt(qg, kc_ref[0]), bias_c)
        m = jnp.max(s, axis=-1, keepdims=True)
        p = (jnp.exp2(s - m).reshape(rep, qb, ncmp) * has_cmp[None]).reshape(rows, ncmp)
        l = jnp.sum(p, axis=-1, keepdims=True)
        p_c = p / jnp.where(l > 0.0, l, 1.0)
        o_cs.append(_dot(_bf(p_c), vc_ref[0]))
        p_sum = p_c[0:qb] + p_c[qb:2 * qb] + p_c[2 * qb:3 * qb] + p_c[3 * qb:4 * qb]
        imp = jnp.dot(p_sum, ovl_ref[...], preferred_element_type=jnp.float32,
                      precision=lax.Precision.HIGHEST)
        sel = _select_blocks(imp, t_col, seq // SLC_BLK)
        sel_bias = _bf(jnp.where(sel, 0.0, NEG))
        for r, qh in enumerate(q_heads):
            qa_scr[g, r * qb:(r + 1) * qb, :] = jnp.where(lane_low, qh, sel_bias)
        m_scr[g] = jnp.full((rows, LANES), NEG, jnp.float32)
        acc_scr[g] = jnp.zeros((rows, LANES), jnp.float32)

    def chunk_start(j):
        return pl.multiple_of(jnp.minimum(j * SEL_KB, seq - SEL_KB), SEL_KB)

    def produce(j, dst_scr, dst_max_scr):
        k0 = chunk_start(j)
        for g in range(NSA_GROUPS):
            s = _dot_nt(qa_scr[g], groups[g][2][0, pl.ds(k0, SEL_KB), :])
            dst_scr[g] = s
            dst_max_scr[g] = jnp.broadcast_to(jnp.max(s, axis=-1, keepdims=True), (rows, LANES))

    def consume(j, src_scr, src_max_scr, causal):
        k0 = chunk_start(j)
        if causal:
            tok = j * SEL_KB + lax.broadcasted_iota(jnp.int32, (1, SEL_KB), 1)
            bias = jnp.where(tok <= t_col, 0.0, NEG)
        for g in range(NSA_GROUPS):
            vs_ref = groups[g][3]
            m = m_scr[g]
            if causal:
                s = with_bias(src_scr[g], bias)
                m_new = jnp.maximum(m, jnp.max(s, axis=-1, keepdims=True))
            else:
                s = src_scr[g]
                m_new = jnp.maximum(m, src_max_scr[g])
            p = jnp.exp2(s - jnp.concatenate([m_new] * (SEL_KB // LANES), axis=1))
            pv = _dot(_bf(p), vs_ref[0, pl.ds(k0, SEL_KB), :])
            acc_scr[g] = jnp.exp2(m - m_new) * acc_scr[g] + pv
            m_scr[g] = m_new

    def loop_body(i, carry):
        produce(2 * i + 1, sb_scr, mxb_scr)
        consume(2 * i, sa_scr, mxa_scr, False)
        produce(2 * i + 2, sa_scr, mxa_scr)
        consume(2 * i + 1, sb_scr, mxb_scr, False)
        return carry

    n_pairs = n_full // 2
    produce(0, sa_scr, mxa_scr)
    lax.fori_loop(0, n_pairs, loop_body, 0)
    produce(2 * n_pairs + 1, sb_scr, mxb_scr)
    consume(2 * n_pairs, sa_scr, mxa_scr, True)

    consume(2 * n_pairs + 1, sb_scr, mxb_scr, True)
    carry = (None, acc_scr[0], None, acc_scr[1])

    for g in range(NSA_GROUPS):
        kw_ref, vw_ref = groups[g][4], groups[g][5]
        qg = jnp.concatenate(
            [q_ref[0, :, (g * rep + r) * LANES:(g * rep + r + 1) * LANES] for r in range(rep)], axis=0)
        s = with_bias(_dot_nt(qg, kw_ref[0, pl.ds(w_start, wk), :]), bias_w)
        m = jnp.max(s, axis=-1, keepdims=True)
        acc_ws.append(_dot(_bf(jnp.exp2(s - m)), vw_ref[0, pl.ds(w_start, wk), :]))

    def normalized(acc):
        return acc / pltpu.roll(acc, NSA_DH, axis=1)

    for g in range(NSA_GROUPS):
        o_c, o_s, o_w = o_cs[g], normalized(carry[2 * g + 1]), normalized(acc_ws[g])
        for r in range(rep):
            hd = g * rep + r
            sl = slice(r * qb, (r + 1) * qb)
            out = (o_c[sl] * gates[:, 3 * hd:3 * hd + 1] + o_s[sl] * gates[:, 3 * hd + 1:3 * hd + 2]
                   + o_w[sl] * gates[:, 3 * hd + 2:3 * hd + 3])
            y_ref[0, :, hd * LANES:(hd + 1) * LANES] = _bf(jnp.where(lane_low, out, 0.0))


def _nsa_attn(q_rot, kcs, kvs, z3, ovl):
    bsz, t, qw = q_rot.shape
    ncmp = kcs[0].shape[1]
    assert t % SEL_KB == 0 and t >= WIN + Q_BLOCK and t // SLC_BLK <= LANES - NSA_DH
    gblk = C_NGATE // LANES
    per_b = lambda rows: pl.BlockSpec((1, rows, LANES), lambda b, c: (b, 0, 0))
    return pl.pallas_call(
        functools.partial(_nsa_attn_kernel, seq=t),
        grid=(bsz, t // Q_BLOCK),
        in_specs=[pl.BlockSpec((1, Q_BLOCK, qw), lambda b, c: (b, c, 0))]
        + [per_b(ncmp)] * 4 + [per_b(t)] * 8
        + [pl.BlockSpec((1, Q_BLOCK, LANES), lambda b, c: (b, c, gblk)),
           pl.BlockSpec(ovl.shape, lambda b, c: (0, 0))],
        out_specs=pl.BlockSpec((1, Q_BLOCK, qw), lambda b, c: (b, c, 0)),
        out_shape=jax.ShapeDtypeStruct((bsz, t, qw), jnp.bfloat16),
        scratch_shapes=[pltpu.VMEM((NSA_GROUPS, NSA_REP * Q_BLOCK, LANES), jnp.bfloat16),
                        pltpu.VMEM((NSA_GROUPS, NSA_REP * Q_BLOCK, LANES), jnp.float32),
                        pltpu.VMEM((NSA_GROUPS, NSA_REP * Q_BLOCK, LANES), jnp.float32),
                        pltpu.VMEM((NSA_GROUPS, NSA_REP * Q_BLOCK, SEL_KB), jnp.float32),
                        pltpu.VMEM((NSA_GROUPS, NSA_REP * Q_BLOCK, SEL_KB), jnp.float32),
                        pltpu.VMEM((NSA_GROUPS, NSA_REP * Q_BLOCK, LANES), jnp.float32),
                        pltpu.VMEM((NSA_GROUPS, NSA_REP * Q_BLOCK, LANES), jnp.float32)],
        compiler_params=pltpu.CompilerParams(
            dimension_semantics=("parallel", "arbitrary"), vmem_limit_bytes=VMEM_LIMIT),
        name="nsa_attn",
    )(q_rot, *kcs, *kvs, z3, ovl)


def _merge_kernel(x_ref, yh_ref, yg_ref, yn_ref, mg_ref, wh_ref, wg_ref, wn_ref, wo_ref, o_ref):
    d = D_MODEL
    m = _sigmoid(_f32(mg_ref[:, 0:d])) * _dot(yh_ref[...], wh_ref[...])
    m = m + _sigmoid(_f32(mg_ref[:, d:2 * d])) * _dot(yg_ref[...], wg_ref[...])
    m = m + _sigmoid(_f32(mg_ref[:, 2 * d:3 * d])) * _dot(yn_ref[...], wn_ref[...])
    o_ref[...] = x_ref[...] + _dot(_bf(m), wo_ref[...])


def _merge(x2, yh, yg, yn, z2, wh, wg, wn, wo, tm=512):
    n, d = x2.shape
    mgblk = C_MG // (3 * d)
    row = lambda w: pl.BlockSpec((tm, w), lambda i: (i, 0))
    full = lambda a: pl.BlockSpec(a.shape, lambda i: (0, 0))
    return pl.pallas_call(
        _merge_kernel,
        grid=(n // tm,),
        in_specs=[row(d), row(yh.shape[1]), row(yg.shape[1]), row(yn.shape[1]),
                  pl.BlockSpec((tm, 3 * d), lambda i: (i, mgblk)),
                  full(wh), full(wg), full(wn), full(wo)],
        out_specs=row(d),
        out_shape=jax.ShapeDtypeStruct((n, d), jnp.float32),
        compiler_params=pltpu.CompilerParams(
            dimension_semantics=("parallel",), vmem_limit_bytes=VMEM_LIMIT),
        name="merge",
    )(x2, yh, yg, yn, z2, wh, wg, wn, wo)


def _ffn_kernel(x_ref, g_ref, wg_ref, wu_ref, wd_ref, fg_ref, o_ref, h_scr, acc_scr, *, final_norm):
    j = pl.program_id(1)

    @pl.when(j == 0)
    def _():
        x = x_ref[...]
        y = x * lax.rsqrt(jnp.mean(x * x, axis=-1, keepdims=True) + EPS)
        h_scr[...] = _bf(y * g_ref[...])
        acc_scr[...] = jnp.zeros_like(acc_scr)

    h = h_scr[...]
    a = _dot(h, wg_ref[...])
    u = _dot(h, wu_ref[...])
    acc_scr[...] += _dot(_bf(a * _sigmoid(a) * u), wd_ref[...])

    @pl.when(j == pl.num_programs(1) - 1)
    def _():
        out = x_ref[...] + acc_scr[...]
        if final_norm:
            out = out * lax.rsqrt(jnp.mean(out * out, axis=-1, keepdims=True) + EPS) * fg_ref[...]
        o_ref[...] = out


def _ffn(x2, g, wg, wu, wd, fg, final_norm, tm=1024, tf=256):
    n, d = x2.shape
    ff = wg.shape[1]
    return pl.pallas_call(
        functools.partial(_ffn_kernel, final_norm=final_norm),
        grid=(n // tm, ff // tf),
        in_specs=[pl.BlockSpec((tm, d), lambda i, j: (i, 0)),
                  pl.BlockSpec((1, d), lambda i, j: (0, 0)),
                  pl.BlockSpec((d, tf), lambda i, j: (0, j)),
                  pl.BlockSpec((d, tf), lambda i, j: (0, j)),
                  pl.BlockSpec((tf, d), lambda i, j: (j, 0)),
                  pl.BlockSpec((1, d), lambda i, j: (0, 0))],
        out_specs=pl.BlockSpec((tm, d), lambda i, j: (i, 0)),
        out_shape=jax.ShapeDtypeStruct((n, d), jnp.float32),
        scratch_shapes=[pltpu.VMEM((tm, d), jnp.bfloat16), pltpu.VMEM((tm, d), jnp.float32)],
        compiler_params=pltpu.CompilerParams(
            dimension_semantics=("parallel", "arbitrary"), vmem_limit_bytes=VMEM_LIMIT),
        name="ffn",
    )(x2, g, wg, wu, wd, fg)


def _pad_heads(w, heads, width):
    lead = w.shape[:-1]
    w = w.reshape(lead + (heads, width))
    w = jnp.pad(w, [(0, 0)] * len(lead) + [(0, 0), (0, LANES - width)])
    return w.reshape(lead + (heads * LANES,))


def _layout_w_in(w):
    d = w.shape[0]
    o = 0
    parts = {}
    for name, n in (("hq", 512), ("hf", 512), ("hi", 512), ("hog", 512), ("gq", 256), ("gk", 256),
                    ("gv", 512), ("glr", 16), ("gog", 512), ("nq", 512), ("nkc", 128), ("nvc", 128),
                    ("nks", 128), ("nvs", 128), ("nkw", 128), ("nvw", 128), ("ngate", 24), ("mg", 3072)):
        parts[name] = w[:, o:o + n]
        o += n
    nq = _pad_heads(parts["nq"], NSA_HEADS, NSA_DH)
    padc = lambda a: jnp.pad(a, ((0, 0), (0, LANES - a.shape[1])))
    cols = [parts["hq"], parts["hf"], parts["hi"], parts["hog"],
            _pad_heads(parts["gq"], GLA_HEADS, GLA_DK), _pad_heads(parts["gk"], GLA_HEADS, GLA_DK),
            parts["gv"], parts["gog"], nq,
            parts["nkc"], parts["nvc"], parts["nks"], parts["nvs"], parts["nkw"], parts["nvw"],
            padc(parts["ngate"]), padc(parts["glr"]), parts["mg"]]
    out = jnp.concatenate(cols, axis=1)
    assert out.shape[1] == Z_COLS
    return _bf(out)


def _blockdiag2(w):
    z = jnp.zeros_like(w)
    top = jnp.concatenate([w, z], axis=-1)
    bot = jnp.concatenate([z, w], axis=-1)
    return jnp.concatenate([top, bot], axis=-2)


def _rope_tables(t):
    pos = jnp.arange(t, dtype=jnp.float32)
    inv_freq = ROPE_THETA ** (-jnp.arange(0, ROT_DIM, 2, dtype=jnp.float32) / ROT_DIM)
    ang = pos[:, None] * inv_freq[None, :]
    cos, sin = jnp.cos(ang), jnp.sin(ang)
    half = ROT_DIM // 2
    lane = jnp.arange(LANES) % NSA_DH
    first = lane < half
    second = (lane >= half) & (lane < ROT_DIM)
    idx = jnp.where(first, lane, jnp.where(second, lane - half, 0))
    cos_t = jnp.where(first | second, cos[:, idx], 1.0)
    sin_a = jnp.where(first, -sin[:, idx], 0.0)
    sin_b = jnp.where(second, sin[:, idx], 0.0)
    return cos_t, sin_a, sin_b


def kernel(x, norm1_g, w_in, hg_lb, hg_norm_g, gla_w2, gla_b, gla_norm_g, cmp_pos_k, cmp_pos_v,
           cmp_w1_k, cmp_w2_k, cmp_w1_v, cmp_w2_v, w_br_hg, w_br_gla, w_br_nsa, w_out,
           norm2_g, w_ffn_gate, w_ffn_up, w_ffn_down, final_norm_g):
    bsz, t, d = x.shape
    depth = w_in.shape[0]
    n = bsz * t
    f32 = jnp.float32

    cos_t, sin_a, sin_b = _rope_tables(t)
    lbs = jnp.cumsum(jax.nn.softmax(hg_lb.astype(f32), axis=0), axis=0)
    lbs = lbs - lbs[0]

    nslc = t // SLC_BLK
    ncmp_pad = t // CMP_STRIDE
    cmp_start = jnp.arange(ncmp_pad) * CMP_STRIDE
    blk = jnp.arange(LANES) - NSA_DH
    ovl = ((cmp_start[:, None] < (blk[None, :] + 1) * SLC_BLK)
           & (cmp_start[:, None] + CMP_BLK - 1 >= blk[None, :] * SLC_BLK)
           & (blk[None, :] >= 0) & (blk[None, :] < nslc)
           & (jnp.arange(ncmp_pad)[:, None] < ncmp_pad - 1)).astype(f32)

    x2 = x.reshape(n, d)
    for l in range(depth):
        z2 = _inproj(x2, norm1_g[l][None, :], _layout_w_in(w_in[l]))
        z3 = z2.reshape(bsz, t, Z_COLS)

        lb = lbs[l].reshape(HG_HEADS, LANES)
        zeros = jnp.zeros_like(lb)
        hg_par = jnp.stack([jnp.log(lb), jnp.log1p(-lb), 1.0 - lb,
                            hg_norm_g[l].reshape(HG_HEADS, LANES), zeros, zeros, zeros, zeros], axis=1)
        y_hg = _hgrn2(z3, hg_par)

        w2p = jnp.pad(_pad_heads(gla_w2[l], GLA_HEADS, GLA_DK), ((0, LANES - GLA_RANK), (0, 0)))
        w2p = _bf(w2p.reshape(LANES, GLA_HEADS, LANES).transpose(1, 0, 2))
        gb = _pad_heads(gla_b[l], GLA_HEADS, GLA_DK).reshape(GLA_HEADS, LANES)
        gla_par = jnp.stack([gb, gla_norm_g[l].reshape(GLA_HEADS, LANES),
                             zeros, zeros, zeros, zeros, zeros, zeros], axis=1)
        y_gla = _gla(z3, w2p, gla_par)

        q_rot, kc_rot, vc_raw, *kvs = _nsa_prep(z3, cos_t, sin_a, sin_b)
        tile2 = lambda p: jnp.concatenate([p, p], axis=1)
        w1k = _bf(_blockdiag2(cmp_w1_k[l].reshape(CMP_BLK, NSA_DH, CMP_HID)))
        w1v = _bf(_blockdiag2(cmp_w1_v[l].reshape(CMP_BLK, NSA_DH, CMP_HID)))
        kcs = _compress(kc_rot, vc_raw, tile2(cmp_pos_k[l]), tile2(cmp_pos_v[l]),
                        w1k, _bf(_blockdiag2(cmp_w2_k[l])), w1v, _bf(_blockdiag2(cmp_w2_v[l])))
        y_nsa = _nsa_attn(q_rot, kcs, kvs, z3, ovl)

        wn = _pad_heads(w_br_nsa[l].T, NSA_HEADS, NSA_DH).T
        x2 = _merge(x2, y_hg.reshape(n, HG_DIM), y_gla.reshape(n, GLA_VDIM),
                    y_nsa.reshape(n, NSA_HEADS * LANES), z2,
                    _bf(w_br_hg[l]), _bf(w_br_gla[l]), _bf(wn), _bf(w_out[l]))

        x2 = _ffn(x2, norm2_g[l][None, :], _bf(w_ffn_gate[l]), _bf(w_ffn_up[l]), _bf(w_ffn_down[l]),
                  final_norm_g[None, :], final_norm=(l == depth - 1))
    return x2.reshape(bsz, t, d)
```
